```python
import jax, jax.numpy as jnp
from jax import lax
import numpy as np

D_MODEL = 1024
BATCH = 8
SEQ = 2048
DEPTH = 2

EPS = 1e-6
NEG_INF = -1e30
FORCE_SCORE = 1e4
N_BRANCH = 3

GMLP_WIDTH = 1024
GMLP_GROUPS = 4
GMLP_GROUP_DIM = GMLP_WIDTH // GMLP_GROUPS
GMLP_CHUNK = 128

NSA_HEADS = 16
NSA_KV_GROUPS = 4
NSA_HEAD_DIM = 64
NSA_HPG = NSA_HEADS // NSA_KV_GROUPS
NSA_Q_WIDTH = NSA_HEADS * NSA_HEAD_DIM
NSA_KV_WIDTH = NSA_KV_GROUPS * NSA_HEAD_DIM
CMP_BLOCK = 32
CMP_STRIDE = 16
CMP_HIDDEN = 256
SEL_BLOCK = 64
SEL_TOP_N = 16
WINDOW = 512
NSA_Q_CHUNK = 64

RNN_WIDTH = 1024
RNN_HEADS = 16
RNN_HEAD_DIM = RNN_WIDTH // RNN_HEADS
CONV_WIDTH = 4
LRU_C = 8.0

D_FF = -(-8 * D_MODEL // (3 * 256)) * 256

IN_SPLITS = (GMLP_WIDTH, GMLP_WIDTH, NSA_Q_WIDTH, 6 * NSA_KV_WIDTH, N_BRANCH * NSA_HEADS, RNN_WIDTH, RNN_WIDTH, N_BRANCH * D_MODEL)
D_IN = sum(IN_SPLITS)

kernel_name = "hybrid_gmlp_nsa_rglru_block"


def rms_norm(x, g):
    xf = x.astype(jnp.float32)
    y = xf * lax.rsqrt(jnp.mean(xf * xf, axis=-1, keepdims=True) + EPS)
    return (y * g.astype(jnp.float32)).astype(x.dtype)


def layer_norm(x, g, b):
    xf = x.astype(jnp.float32)
    mu = jnp.mean(xf, axis=-1, keepdims=True)
    var = jnp.mean(jnp.square(xf - mu), axis=-1, keepdims=True)
    y = (xf - mu) * lax.rsqrt(var + EPS) * g.astype(jnp.float32) + b.astype(jnp.float32)
    return y.astype(x.dtype)


def gmlp_mixer(u, v, ln_g, ln_b, w_s, b_s):
    B, S, _ = u.shape
    n_chunks = S // GMLP_CHUNK
    vn = layer_norm(v, ln_g, ln_b).reshape(B, n_chunks, GMLP_CHUNK, GMLP_GROUPS, GMLP_GROUP_DIM)
    causal = jnp.tril(jnp.ones((GMLP_CHUNK, GMLP_CHUNK), w_s.dtype))
    w = w_s * causal
    mixed = jnp.einsum('gts,bnsgd->bntgd', w, vn) + b_s.T[None, None, :, :, None]
    return u * mixed.reshape(B, S, GMLP_WIDTH)


def nsa_mixer(q, k_cmp, v_cmp, k_slc, v_slc, k_win, v_win, branch_gates,
              pe_k, pe_v, wk1, wk2, wv1, wv2):
    B, S, _ = q.shape
    G, HP, DH = NSA_KV_GROUPS, NSA_HPG, NSA_HEAD_DIM
    scale = DH ** -0.5
    qh = q.reshape(B, S, G, HP, DH).transpose(0, 2, 3, 1, 4)

    def kv_heads(t):
        return t.reshape(B, S, G, DH).transpose(0, 2, 1, 3)

    k_cmp, v_cmp, k_slc, v_slc, k_win, v_win = map(kv_heads, (k_cmp, v_cmp, k_slc, v_slc, k_win, v_win))
    pos = jnp.arange(S)

    n_cmp = (S - CMP_BLOCK) // CMP_STRIDE + 1
    blk_idx = jnp.arange(n_cmp)[:, None] * CMP_STRIDE + jnp.arange(CMP_BLOCK)[None, :]

    def compress(t, pe, w1, w2):
        blocks = t[:, :, blk_idx] + pe
        flat = blocks.reshape(B, G, n_cmp, CMP_BLOCK * DH)
        return jax.nn.gelu(flat @ w1) @ w2

    kc = compress(k_cmp, pe_k, wk1, wk2)
    vc = compress(v_cmp, pe_v, wv1, wv2)
    cmp_end = jnp.arange(n_cmp) * CMP_STRIDE + CMP_BLOCK - 1
    valid_c = cmp_end[None, :] <= pos[:, None]
    s_c = jnp.einsum('bghsd,bgnd->bghsn', qh, kc).astype(jnp.float32) * scale
    s_c = jnp.where(valid_c, s_c, NEG_INF)
    p_c = jax.nn.softmax(s_c, axis=-1) * jnp.any(valid_c, axis=-1)[:, None].astype(jnp.float32)
    o_cmp = jnp.einsum('bghsn,bgnd->bghsd', p_c.astype(vc.dtype), vc)

    n_sel = S // SEL_BLOCK
    c_start = np.arange(n_cmp) * CMP_STRIDE
    s_start = np.arange(n_sel) * SEL_BLOCK
    overlap = np.clip(np.minimum(c_start[:, None] + CMP_BLOCK, s_start[None, :] + SEL_BLOCK)
                      - np.maximum(c_start[:, None], s_start[None, :]), 0, None) / CMP_BLOCK
    overlap = jnp.asarray(overlap, jnp.float32)
    imp = jnp.einsum('bghsn,nj->bgsj', p_c, overlap)
    cur = (pos // SEL_BLOCK)[:, None]
    j = jnp.arange(n_sel)[None, :]
    forced = (j == 0) | (j == cur) | (j == cur - 1)
    imp = jnp.where(forced, FORCE_SCORE, jnp.where(j > cur, NEG_INF, imp))
    top_n = min(SEL_TOP_N, n_sel)
    _, sel_idx = lax.top_k(imp, top_n)

    k_sb = k_slc.reshape(B, G, n_sel, SEL_BLOCK, DH)
    v_sb = v_slc.reshape(B, G, n_sel, SEL_BLOCK, DH)
    pad = jnp.zeros((B, G, WINDOW, DH), k_win.dtype)
    k_wp = jnp.concatenate([pad, k_win], axis=2)
    v_wp = jnp.concatenate([pad, v_win], axis=2)
    gather = jax.vmap(jax.vmap(lambda blocks, ix: blocks[ix]))

    C = NSA_Q_CHUNK
    n_chunks = S // C
    q_ch = jnp.moveaxis(qh.reshape(B, G, HP, n_chunks, C, DH), 3, 0)
    idx_ch = jnp.moveaxis(sel_idx.reshape(B, G, n_chunks, C, top_n), 2, 0)

    def chunk_fn(args):
        c, qc, ic = args
        t = c * C + jnp.arange(C)
        kg = gather(k_sb, ic)
        vg = gather(v_sb, ic)
        kpos = ic[..., None] * SEL_BLOCK + jnp.arange(SEL_BLOCK)
        s = jnp.einsum('bghcd,bgckld->bghckl', qc, kg).astype(jnp.float32) * scale
        s = jnp.where((kpos <= t[:, None, None])[:, :, None], s, NEG_INF)
        p = jax.nn.softmax(s.reshape(B, G, HP, C, top_n * SEL_BLOCK), axis=-1).reshape(s.shape)
        o_s = jnp.einsum('bghckl,bgckld->bghcd', p.astype(vg.dtype), vg)
        kw = lax.dynamic_slice_in_dim(k_wp, c * C, WINDOW + C, axis=2)
        vw = lax.dynamic_slice_in_dim(v_wp, c * C, WINDOW + C, axis=2)
        wpos = c * C - WINDOW + jnp.arange(WINDOW + C)
        wmask = (wpos[None, :] >= 0) & (wpos[None, :] <= t[:, None]) & (wpos[None, :] > t[:, None] - WINDOW)
        sw = jnp.einsum('bghcd,bgkd->bghck', qc, kw).astype(jnp.float32) * scale
        sw = jnp.where(wmask, sw, NEG_INF)
        o_w = jnp.einsum('bghck,bgkd->bghcd', jax.nn.softmax(sw, axis=-1).astype(vw.dtype), vw)
        return o_s, o_w

    o_slc, o_win = lax.map(chunk_fn, (jnp.arange(n_chunks), q_ch, idx_ch))
    o_slc = jnp.moveaxis(o_slc, 0, 3).reshape(B, G, HP, S, DH)
    o_win = jnp.moveaxis(o_win, 0, 3).reshape(B, G, HP, S, DH)

    g = jax.nn.sigmoid(branch_gates).reshape(B, S, G, HP, N_BRANCH).transpose(0, 2, 3, 1, 4)
    o = g[..., 0:1] * o_cmp + g[..., 1:2] * o_slc + g[..., 2:3] * o_win
    return o.transpose(0, 3, 1, 2, 4).reshape(B, S, NSA_Q_WIDTH)


def rglru_mixer(xr, gate, conv_w, conv_b, w_a, b_a, w_x, b_x, lam):
    B, S, _ = xr.shape
    xc = lax.conv_general_dilated(xr, conv_w[:, None, :], window_strides=(1,),
                                  padding=[(CONV_WIDTH - 1, 0)],
                                  dimension_numbers=('NWC', 'WIO', 'NWC'),
                                  feature_group_count=RNN_WIDTH) + conv_b
    xh = xc.reshape(B, S, RNN_HEADS, RNN_HEAD_DIM)
    r = jax.nn.sigmoid(jnp.einsum('bshi,hio->bsho', xh, w_a).reshape(B, S, RNN_WIDTH) + b_a)
    i = jax.nn.sigmoid(jnp.einsum('bshi,hio->bsho', xh, w_x).reshape(B, S, RNN_WIDTH) + b_x)
    log_a = -LRU_C * r.astype(jnp.float32) * jax.nn.softplus(-lam.astype(jnp.float32))
    a = jnp.exp(log_a)
    b_in = jnp.sqrt(-jnp.expm1(2.0 * log_a)) * (i * xc).astype(jnp.float32)

    def combine(left, right):
        a1, b1 = left
        a2, b2 = right
        return a2 * a1, a2 * b1 + b2

    _, h = lax.associative_scan(combine, (a, b_in), axis=1)
    return h.astype(xr.dtype) * jax.nn.gelu(gate)


def hybrid_layer(x, g_pre_mix, g_post_mix, g_pre_ffn, g_post_ffn, w_in,
                 gmlp_ln_g, gmlp_ln_b, gmlp_ws, gmlp_bs,
                 nsa_pe_k, nsa_pe_v, nsa_wk1, nsa_wk2, nsa_wv1, nsa_wv2,
                 rnn_conv_w, rnn_conv_b, rnn_wa, rnn_ba, rnn_wx, rnn_bx, rnn_lam,
                 w_br_a, w_br_b, w_br_c, w_o, w_ffn_in, w_ffn_out):
    B, S, _ = x.shape
    h = rms_norm(x, g_pre_mix)
    z = h @ w_in
    split_points = [int(p) for p in np.cumsum(IN_SPLITS)[:-1]]
    u, v, q, kv, nsa_g, xr, rg, mg = jnp.split(z, split_points, axis=-1)
    k_cmp, v_cmp, k_slc, v_slc, k_win, v_win = jnp.split(kv, 6, axis=-1)

    y_a = gmlp_mixer(jax.nn.gelu(u), jax.nn.gelu(v), gmlp_ln_g, gmlp_ln_b, gmlp_ws, gmlp_bs)
    y_b = nsa_mixer(q, k_cmp, v_cmp, k_slc, v_slc, k_win, v_win, nsa_g,
                    nsa_pe_k, nsa_pe_v, nsa_wk1, nsa_wk2, nsa_wv1, nsa_wv2)
    y_c = rglru_mixer(xr, rg, rnn_conv_w, rnn_conv_b, rnn_wa, rnn_ba, rnn_wx, rnn_bx, rnn_lam)

    gates = jax.nn.sigmoid(mg).reshape(B, S, N_BRANCH, D_MODEL)
    merged = (gates[:, :, 0] * (y_a @ w_br_a)
              + gates[:, :, 1] * (y_b @ w_br_b)
              + gates[:, :, 2] * (y_c @ w_br_c))
    x = x + rms_norm(merged @ w_o, g_post_mix)

    hf = rms_norm(x, g_pre_ffn)
    f_gate, f_up = jnp.split(hf @ w_ffn_in, 2, axis=-1)
    f = (jax.nn.silu(f_gate) * f_up) @ w_ffn_out
    return x + rms_norm(f, g_post_ffn)


def setup_inputs(seed: int = 0) -> dict:
    key = jax.random.key(seed)
    keys = iter(jax.random.split(key, 48))
    L = DEPTH

    def nrm(shape, scale):
        return jax.random.normal(next(keys), shape, jnp.float32) * scale

    def gain(n):
        return 1.0 + nrm((L, n), 0.05)

    u_lam = jax.random.uniform(next(keys), (L, RNN_WIDTH), jnp.float32, minval=0.9, maxval=0.999)
    a0 = u_lam ** (1.0 / LRU_C)
    rnn_lam = jnp.log(a0) - jnp.log1p(-a0)

    return {
        "x": nrm((BATCH, SEQ, D_MODEL), 1.0),
        "g_pre_mix": gain(D_MODEL),
        "g_post_mix": gain(D_MODEL),
        "g_pre_ffn": gain(D_MODEL),
        "g_post_ffn": gain(D_MODEL),
        "w_in": nrm((L, D_MODEL, D_IN), D_MODEL ** -0.5),
        "gmlp_ln_g": gain(GMLP_WIDTH),
        "gmlp_ln_b": nrm((L, GMLP_WIDTH), 0.05),
        "gmlp_ws": nrm((L, GMLP_GROUPS, GMLP_CHUNK, GMLP_CHUNK), GMLP_CHUNK ** -0.5),
        "gmlp_bs": 1.0 + nrm((L, GMLP_GROUPS, GMLP_CHUNK), 0.1),
        "nsa_pe_k": nrm((L, CMP_BLOCK, NSA_HEAD_DIM), 0.1),
        "nsa_pe_v": nrm((L, CMP_BLOCK, NSA_HEAD_DIM), 0.1),
        "nsa_wk1": nrm((L, CMP_BLOCK * NSA_HEAD_DIM, CMP_HIDDEN), (CMP_BLOCK * NSA_HEAD_DIM) ** -0.5),
        "nsa_wk2": nrm((L, CMP_HIDDEN, NSA_HEAD_DIM), CMP_HIDDEN ** -0.5),
        "nsa_wv1": nrm((L, CMP_BLOCK * NSA_HEAD_DIM, CMP_HIDDEN), (CMP_BLOCK * NSA_HEAD_DIM) ** -0.5),
        "nsa_wv2": nrm((L, CMP_HIDDEN, NSA_HEAD_DIM), CMP_HIDDEN ** -0.5),
        "rnn_conv_w": nrm((L, CONV_WIDTH, RNN_WIDTH), CONV_WIDTH ** -0.5),
        "rnn_conv_b": nrm((L, RNN_WIDTH), 0.05),
        "rnn_wa": nrm((L, RNN_HEADS, RNN_HEAD_DIM, RNN_HEAD_DIM), RNN_HEAD_DIM ** -0.5),
        "rnn_ba": nrm((L, RNN_WIDTH), 0.1),
        "rnn_wx": nrm((L, RNN_HEADS, RNN_HEAD_DIM, RNN_HEAD_DIM), RNN_HEAD_DIM ** -0.5),
        "rnn_bx": nrm((L, RNN_WIDTH), 0.1),
        "rnn_lam": rnn_lam,
        "w_br_a": nrm((L, GMLP_WIDTH, D_MODEL), GMLP_WIDTH ** -0.5),
        "w_br_b": nrm((L, NSA_Q_WIDTH, D_MODEL), NSA_Q_WIDTH ** -0.5),
        "w_br_c": nrm((L, RNN_WIDTH, D_MODEL), RNN_WIDTH ** -0.5),
        "w_o": nrm((L, D_MODEL, D_MODEL), D_MODEL ** -0.5),
        "w_ffn_in": nrm((L, D_MODEL, 2 * D_FF), D_MODEL ** -0.5),
        "w_ffn_out": nrm((L, D_FF, D_MODEL), D_FF ** -0.5),
    }


def reference(x, g_pre_mix, g_post_mix, g_pre_ffn, g_post_ffn, w_in,
              gmlp_ln_g, gmlp_ln_b, gmlp_ws, gmlp_bs,
              nsa_pe_k, nsa_pe_v, nsa_wk1, nsa_wk2, nsa_wv1, nsa_wv2,
              rnn_conv_w, rnn_conv_b, rnn_wa, rnn_ba, rnn_wx, rnn_bx, rnn_lam,
              w_br_a, w_br_b, w_br_c, w_o, w_ffn_in, w_ffn_out):
    for l in range(DEPTH):
        x = hybrid_layer(x, g_pre_mix[l], g_post_mix[l], g_pre_ffn[l], g_post_ffn[l], w_in[l],
                         gmlp_ln_g[l], gmlp_ln_b[l], gmlp_ws[l], gmlp_bs[l],
                         nsa_pe_k[l], nsa_pe_v[l], nsa_wk1[l], nsa_wk2[l], nsa_wv1[l], nsa_wv2[l],
                         rnn_conv_w[l], rnn_conv_b[l], rnn_wa[l], rnn_ba[l], rnn_wx[l], rnn_bx[l], rnn_lam[l],
                         w_br_a[l], w_br_b[l], w_br_c[l], w_o[l], w_ffn_in[l], w_ffn_out[l])
    return x
```

```python
import functools

import jax
import jax.numpy as jnp
import numpy as np
from jax import lax
from jax.experimental import pallas as pl
from jax.experimental.pallas import tpu as pltpu

F32 = jnp.float32
BF16 = jnp.bfloat16

EPS = 1e-6
NEG_INF = -1e30
FORCE_SCORE = 1e4

D_MODEL = 1024
GMLP_WIDTH = 1024
GMLP_GROUPS = 4
GMLP_GROUP_DIM = GMLP_WIDTH // GMLP_GROUPS
GMLP_CHUNK = 128

NSA_HEADS = 16
NSA_KV_GROUPS = 4
NSA_HEAD_DIM = 64
NSA_HPG = NSA_HEADS // NSA_KV_GROUPS
NSA_Q_WIDTH = NSA_HEADS * NSA_HEAD_DIM
NSA_KV_WIDTH = NSA_KV_GROUPS * NSA_HEAD_DIM
N_BRANCH = 3
CMP_BLOCK = 32
CMP_STRIDE = 16
CMP_HIDDEN = 256
SEL_BLOCK = 64
SEL_TOP_N = 16
WINDOW = 512

RNN_WIDTH = 1024
RNN_HEADS = 16
RNN_HEAD_DIM = RNN_WIDTH // RNN_HEADS
CONV_WIDTH = 4
LRU_C = 8.0
RNN_BLOCK = 256

D_FF = 2816

OFF_U = 0
OFF_V = 1024
OFF_Q = 2048
OFF_XR = 3072
OFF_RG = 4096
OFF_MG = 5120
OFF_KV = 8192
OFF_NG = OFF_KV + 6 * NSA_KV_WIDTH
D_IN_PAD = 9984

LANES = 128
V7X_VMEM_LIMIT = 56 * 1024 * 1024


def _cparams(sem, vmem=V7X_VMEM_LIMIT):
    return pltpu.CompilerParams(dimension_semantics=sem, vmem_limit_bytes=vmem)


def _rms(x, g):
    ms = jnp.mean(x * x, axis=-1, keepdims=True)
    return x * lax.rsqrt(ms + EPS) * g


def _gelu(x):
    return jax.nn.gelu(x)


def _sigmoid(x):
    return jax.nn.sigmoid(x)


def _in_proj_kernel(x_ref, g_ref, w_ref, o_ref, h_scr):
    @pl.when(pl.program_id(1) == 0)
    def _():
        h_scr[...] = _rms(x_ref[...], g_ref[...]).astype(BF16)

    o_ref[...] = jnp.dot(h_scr[...], w_ref[...], preferred_element_type=F32)


def _in_proj(x2d, g, w, *, tm=1024, tn=768):
    T, D = x2d.shape
    N = w.shape[1]
    return pl.pallas_call(
        _in_proj_kernel,
        grid=(T // tm, N // tn),
        in_specs=[
            pl.BlockSpec((tm, D), lambda i, j: (i, 0)),
            pl.BlockSpec((1, D), lambda i, j: (0, 0)),
            pl.BlockSpec((D, tn), lambda i, j: (0, j)),
        ],
        out_specs=pl.BlockSpec((tm, tn), lambda i, j: (i, j)),
        out_shape=jax.ShapeDtypeStruct((T, N), F32),
        scratch_shapes=[pltpu.VMEM((tm, D), BF16)],
        compiler_params=_cparams(("parallel", "arbitrary")),
        name="in_proj",
    )(x2d, g, w)


def _gmlp_kernel(u_ref, v_ref, lng_ref, lnb_ref, ws_ref, bst_ref, o_ref, *, n_chunks):
    C = GMLP_CHUNK
    row = lax.broadcasted_iota(jnp.int32, (C, C), 0)
    col = lax.broadcasted_iota(jnp.int32, (C, C), 1)
    causal = col <= row
    ws = [jnp.where(causal, ws_ref[g], 0.0).astype(BF16) for g in range(GMLP_GROUPS)]
    bst = bst_ref[...]
    for c in range(n_chunks):
        rows = slice(c * C, (c + 1) * C)
        gv = _gelu(v_ref[rows, :])
        mu = jnp.mean(gv, axis=-1, keepdims=True)
        d = gv - mu
        var = jnp.mean(d * d, axis=-1, keepdims=True)
        vn = (d * lax.rsqrt(var + EPS) * lng_ref[...] + lnb_ref[...]).astype(BF16)
        for g in range(GMLP_GROUPS):
            cols = slice(g * GMLP_GROUP_DIM, (g + 1) * GMLP_GROUP_DIM)
            mixed = jnp.dot(ws[g], vn[:, cols], preferred_element_type=F32) + bst[:, g:g + 1]
            o_ref[rows, cols] = (_gelu(u_ref[rows, cols]) * mixed).astype(o_ref.dtype)


def _gmlp(z, lng, lnb, ws, bst, *, n_chunks=4):
    T = z.shape[0]
    tm = GMLP_CHUNK * n_chunks
    W = GMLP_WIDTH
    return pl.pallas_call(
        functools.partial(_gmlp_kernel, n_chunks=n_chunks),
        grid=(T // tm,),
        in_specs=[
            pl.BlockSpec((tm, W), lambda i: (i, OFF_U // W)),
            pl.BlockSpec((tm, W), lambda i: (i, OFF_V // W)),
            pl.BlockSpec((1, W), lambda i: (0, 0)),
            pl.BlockSpec((1, W), lambda i: (0, 0)),
            pl.BlockSpec((GMLP_GROUPS, GMLP_CHUNK, GMLP_CHUNK), lambda i: (0, 0, 0)),
            pl.BlockSpec((GMLP_CHUNK, GMLP_GROUPS), lambda i: (0, 0)),
        ],
        out_specs=pl.BlockSpec((tm, W), lambda i: (i, 0)),
        out_shape=jax.ShapeDtypeStruct((T, W), BF16),
        compiler_params=_cparams(("parallel",)),
        name="gmlp",
    )(z, z, lng, lnb, ws, bst)


def _rglru_kernel(xr_ref, rg_ref, cw_ref, cb_ref, wab_ref, ba_ref, bx_ref, lam_ref, o_ref,
                  ext_scr, a_scr, b_scr, hc_scr, *, tt):
    W = RNN_WIDTH
    seg = tt // 8

    @pl.when(pl.program_id(1) == 0)
    def _():
        ext_scr[0:8, :] = jnp.zeros((8, W), F32)
        hc_scr[...] = jnp.zeros((8, W), F32)

    xr = xr_ref[...]
    ext_scr[8:8 + tt, :] = xr
    cw = cw_ref[...]
    xc = (cw[3:4] * xr + cw[2:3] * ext_scr[7:7 + tt, :] + cw[1:2] * ext_scr[6:6 + tt, :]
          + cw[0:1] * ext_scr[5:5 + tt, :] + cb_ref[...])
    ext_scr[0:8, :] = xr[tt - 8:tt, :]

    lam = lam_ref[...]
    neg = -lam
    softplus = jnp.maximum(neg, 0.0) + jnp.log1p(jnp.exp(-jnp.abs(neg)))
    for k in range(W // RNN_BLOCK):
        cols = slice(k * RNN_BLOCK, (k + 1) * RNN_BLOCK)
        xck = xc[:, cols]
        gates = jnp.dot(xck.astype(BF16), wab_ref[k], preferred_element_type=F32)
        r = _sigmoid(gates[:, :RNN_BLOCK] + ba_ref[:, cols])
        i = _sigmoid(gates[:, RNN_BLOCK:] + bx_ref[:, cols])
        log_a = -LRU_C * r * softplus[:, cols]
        a = jnp.exp(log_a)
        one_minus_a2 = -jnp.tanh(log_a) * (1.0 + a * a)
        b_in = jnp.sqrt(one_minus_a2) * (i * xck)
        for c in range(RNN_BLOCK // LANES):
            a_scr[k * (RNN_BLOCK // LANES) + c] = a[:, c * LANES:(c + 1) * LANES]
            b_scr[k * (RNN_BLOCK // LANES) + c] = b_in[:, c * LANES:(c + 1) * LANES]

    n_lane_blocks = W // LANES

    def scan_step(j, carry):
        hs, ps = carry
        new_h, new_p = [], []
        for c in range(n_lane_blocks):
            a = a_scr[c, pl.ds(j, 8, stride=seg), :]
            b = b_scr[c, pl.ds(j, 8, stride=seg), :]
            h = a * hs[c] + b
            p = a * ps[c]
            b_scr[c, pl.ds(j, 8, stride=seg), :] = h
            a_scr[c, pl.ds(j, 8, stride=seg), :] = p
            new_h.append(h)
            new_p.append(p)
        return tuple(new_h), tuple(new_p)

    init = (tuple(jnp.zeros((8, LANES), F32) for _ in range(n_lane_blocks)),
            tuple(jnp.ones((8, LANES), F32) for _ in range(n_lane_blocks)))
    h_loc, p_all = lax.fori_loop(0, seg, scan_step, init)

    for c in range(n_lane_blocks):
        cols = slice(c * LANES, (c + 1) * LANES)
        h_in = hc_scr[0:1, cols]
        for s in range(8):
            rows = slice(s * seg, (s + 1) * seg)
            h = b_scr[c, rows, :] + a_scr[c, rows, :] * h_in
            o_ref[rows, cols] = (h * _gelu(rg_ref[rows, cols])).astype(o_ref.dtype)
            h_in = h_loc[c][s:s + 1, :] + p_all[c][s:s + 1, :] * h_in
        hc_scr[:, cols] = jnp.broadcast_to(h_in, (8, LANES))


def _rglru(z, cw, cb, wab, ba, bx, lam, *, B, S, tt=512):
    T = z.shape[0]
    W = RNN_WIDTH
    nt = S // tt
    return pl.pallas_call(
        functools.partial(_rglru_kernel, tt=tt),
        grid=(B, nt),
        in_specs=[
            pl.BlockSpec((tt, W), lambda b, t: (b * nt + t, OFF_XR // W)),
            pl.BlockSpec((tt, W), lambda b, t: (b * nt + t, OFF_RG // W)),
            pl.BlockSpec((CONV_WIDTH, W), lambda b, t: (0, 0)),
            pl.BlockSpec((1, W), lambda b, t: (0, 0)),
            pl.BlockSpec((W // RNN_BLOCK, RNN_BLOCK, 2 * RNN_BLOCK), lambda b, t: (0, 0, 0)),
            pl.BlockSpec((1, W), lambda b, t: (0, 0)),
            pl.BlockSpec((1, W), lambda b, t: (0, 0)),
            pl.BlockSpec((1, W), lambda b, t: (0, 0)),
        ],
        out_specs=pl.BlockSpec((tt, W), lambda b, t: (b * nt + t, 0)),
        out_shape=jax.ShapeDtypeStruct((T, W), BF16),
        scratch_shapes=[
            pltpu.VMEM((tt + 8, W), F32),
            pltpu.VMEM((W // LANES, tt, LANES), F32),
            pltpu.VMEM((W // LANES, tt, LANES), F32),
            pltpu.VMEM((8, W), F32),
        ],
        compiler_params=_cparams(("parallel", "arbitrary")),
        name="rglru",
    )(z, z, cw, cb, wab, ba, bx, lam)


def _compress_kernel(hk_ref, hv_ref, pek_ref, pev_ref, wk1_ref, wk2_ref, wv1_ref, wv2_ref, kc_ref, vc_ref):
    half = CMP_STRIDE * NSA_HEAD_DIM
    for h_ref, pe_ref, w1_ref, w2_ref, o_ref in ((hk_ref, pek_ref, wk1_ref, wk2_ref, kc_ref),
                                                 (hv_ref, pev_ref, wv1_ref, wv2_ref, vc_ref)):
        hm = h_ref[...]
        rows = hm.shape[0]
        lo = jnp.dot(hm, w1_ref[0:half, :], preferred_element_type=F32)
        hi = jnp.dot(hm, w1_ref[half:2 * half, :], preferred_element_type=F32)
        pe = jnp.broadcast_to(pe_ref[...], (8, 2 * half)).astype(BF16)
        pe_term = jnp.dot(pe, w1_ref[...], preferred_element_type=F32)[0:1, :]
        hid = _gelu(lo + pltpu.roll(hi, rows - 1, axis=0) + pe_term)
        o_ref[...] = jnp.dot(hid.astype(BF16), w2_ref[...], preferred_element_type=F32)


def _compress(hk, hv, pek, pev, wk1, wk2, wv1, wv2, *, tr=512):
    R, K = hk.shape
    full = lambda shape: pl.BlockSpec(shape, lambda i: (0,) * len(shape))
    out = jax.ShapeDtypeStruct((R, NSA_HEAD_DIM), F32)
    return pl.pallas_call(
        _compress_kernel,
        grid=(R // tr,),
        in_specs=[
            pl.BlockSpec((tr, K), lambda i: (i, 0)),
            pl.BlockSpec((tr, K), lambda i: (i, 0)),
            full(pek.shape), full(pev.shape), full(wk1.shape), full(wk2.shape), full(wv1.shape), full(wv2.shape),
        ],
        out_specs=[pl.BlockSpec((tr, NSA_HEAD_DIM), lambda i: (i, 0))] * 2,
        out_shape=[out, out],
        compiler_params=_cparams(("parallel",)),
        name="nsa_compress",
    )(hk, hv, pek, pev, wk1, wk2, wv1, wv2)


def _nsa_kernel(q_ref, kc_ref, vc_ref, ks_ref, vs_ref, kw_ref, vw_ref, gate_ref, ov_ref, ex_ref, o_ref,
                bias_scr, m_scr, l_scr, acc_scr, *, tq, S):
    HP, DH = NSA_HPG, NSA_HEAD_DIM
    M = HP * tq
    tk = tq
    n_sel = S // SEL_BLOCK
    n_cmp_slots = kc_ref.shape[0]
    qi = pl.program_id(2)
    t0 = qi * tq
    nt_dims = (((1,), (1,)), ((), ()))

    q = q_ref[...].reshape(M, DH) * jnp.asarray(DH ** -0.5, BF16)

    s = lax.dot_general(q, kc_ref[...].astype(BF16), nt_dims, preferred_element_type=F32)
    s3 = s.reshape(HP, tq, n_cmp_slots)
    n_idx = lax.broadcasted_iota(jnp.int32, (tq, n_cmp_slots), 1)
    t_idx = t0 + lax.broadcasted_iota(jnp.int32, (tq, n_cmp_slots), 0)
    valid = (n_idx * CMP_STRIDE + (CMP_BLOCK - 1) <= t_idx)[None]
    s3 = jnp.where(valid, s3, NEG_INF)
    mx = jnp.max(s3, axis=-1, keepdims=True)
    p = jnp.where(valid, jnp.exp(s3 - mx), 0.0)
    den = jnp.sum(p, axis=-1, keepdims=True)
    p_c = p * (1.0 / jnp.where(den > 0.0, den, 1.0))
    o_cmp = jnp.dot(p_c.reshape(M, n_cmp_slots).astype(BF16), vc_ref[...].astype(BF16),
                    preferred_element_type=F32)

    p_sum = p_c[0]
    for h in range(1, HP):
        p_sum = p_sum + p_c[h]
    imp = jnp.dot(p_sum, ov_ref[...], preferred_element_type=F32, precision=lax.Precision.HIGHEST)
    j_idx = lax.broadcasted_iota(jnp.int32, (tq, n_sel), 1)
    cur = (t0 + lax.broadcasted_iota(jnp.int32, (tq, n_sel), 0)) // SEL_BLOCK
    forced = (j_idx == 0) | (j_idx == cur) | (j_idx == cur - 1)
    imp = jnp.where(forced, FORCE_SCORE, jnp.where(j_idx > cur, NEG_INF, imp))
    rank = jnp.zeros((tq, n_sel), F32)
    for i in range(n_sel):
        ci = imp[:, i:i + 1]
        ge = jnp.where(ci >= imp, 1.0, 0.0)
        gt = jnp.where(ci > imp, 1.0, 0.0)
        rank = rank + jnp.where(j_idx > i, ge, gt)
    sel = jnp.where(rank < float(min(SEL_TOP_N, n_sel)), 1.0, 0.0).astype(BF16)

    for kt in range(S // tk):
        hit = jnp.dot(sel, ex_ref[:, kt * tk:(kt + 1) * tk], preferred_element_type=F32)
        kpos = kt * tk + lax.broadcasted_iota(jnp.int32, (tq, tk), 1)
        tpos = t0 + lax.broadcasted_iota(jnp.int32, (tq, tk), 0)
        bias_scr[kt] = jnp.where((hit > 0.5) & (kpos <= tpos), 0.0, NEG_INF)

    m_scr[...] = jnp.full((HP, tq, 1), NEG_INF, F32)
    l_scr[...] = jnp.zeros((HP, tq, 1), F32)
    acc_scr[...] = jnp.zeros((M, DH), F32)

    def slc_step(kt, carry):
        start = pl.multiple_of(kt * tk, tk)
        k = ks_ref[pl.ds(start, tk), :]
        v = vs_ref[pl.ds(start, tk), :]
        sc = lax.dot_general(q, k, nt_dims, preferred_element_type=F32).reshape(HP, tq, tk)
        sc = sc + bias_scr[kt][None]
        m_old = m_scr[...]
        m_new = jnp.maximum(m_old, jnp.max(sc, axis=-1, keepdims=True))
        alpha = jnp.exp(m_old - m_new)
        pr = jnp.exp(sc - m_new)
        l_scr[...] = alpha * l_scr[...] + jnp.sum(pr, axis=-1, keepdims=True)
        m_scr[...] = m_new
        pv = jnp.dot(pr.reshape(M, tk).astype(BF16), v, preferred_element_type=F32)
        acc_scr[...] = alpha.reshape(M, 1) * acc_scr[...] + pv
        return carry

    lax.fori_loop(0, qi + 1, slc_step, 0)
    o_slc = acc_scr[...] * (1.0 / l_scr[...]).reshape(M, 1)

    n_back = WINDOW // tk
    scs, vws = [], []
    for d in range(n_back, -1, -1):
        kt = qi - d
        start = pl.multiple_of(jnp.maximum(kt, 0) * tk, tk)
        k = kw_ref[pl.ds(start, tk), :]
        vws.append(vw_ref[pl.ds(start, tk), :])
        sc = lax.dot_general(q, k, nt_dims, preferred_element_type=F32).reshape(HP, tq, tk)
        wpos = kt * tk + lax.broadcasted_iota(jnp.int32, (tq, tk), 1)
        tpos = t0 + lax.broadcasted_iota(jnp.int32, (tq, tk), 0)
        ok = (wpos >= 0) & (wpos <= tpos) & (wpos > tpos - WINDOW)
        scs.append(jnp.where(ok[None], sc, NEG_INF))
    mw = scs[0].max(axis=-1, keepdims=True)
    for sc in scs[1:]:
        mw = jnp.maximum(mw, sc.max(axis=-1, keepdims=True))
    lw = jnp.zeros((HP, tq, 1), F32)
    o_win = jnp.zeros((M, DH), F32)
    for sc, v in zip(scs, vws):
        pr = jnp.exp(sc - mw)
        lw = lw + jnp.sum(pr, axis=-1, keepdims=True)
        o_win = o_win + jnp.dot(pr.reshape(M, tk).astype(BF16), v, preferred_element_type=F32)
    o_win = o_win * (1.0 / lw).reshape(M, 1)

    g = _sigmoid(gate_ref[...].reshape(M, N_BRANCH))
    o = g[:, 0:1] * o_cmp + g[:, 1:2] * o_slc + g[:, 2:3] * o_win
    o_ref[...] = jnp.concatenate([o[h * tq:(h + 1) * tq, :] for h in range(HP)], axis=-1).astype(o_ref.dtype)


def _nsa(q_l, kc, vc, kv_l, gates, overlap, expand, *, B, S, tq=256):
    G, HP, DH = NSA_KV_GROUPS, NSA_HPG, NSA_HEAD_DIM
    slots = S // CMP_STRIDE
    n_sel = S // SEL_BLOCK

    def kv_spec(which):
        return pl.BlockSpec((None, None, None, S, DH), lambda b, g, i: (which, b, g, 0, 0))

    return pl.pallas_call(
        functools.partial(_nsa_kernel, tq=tq, S=S),
        grid=(B, G, S // tq),
        in_specs=[
            pl.BlockSpec((None, None, HP, tq, DH), lambda b, g, i: (b, g, 0, i, 0)),
            pl.BlockSpec((slots, DH), lambda b, g, i: (b * G + g, 0)),
            pl.BlockSpec((slots, DH), lambda b, g, i: (b * G + g, 0)),
            kv_spec(2), kv_spec(3), kv_spec(4), kv_spec(5),
            pl.BlockSpec((None, None, HP, tq, N_BRANCH), lambda b, g, i: (b, g, 0, i, 0)),
            pl.BlockSpec((slots, n_sel), lambda b, g, i: (0, 0)),
            pl.BlockSpec((n_sel, S), lambda b, g, i: (0, 0)),
        ],
        out_specs=pl.BlockSpec((None, tq, HP * DH), lambda b, g, i: (b, i, g)),
        out_shape=jax.ShapeDtypeStruct((B, S, NSA_Q_WIDTH), BF16),
        scratch_shapes=[
            pltpu.VMEM((S // tq, tq, tq), F32),
            pltpu.VMEM((HP, tq, 1), F32),
            pltpu.VMEM((HP, tq, 1), F32),
            pltpu.VMEM((HP * tq, DH), F32),
        ],
        compiler_params=_cparams(("parallel", "parallel", "arbitrary")),
        name="nsa_attention",
    )(q_l, kc, vc, kv_l, kv_l, kv_l, kv_l, gates, overlap, expand)


def _merge_kernel(ya_ref, yb_ref, yc_ref, mga_ref, mgb_ref, mgc_ref, x_ref, wa_ref, wb_ref, wc_ref, wo_ref,
                  gpost_ref, gpre_ref, x1_ref, hf_ref):
    merged = _sigmoid(mga_ref[...]) * jnp.dot(ya_ref[...], wa_ref[...], preferred_element_type=F32)
    merged = merged + _sigmoid(mgb_ref[...]) * jnp.dot(yb_ref[...], wb_ref[...], preferred_element_type=F32)
    merged = merged + _sigmoid(mgc_ref[...]) * jnp.dot(yc_ref[...], wc_ref[...], preferred_element_type=F32)
    y = jnp.dot(merged.astype(BF16), wo_ref[...], preferred_element_type=F32)
    x1 = x_ref[...] + _rms(y, gpost_ref[...])
    x1_ref[...] = x1
    hf_ref[...] = _rms(x1, gpre_ref[...]).astype(BF16)


def _merge(ya, yb, yc, z, x2d, wa, wb, wc, wo, gpost, gpre, *, tm=512):
    T, D = x2d.shape
    row = lambda c: pl.BlockSpec((tm, D), lambda i: (i, c))
    wfull = pl.BlockSpec((D, D), lambda i: (0, 0))
    vec = pl.BlockSpec((1, D), lambda i: (0, 0))
    mg0 = OFF_MG // D
    return pl.pallas_call(
        _merge_kernel,
        grid=(T // tm,),
        in_specs=[row(0), row(0), row(0), row(mg0), row(mg0 + 1), row(mg0 + 2), row(0),
                  wfull, wfull, wfull, wfull, vec, vec],
        out_specs=[row(0), row(0)],
        out_shape=[jax.ShapeDtypeStruct((T, D), F32), jax.ShapeDtypeStruct((T, D), BF16)],
        compiler_params=_cparams(("parallel",)),
        name="merge",
    )(ya, yb, yc, z, z, z, x2d, wa, wb, wc, wo, gpost, gpre)


def _ffn_kernel(hf_ref, x1_ref, win_ref, wout_ref, gpost_ref, o_ref, acc_scr, *, fc):
    hf = hf_ref[...]
    n_chunks = D_FF // fc
    for c in range(n_chunks):
        gate = jnp.dot(hf, win_ref[:, c * fc:(c + 1) * fc], preferred_element_type=F32)
        up = jnp.dot(hf, win_ref[:, D_FF + c * fc:D_FF + (c + 1) * fc], preferred_element_type=F32)
        act = (gate * _sigmoid(gate) * up).astype(BF16)
        part = jnp.dot(act, wout_ref[c * fc:(c + 1) * fc, :], preferred_element_type=F32)
        if c == 0:
            acc_scr[...] = part
        else:
            acc_scr[...] += part
    o_ref[...] = x1_ref[...] + _rms(acc_scr[...], gpost_ref[...])


def _ffn(hf, x1, win, wout, gpost, *, tm=512, fc=1408):
    T, D = x1.shape
    row = pl.BlockSpec((tm, D), lambda i: (i, 0))
    return pl.pallas_call(
        functools.partial(_ffn_kernel, fc=fc),
        grid=(T // tm,),
        in_specs=[row, row,
                  pl.BlockSpec(win.shape, lambda i: (0, 0), pipeline_mode=pl.Buffered(1)),
                  pl.BlockSpec(wout.shape, lambda i: (0, 0), pipeline_mode=pl.Buffered(1)),
                  pl.BlockSpec((1, D), lambda i: (0, 0))],
        out_specs=row,
        out_shape=jax.ShapeDtypeStruct((T, D), F32),
        scratch_shapes=[pltpu.VMEM((tm, D), F32)],
        compiler_params=_cparams(("parallel",)),
        name="ffn",
    )(hf, x1, win, wout, gpost)


def _selection_constants(S):
    slots = S // CMP_STRIDE
    n_sel = S // SEL_BLOCK
    c_start = np.arange(slots) * CMP_STRIDE
    s_start = np.arange(n_sel) * SEL_BLOCK
    overlap = np.clip(np.minimum(c_start[:, None] + CMP_BLOCK, s_start[None, :] + SEL_BLOCK)
                      - np.maximum(c_start[:, None], s_start[None, :]), 0, None) / CMP_BLOCK
    expand = (np.arange(S)[None, :] // SEL_BLOCK == np.arange(n_sel)[:, None])
    return jnp.asarray(overlap, F32), jnp.asarray(expand, BF16)


def _reorder_w_in(w_in):
    s = np.cumsum([0, GMLP_WIDTH, GMLP_WIDTH, NSA_Q_WIDTH, 6 * NSA_KV_WIDTH, N_BRANCH * NSA_HEADS,
                   RNN_WIDTH, RNN_WIDTH, N_BRANCH * D_MODEL])
    u, v, q, kv, ng, xr, rg, mg = (w_in[:, s[i]:s[i + 1]] for i in range(8))
    pad = jnp.zeros((w_in.shape[0], D_IN_PAD - int(s[-1])), w_in.dtype)
    return jnp.concatenate([u, v, q, xr, rg, mg, kv, ng, pad], axis=1).astype(BF16)


def _block_diag_gates(wa, wx):
    per = RNN_BLOCK // RNN_HEAD_DIM
    nblk = RNN_HEADS // per
    eye = jnp.eye(per, dtype=wa.dtype)

    def bd(w):
        w = w.reshape(nblk, per, RNN_HEAD_DIM, RNN_HEAD_DIM)
        return jnp.einsum('kpio,pq->kpiqo', w, eye).reshape(nblk, RNN_BLOCK, RNN_BLOCK)

    return jnp.concatenate([bd(wa), bd(wx)], axis=-1).astype(BF16)


def _layer(x2d, B, S, overlap, expand, g_pre_mix, g_post_mix, g_pre_ffn, g_post_ffn, w_in,
           gmlp_ln_g, gmlp_ln_b, gmlp_ws, gmlp_bs,
           nsa_pe_k, nsa_pe_v, nsa_wk1, nsa_wk2, nsa_wv1, nsa_wv2,
           rnn_conv_w, rnn_conv_b, rnn_wa, rnn_ba, rnn_wx, rnn_bx, rnn_lam,
           w_br_a, w_br_b, w_br_c, w_o, w_ffn_in, w_ffn_out):
    G, HP, DH = NSA_KV_GROUPS, NSA_HPG, NSA_HEAD_DIM
    row = lambda a: a.reshape(1, -1)

    z = _in_proj(x2d, row(g_pre_mix), _reorder_w_in(w_in))

    y_a = _gmlp(z, row(gmlp_ln_g), row(gmlp_ln_b), gmlp_ws, gmlp_bs.T)
    y_c = _rglru(z, rnn_conv_w, row(rnn_conv_b), _block_diag_gates(rnn_wa, rnn_wx),
                 row(rnn_ba), row(rnn_bx), row(rnn_lam), B=B, S=S)

    q_l = z[:, OFF_Q:OFF_Q + NSA_Q_WIDTH].reshape(B, S, G, HP, DH).transpose(0, 2, 3, 1, 4).astype(BF16)
    kv_l = z[:, OFF_KV:OFF_NG].reshape(B, S, 6, G, DH).transpose(2, 0, 3, 1, 4).astype(BF16)
    gates = z[:, OFF_NG:OFF_NG + N_BRANCH * NSA_HEADS].reshape(B, S, G, HP, N_BRANCH).transpose(0, 2, 3, 1, 4)
    halves = lambda t: t.reshape(B * G * (S // CMP_STRIDE), CMP_STRIDE * DH)
    kc, vc = _compress(halves(kv_l[0]), halves(kv_l[1]), nsa_pe_k.reshape(1, -1), nsa_pe_v.reshape(1, -1),
                       nsa_wk1.astype(BF16), nsa_wk2.astype(BF16), nsa_wv1.astype(BF16), nsa_wv2.astype(BF16))
    y_b = _nsa(q_l, kc, vc, kv_l, gates, overlap, expand, B=B, S=S).reshape(B * S, NSA_Q_WIDTH)

    x1, hf = _merge(y_a, y_b, y_c, z, x2d, w_br_a.astype(BF16), w_br_b.astype(BF16), w_br_c.astype(BF16),
                    w_o.astype(BF16), row(g_post_mix), row(g_pre_ffn))
    return _ffn(hf, x1, w_ffn_in.astype(BF16), w_ffn_out.astype(BF16), row(g_post_ffn))


def kernel(x, g_pre_mix, g_post_mix, g_pre_ffn, g_post_ffn, w_in, gmlp_ln_g, gmlp_ln_b, gmlp_ws, gmlp_bs, nsa_pe_k, nsa_pe_v, nsa_wk1, nsa_wk2, nsa_wv1, nsa_wv2, rnn_conv_w, rnn_conv_b, rnn_wa, rnn_ba, rnn_wx, rnn_bx, rnn_lam, w_br_a, w_br_b, w_br_c, w_o, w_ffn_in, w_ffn_out):
    B, S, D = x.shape
    params = (g_pre_mix, g_post_mix, g_pre_ffn, g_post_ffn, w_in, gmlp_ln_g, gmlp_ln_b, gmlp_ws, gmlp_bs,
              nsa_pe_k, nsa_pe_v, nsa_wk1, nsa_wk2, nsa_wv1, nsa_wv2,
              rnn_conv_w, rnn_conv_b, rnn_wa, rnn_ba, rnn_wx, rnn_bx, rnn_lam,
              w_br_a, w_br_b, w_br_c, w_o, w_ffn_in, w_ffn_out)
    overlap, expand = _selection_constants(S)
    x2d = x.reshape(B * S, D)
    for l in range(w_in.shape[0]):
        x2d = _layer(x2d, B, S, overlap, expand, *(p[l] for p in params))
    return x2d.reshape(B, S, D)
```

```python
import functools

import jax
import jax.numpy as jnp
import numpy as np
from jax import lax
from jax.experimental import pallas as pl
from jax.experimental.pallas import tpu as pltpu

F32 = jnp.float32
BF16 = jnp.bfloat16

EPS = 1e-6
NEG_INF = -1e30
FORCE_SCORE = 1e4

D_MODEL = 1024
GMLP_WIDTH = 1024
GMLP_GROUPS = 4
GMLP_GROUP_DIM = GMLP_WIDTH // GMLP_GROUPS
GMLP_CHUNK = 128

NSA_HEADS = 16
NSA_KV_GROUPS = 4
NSA_HEAD_DIM = 64
NSA_HPG = NSA_HEADS // NSA_KV_GROUPS
NSA_Q_WIDTH = NSA_HEADS * NSA_HEAD_DIM
NSA_KV_WIDTH = NSA_KV_GROUPS * NSA_HEAD_DIM
N_BRANCH = 3
CMP_BLOCK = 32
CMP_STRIDE = 16
CMP_HIDDEN = 256
SEL_BLOCK = 64
SEL_TOP_N = 16
WINDOW = 512

RNN_WIDTH = 1024
RNN_HEADS = 16
RNN_HEAD_DIM = RNN_WIDTH // RNN_HEADS
CONV_WIDTH = 4
LRU_C = 8.0
RNN_BLOCK = 256

D_FF = 2816

OFF_U = 0
OFF_V = 1024
OFF_XR = 2048
OFF_RG = 3072
OFF_MG = 4096
OFF_KC = 7168
OFF_NG = OFF_KC + 2 * NSA_KV_WIDTH
D_IN_PAD = 8192
ROW_Q = 0
ROW_KV = NSA_Q_WIDTH
D_IN_T = NSA_Q_WIDTH + 4 * NSA_KV_WIDTH

LANES = 128
BF16_SUBLANES = 16
V7X_VMEM_LIMIT = 56 * 1024 * 1024


def _cparams(sem, vmem=V7X_VMEM_LIMIT):
    return pltpu.CompilerParams(dimension_semantics=sem, vmem_limit_bytes=vmem)


def _rms(x, g):
    ms = jnp.mean(x * x, axis=-1, keepdims=True)
    return x * lax.rsqrt(ms + EPS) * g


def _gelu(x):
    return jax.nn.gelu(x)


def _sigmoid(x):
    return jax.nn.sigmoid(x)


_NT = (((1,), (1,)), ((), ()))


def _in_proj_kernel(x_ref, g_ref, w_ref, o_ref, h_scr):
    @pl.when(pl.program_id(1) == 0)
    def _():
        h_scr[...] = _rms(x_ref[...], g_ref[...]).astype(BF16)

    o_ref[...] = jnp.dot(h_scr[...], w_ref[...], preferred_element_type=F32)


def _in_proj(x2d, g, w, *, tm=1024, tn=1024):
    T, D = x2d.shape
    N = w.shape[1]
    return pl.pallas_call(
        _in_proj_kernel,
        grid=(T // tm, N // tn),
        in_specs=[
            pl.BlockSpec((tm, D), lambda i, j: (i, 0)),
            pl.BlockSpec((1, D), lambda i, j: (0, 0)),
            pl.BlockSpec((D, tn), lambda i, j: (0, j)),
        ],
        out_specs=pl.BlockSpec((tm, tn), lambda i, j: (i, j)),
        out_shape=jax.ShapeDtypeStruct((T, N), F32),
        scratch_shapes=[pltpu.VMEM((tm, D), BF16)],
        compiler_params=_cparams(("parallel", "arbitrary")),
        name="in_proj",
    )(x2d, g, w)


def _in_proj_t_kernel(x_ref, g_ref, wt_ref, o_ref, h_scr):
    @pl.when(pl.program_id(1) == 0)
    def _():
        h_scr[...] = _rms(x_ref[...], g_ref[...]).astype(BF16)

    o_ref[...] = lax.dot_general(wt_ref[...], h_scr[...], _NT, preferred_element_type=F32).astype(o_ref.dtype)


def _in_proj_t(x2d, g, wt, *, tm=1024, tr=512):
    T, D = x2d.shape
    R = wt.shape[0]
    return pl.pallas_call(
        _in_proj_t_kernel,
        grid=(T // tm, R // tr),
        in_specs=[
            pl.BlockSpec((tm, D), lambda i, j: (i, 0)),
            pl.BlockSpec((1, D), lambda i, j: (0, 0)),
            pl.BlockSpec((tr, D), lambda i, j: (j, 0)),
        ],
        out_specs=pl.BlockSpec((tr, tm), lambda i, j: (j, i)),
        out_shape=jax.ShapeDtypeStruct((R, T), BF16),
        scratch_shapes=[pltpu.VMEM((tm, D), BF16)],
        compiler_params=_cparams(("parallel", "arbitrary")),
        name="in_proj_t",
    )(x2d, g, wt)


def _gmlp_kernel(u_ref, v_ref, lng_ref, lnb_ref, ws_ref, bst_ref, o_ref, *, n_chunks):
    C = GMLP_CHUNK
    row = lax.broadcasted_iota(jnp.int32, (C, C), 0)
    col = lax.broadcasted_iota(jnp.int32, (C, C), 1)
    causal = col <= row
    ws = [jnp.where(causal, ws_ref[g], 0.0).astype(BF16) for g in range(GMLP_GROUPS)]
    bst = bst_ref[...]
    for c in range(n_chunks):
        rows = slice(c * C, (c + 1) * C)
        gv = _gelu(v_ref[rows, :])
        mu = jnp.mean(gv, axis=-1, keepdims=True)
        d = gv - mu
        var = jnp.mean(d * d, axis=-1, keepdims=True)
        vn = (d * lax.rsqrt(var + EPS) * lng_ref[...] + lnb_ref[...]).astype(BF16)
        for g in range(GMLP_GROUPS):
            cols = slice(g * GMLP_GROUP_DIM, (g + 1) * GMLP_GROUP_DIM)
            mixed = jnp.dot(ws[g], vn[:, cols], preferred_element_type=F32) + bst[:, g:g + 1]
            o_ref[rows, cols] = (_gelu(u_ref[rows, cols]) * mixed).astype(o_ref.dtype)


def _gmlp(z, lng, lnb, ws, bst, *, n_chunks=4):
    T = z.shape[0]
    tm = GMLP_CHUNK * n_chunks
    W = GMLP_WIDTH
    return pl.pallas_call(
        functools.partial(_gmlp_kernel, n_chunks=n_chunks),
        grid=(T // tm,),
        in_specs=[
            pl.BlockSpec((tm, W), lambda i: (i, OFF_U // W)),
            pl.BlockSpec((tm, W), lambda i: (i, OFF_V // W)),
            pl.BlockSpec((1, W), lambda i: (0, 0)),
            pl.BlockSpec((1, W), lambda i: (0, 0)),
            pl.BlockSpec((GMLP_GROUPS, GMLP_CHUNK, GMLP_CHUNK), lambda i: (0, 0, 0)),
            pl.BlockSpec((GMLP_CHUNK, GMLP_GROUPS), lambda i: (0, 0)),
        ],
        out_specs=pl.BlockSpec((tm, W), lambda i: (i, 0)),
        out_shape=jax.ShapeDtypeStruct((T, W), BF16),
        compiler_params=_cparams(("parallel",)),
        name="gmlp",
    )(z, z, lng, lnb, ws, bst)


def _rglru_kernel(xr_ref, rg_ref, cw_ref, cb_ref, wab_ref, ba_ref, bx_ref, lam_ref, o_ref,
                  ext_scr, a_scr, b_scr, hc_scr, *, tt):
    W = RNN_WIDTH
    seg = tt // 8

    @pl.when(pl.program_id(1) == 0)
    def _():
        ext_scr[0:8, :] = jnp.zeros((8, W), F32)
        hc_scr[...] = jnp.zeros((8, W), F32)

    xr = xr_ref[...]
    ext_scr[8:8 + tt, :] = xr
    cw = cw_ref[...]
    xc = (cw[3:4] * xr + cw[2:3] * ext_scr[7:7 + tt, :] + cw[1:2] * ext_scr[6:6 + tt, :]
          + cw[0:1] * ext_scr[5:5 + tt, :] + cb_ref[...])
    ext_scr[0:8, :] = xr[tt - 8:tt, :]

    lam = lam_ref[...]
    neg = -lam
    softplus = jnp.maximum(neg, 0.0) + jnp.log1p(jnp.exp(-jnp.abs(neg)))
    for k in range(W // RNN_BLOCK):
        cols = slice(k * RNN_BLOCK, (k + 1) * RNN_BLOCK)
        xck = xc[:, cols]
        gates = jnp.dot(xck.astype(BF16), wab_ref[k], preferred_element_type=F32)
        r = _sigmoid(gates[:, :RNN_BLOCK] + ba_ref[:, cols])
        i = _sigmoid(gates[:, RNN_BLOCK:] + bx_ref[:, cols])
        log_a = -LRU_C * r * softplus[:, cols]
        a = jnp.exp(log_a)
        one_minus_a2 = -jnp.tanh(log_a) * (1.0 + a * a)
        b_in = jnp.sqrt(one_minus_a2) * (i * xck)
        for c in range(RNN_BLOCK // LANES):
            a_scr[k * (RNN_BLOCK // LANES) + c] = a[:, c * LANES:(c + 1) * LANES]
            b_scr[k * (RNN_BLOCK // LANES) + c] = b_in[:, c * LANES:(c + 1) * LANES]

    n_lane_blocks = W // LANES

    def scan_step(j, carry):
        hs, ps = carry
        new_h, new_p = [], []
        for c in range(n_lane_blocks):
            a = a_scr[c, pl.ds(j, 8, stride=seg), :]
            b = b_scr[c, pl.ds(j, 8, stride=seg), :]
            h = a * hs[c] + b
            p = a * ps[c]
            b_scr[c, pl.ds(j, 8, stride=seg), :] = h
            a_scr[c, pl.ds(j, 8, stride=seg), :] = p
            new_h.append(h)
            new_p.append(p)
        return tuple(new_h), tuple(new_p)

    init = (tuple(jnp.zeros((8, LANES), F32) for _ in range(n_lane_blocks)),
            tuple(jnp.ones((8, LANES), F32) for _ in range(n_lane_blocks)))
    h_loc, p_all = lax.fori_loop(0, seg, scan_step, init)

    for c in range(n_lane_blocks):
        cols = slice(c * LANES, (c + 1) * LANES)
        h_in = hc_scr[0:1, cols]
        for s in range(8):
            rows = slice(s * seg, (s + 1) * seg)
            h = b_scr[c, rows, :] + a_scr[c, rows, :] * h_in
            o_ref[rows, cols] = (h * _gelu(rg_ref[rows, cols])).astype(o_ref.dtype)
            h_in = h_loc[c][s:s + 1, :] + p_all[c][s:s + 1, :] * h_in
        hc_scr[:, cols] = jnp.broadcast_to(h_in, (8, LANES))


def _rglru(z, cw, cb, wab, ba, bx, lam, *, B, S, tt=512):
    T = z.shape[0]
    W = RNN_WIDTH
    nt = S // tt
    return pl.pallas_call(
        functools.partial(_rglru_kernel, tt=tt),
        grid=(B, nt),
        in_specs=[
            pl.BlockSpec((tt, W), lambda b, t: (b * nt + t, OFF_XR // W)),
            pl.BlockSpec((tt, W), lambda b, t: (b * nt + t, OFF_RG // W)),
            pl.BlockSpec((CONV_WIDTH, W), lambda b, t: (0, 0)),
            pl.BlockSpec((1, W), lambda b, t: (0, 0)),
            pl.BlockSpec((W // RNN_BLOCK, RNN_BLOCK, 2 * RNN_BLOCK), lambda b, t: (0, 0, 0)),
            pl.BlockSpec((1, W), lambda b, t: (0, 0)),
            pl.BlockSpec((1, W), lambda b, t: (0, 0)),
            pl.BlockSpec((1, W), lambda b, t: (0, 0)),
        ],
        out_specs=pl.BlockSpec((tt, W), lambda b, t: (b * nt + t, 0)),
        out_shape=jax.ShapeDtypeStruct((T, W), BF16),
        scratch_shapes=[
            pltpu.VMEM((tt + 8, W), F32),
            pltpu.VMEM((W // LANES, tt, LANES), F32),
            pltpu.VMEM((W // LANES, tt, LANES), F32),
            pltpu.VMEM((8, W), F32),
        ],
        compiler_params=_cparams(("parallel", "arbitrary")),
        name="rglru",
    )(z, z, cw, cb, wab, ba, bx, lam)


def _compress_hidden(h_ref, pe_ref, w1_ref):
    half = CMP_STRIDE * NSA_HEAD_DIM
    hm = h_ref[...]
    rows = hm.shape[0]
    lo = jnp.dot(hm, w1_ref[0:half, :], preferred_element_type=F32)
    hi = jnp.dot(hm, w1_ref[half:2 * half, :], preferred_element_type=F32)
    pe = jnp.broadcast_to(pe_ref[...], (8, 2 * half)).astype(BF16)
    pe_term = jnp.dot(pe, w1_ref[...], preferred_element_type=F32)[0:1, :]
    return _gelu(lo + pltpu.roll(hi, rows - 1, axis=0) + pe_term).astype(BF16)


def _compress_kernel(hk_ref, hv_ref, pek_ref, pev_ref, wk1_ref, wk2_ref, wv1_ref, wv2t_ref, kc_ref, vct_ref):
    kc_ref[...] = jnp.dot(_compress_hidden(hk_ref, pek_ref, wk1_ref), wk2_ref[...], preferred_element_type=F32)
    vct_ref[...] = lax.dot_general(wv2t_ref[...], _compress_hidden(hv_ref, pev_ref, wv1_ref), _NT,
                                   preferred_element_type=F32)


def _compress(hk, hv, pek, pev, wk1, wk2, wv1, wv2t, *, tr=512):
    R, K = hk.shape
    DH = NSA_HEAD_DIM
    full = lambda shape: pl.BlockSpec(shape, lambda i: (0,) * len(shape))
    return pl.pallas_call(
        _compress_kernel,
        grid=(R // tr,),
        in_specs=[
            pl.BlockSpec((tr, K), lambda i: (i, 0)),
            pl.BlockSpec((tr, K), lambda i: (i, 0)),
            full(pek.shape), full(pev.shape), full(wk1.shape), full(wk2.shape), full(wv1.shape), full(wv2t.shape),
        ],
        out_specs=[pl.BlockSpec((tr, DH), lambda i: (i, 0)), pl.BlockSpec((DH, tr), lambda i: (0, i))],
        out_shape=[jax.ShapeDtypeStruct((R, DH), F32), jax.ShapeDtypeStruct((DH, R), F32)],
        compiler_params=_cparams(("parallel",)),
        name="nsa_compress",
    )(hk, hv, pek, pev, wk1, wk2, wv1, wv2t)


def _nsa_kernel(qt_ref, kc_ref, vct_ref, kst_ref, vst_ref, kwt_ref, vwt_ref, gt_ref, ovt_ref, o_ref,
                ks_scr, kw_scr, vs_scr, vw_scr, bias_scr, m_scr, acc_scr, *, tq, S):
    HP, DH = NSA_HPG, NSA_HEAD_DIM
    M = HP * tq
    tk = tq
    n_kt = S // tk
    n_sel = S // SEL_BLOCK
    slots = kc_ref.shape[0]
    v_rows = DH + BF16_SUBLANES
    g_idx = pl.program_id(1)
    qi = pl.program_id(2)
    t0 = qi * tq

    @pl.when(qi == 0)
    def _():
        ks_scr[...] = kst_ref[...].T
        kw_scr[...] = kwt_ref[...].T
        ones = jnp.ones((BF16_SUBLANES, tk), BF16)
        for kt in range(n_kt):
            vs_scr[kt, 0:DH, :] = vst_ref[:, kt * tk:(kt + 1) * tk]
            vs_scr[kt, DH:v_rows, :] = ones
            vw_scr[kt, 0:DH, :] = vwt_ref[:, kt * tk:(kt + 1) * tk]
            vw_scr[kt, DH:v_rows, :] = ones

    def per_head(x):
        return jnp.concatenate([x] * HP, axis=1)

    qt = jnp.concatenate([qt_ref[h * DH:(h + 1) * DH, :] for h in range(HP)], axis=1)
    qt = qt * jnp.asarray(DH ** -0.5, BF16)

    sc = jnp.dot(kc_ref[...].astype(BF16), qt, preferred_element_type=F32)
    n_idx = lax.broadcasted_iota(jnp.int32, (slots, tq), 0)
    t_idx = t0 + lax.broadcasted_iota(jnp.int32, (slots, tq), 1)
    valid = per_head(jnp.where(n_idx * CMP_STRIDE + (CMP_BLOCK - 1) <= t_idx, 1.0, 0.0)) > 0.5
    sc = jnp.where(valid, sc, NEG_INF)
    mx = jnp.max(sc, axis=0, keepdims=True)
    p = jnp.where(valid, jnp.exp(sc - mx), 0.0)
    den = jnp.sum(p, axis=0, keepdims=True)
    p_c = p * (1.0 / jnp.where(den > 0.0, den, 1.0))
    o_cmp = jnp.dot(vct_ref[...].astype(BF16), p_c.astype(BF16), preferred_element_type=F32)

    p_sum = p_c[:, 0:tq]
    for h in range(1, HP):
        p_sum = p_sum + p_c[:, h * tq:(h + 1) * tq]
    imp = jnp.dot(ovt_ref[...], p_sum, preferred_element_type=F32, precision=lax.Precision.HIGHEST)
    j_idx = lax.broadcasted_iota(jnp.int32, (n_sel, tq), 0)
    cur = (t0 + lax.broadcasted_iota(jnp.int32, (n_sel, tq), 1)) // SEL_BLOCK
    forced = (j_idx == 0) | (j_idx == cur) | (j_idx == cur - 1)
    imp = jnp.where(forced, FORCE_SCORE, jnp.where(j_idx > cur, NEG_INF, imp))
    rank = jnp.zeros((n_sel, tq), F32)
    for i in range(n_sel):
        ci = imp[i:i + 1, :]
        ge = jnp.where(ci >= imp, 1.0, 0.0)
        gt = jnp.where(ci > imp, 1.0, 0.0)
        rank = rank + jnp.where(j_idx > i, ge, gt)
    sel = jnp.where(rank < float(min(SEL_TOP_N, n_sel)), 1.0, 0.0)

    per_tile = tk // SEL_BLOCK
    kloc = lax.broadcasted_iota(jnp.int32, (tk, tq), 0)
    tloc = lax.broadcasted_iota(jnp.int32, (tk, tq), 1)
    for kt in range(n_kt):
        @pl.when(kt <= qi)
        def _():
            hit = jnp.concatenate(
                [jnp.broadcast_to(sel[kt * per_tile + jj:kt * per_tile + jj + 1, :], (SEL_BLOCK, tq))
                 for jj in range(per_tile)], axis=0)
            bias_scr[kt] = jnp.where((hit > 0.5) & (kt * tk + kloc <= t0 + tloc), 0.0, NEG_INF)

    def attend(k, v_ext, bias):
        s = jnp.dot(k, qt, preferred_element_type=F32)
        if bias is not None:
            s = s + per_head(bias)
        m_old = m_scr[...]
        m_new = jnp.maximum(m_old, jnp.max(s, axis=0, keepdims=True))
        alpha = jnp.exp(m_old - m_new)
        pr = jnp.exp(s - m_new).astype(BF16)
        m_scr[...] = m_new
        acc_scr[...] = alpha * acc_scr[...] + jnp.dot(v_ext, pr, preferred_element_type=F32)

    def reset():
        m_scr[...] = jnp.full((1, M), NEG_INF, F32)
        acc_scr[...] = jnp.zeros((v_rows, M), F32)

    def result():
        acc = acc_scr[...]
        return acc[0:DH, :] * (1.0 / acc[DH:DH + 1, :])

    reset()

    def slc_step(kt, carry):
        start = pl.multiple_of(kt * tk, tk)
        attend(ks_scr[pl.ds(start, tk), :], vs_scr[kt], bias_scr[kt])
        return carry

    lax.fori_loop(0, qi + 1, slc_step, 0)
    o_slc = result()

    reset()
    n_back = WINDOW // tk
    start = pl.multiple_of(qi * tk, tk)
    attend(kw_scr[pl.ds(start, tk), :], vw_scr[qi], jnp.where(kloc <= tloc, 0.0, NEG_INF))
    for d in range(1, n_back + 1):
        @pl.when(qi >= d)
        def _():
            start = pl.multiple_of((qi - d) * tk, tk)
            bias = jnp.where(kloc > tloc, 0.0, NEG_INF) if d == n_back else None
            attend(kw_scr[pl.ds(start, tk), :], vw_scr[qi - d], bias)
    o_win = result()

    def gate(branch):
        rows = [gt_ref[pl.ds(g_idx * (HP * N_BRANCH) + h * N_BRANCH + branch, 1), :] for h in range(HP)]
        return _sigmoid(jnp.concatenate(rows, axis=1))

    o = gate(0) * o_cmp + gate(1) * o_slc + gate(2) * o_win
    o_ref[...] = jnp.concatenate([o[:, h * tq:(h + 1) * tq].T for h in range(HP)], axis=1).astype(o_ref.dtype)


def _nsa(zt, kc, vct, gt, ovt, *, B, S, tq=256):
    G, HP, DH = NSA_KV_GROUPS, NSA_HPG, NSA_HEAD_DIM
    T = B * S
    nq = S // tq
    slots = S // CMP_STRIDE
    n_sel = S // SEL_BLOCK
    n_kt = S // tq
    v_rows = DH + BF16_SUBLANES

    def kv_spec(which):
        base = (ROW_KV + which * NSA_KV_WIDTH) // DH
        return pl.BlockSpec((DH, S), lambda b, g, i: (base + g, b))

    return pl.pallas_call(
        functools.partial(_nsa_kernel, tq=tq, S=S),
        grid=(B, G, nq),
        in_specs=[
            pl.BlockSpec((HP * DH, tq), lambda b, g, i: (g, b * nq + i)),
            pl.BlockSpec((slots, DH), lambda b, g, i: (b * G + g, 0)),
            pl.BlockSpec((DH, slots), lambda b, g, i: (0, b * G + g)),
            kv_spec(0), kv_spec(1), kv_spec(2), kv_spec(3),
            pl.BlockSpec((G * HP * N_BRANCH, tq), lambda b, g, i: (0, b * nq + i)),
            pl.BlockSpec((n_sel, slots), lambda b, g, i: (0, 0)),
        ],
        out_specs=pl.BlockSpec((tq, HP * DH), lambda b, g, i: (b * nq + i, g)),
        out_shape=jax.ShapeDtypeStruct((T, NSA_Q_WIDTH), BF16),
        scratch_shapes=[
            pltpu.VMEM((S, DH), BF16),
            pltpu.VMEM((S, DH), BF16),
            pltpu.VMEM((n_kt, v_rows, tq), BF16),
            pltpu.VMEM((n_kt, v_rows, tq), BF16),
            pltpu.VMEM((n_kt, tq, tq), F32),
            pltpu.VMEM((1, HP * tq), F32),
            pltpu.VMEM((v_rows, HP * tq), F32),
        ],
        compiler_params=_cparams(("parallel", "parallel", "arbitrary")),
        name="nsa_attention",
    )(zt, kc, vct, zt, zt, zt, zt, gt, ovt)


def _merge_kernel(ya_ref, yb_ref, yc_ref, mga_ref, mgb_ref, mgc_ref, x_ref, wa_ref, wb_ref, wc_ref, wo_ref,
                  gpost_ref, gpre_ref, x1_ref, hf_ref):
    merged = _sigmoid(mga_ref[...]) * jnp.dot(ya_ref[...], wa_ref[...], preferred_element_type=F32)
    merged = merged + _sigmoid(mgb_ref[...]) * jnp.dot(yb_ref[...], wb_ref[...], preferred_element_type=F32)
    merged = merged + _sigmoid(mgc_ref[...]) * jnp.dot(yc_ref[...], wc_ref[...], preferred_element_type=F32)
    y = jnp.dot(merged.astype(BF16), wo_ref[...], preferred_element_type=F32)
    x1 = x_ref[...] + _rms(y, gpost_ref[...])
    x1_ref[...] = x1
    hf_ref[...] = _rms(x1, gpre_ref[...]).astype(BF16)


def _merge(ya, yb, yc, z, x2d, wa, wb, wc, wo, gpost, gpre, *, tm=512):
    T, D = x2d.shape
    row = lambda c: pl.BlockSpec((tm, D), lambda i: (i, c))
    wfull = pl.BlockSpec((D, D), lambda i: (0, 0))
    vec = pl.BlockSpec((1, D), lambda i: (0, 0))
    mg0 = OFF_MG // D
    return pl.pallas_call(
        _merge_kernel,
        grid=(T // tm,),
        in_specs=[row(0), row(0), row(0), row(mg0), row(mg0 + 1), row(mg0 + 2), row(0),
                  wfull, wfull, wfull, wfull, vec, vec],
        out_specs=[row(0), row(0)],
        out_shape=[jax.ShapeDtypeStruct((T, D), F32), jax.ShapeDtypeStruct((T, D), BF16)],
        compiler_params=_cparams(("parallel",)),
        name="merge",
    )(ya, yb, yc, z, z, z, x2d, wa, wb, wc, wo, gpost, gpre)


def _ffn_kernel(hf_ref, x1_ref, win_ref, wout_ref, gpost_ref, o_ref, acc_scr, *, fc):
    hf = hf_ref[...]
    n_chunks = D_FF // fc
    for c in range(n_chunks):
        gate = jnp.dot(hf, win_ref[:, c * fc:(c + 1) * fc], preferred_element_type=F32)
        up = jnp.dot(hf, win_ref[:, D_FF + c * fc:D_FF + (c + 1) * fc], preferred_element_type=F32)
        act = (gate * _sigmoid(gate) * up).astype(BF16)
        part = jnp.dot(act, wout_ref[c * fc:(c + 1) * fc, :], preferred_element_type=F32)
        if c == 0:
            acc_scr[...] = part
        else:
            acc_scr[...] += part
    o_ref[...] = x1_ref[...] + _rms(acc_scr[...], gpost_ref[...])


def _ffn(hf, x1, win, wout, gpost, *, tm=512, fc=1408):
    T, D = x1.shape
    row = pl.BlockSpec((tm, D), lambda i: (i, 0))
    return pl.pallas_call(
        functools.partial(_ffn_kernel, fc=fc),
        grid=(T // tm,),
        in_specs=[row, row,
                  pl.BlockSpec(win.shape, lambda i: (0, 0), pipeline_mode=pl.Buffered(1)),
                  pl.BlockSpec(wout.shape, lambda i: (0, 0), pipeline_mode=pl.Buffered(1)),
                  pl.BlockSpec((1, D), lambda i: (0, 0))],
        out_specs=row,
        out_shape=jax.ShapeDtypeStruct((T, D), F32),
        scratch_shapes=[pltpu.VMEM((tm, D), F32)],
        compiler_params=_cparams(("parallel",)),
        name="ffn",
    )(hf, x1, win, wout, gpost)


def _overlap_t(S):
    slots = S // CMP_STRIDE
    n_sel = S // SEL_BLOCK
    c_start = np.arange(slots) * CMP_STRIDE
    s_start = np.arange(n_sel) * SEL_BLOCK
    overlap = np.clip(np.minimum(c_start[:, None] + CMP_BLOCK, s_start[None, :] + SEL_BLOCK)
                      - np.maximum(c_start[:, None], s_start[None, :]), 0, None) / CMP_BLOCK
    return jnp.asarray(overlap.T, F32)


def _split_w_in(w_in):
    s = np.cumsum([0, GMLP_WIDTH, GMLP_WIDTH, NSA_Q_WIDTH, 6 * NSA_KV_WIDTH, N_BRANCH * NSA_HEADS,
                   RNN_WIDTH, RNN_WIDTH, N_BRANCH * D_MODEL])
    u, v, q, kv, ng, xr, rg, mg = (w_in[:, s[i]:s[i + 1]] for i in range(8))
    kv_cmp, kv_rest = kv[:, :2 * NSA_KV_WIDTH], kv[:, 2 * NSA_KV_WIDTH:]
    used = 2 * GMLP_WIDTH + 2 * RNN_WIDTH + N_BRANCH * D_MODEL + 2 * NSA_KV_WIDTH + N_BRANCH * NSA_HEADS
    pad = jnp.zeros((w_in.shape[0], D_IN_PAD - used), w_in.dtype)
    w_tok = jnp.concatenate([u, v, xr, rg, mg, kv_cmp, ng, pad], axis=1).astype(BF16)
    w_feat = jnp.concatenate([q, kv_rest], axis=1).T.astype(BF16)
    return w_tok, w_feat


def _block_diag_gates(wa, wx):
    per = RNN_BLOCK // RNN_HEAD_DIM
    nblk = RNN_HEADS // per
    eye = jnp.eye(per, dtype=wa.dtype)

    def bd(w):
        w = w.reshape(nblk, per, RNN_HEAD_DIM, RNN_HEAD_DIM)
        return jnp.einsum('kpio,pq->kpiqo', w, eye).reshape(nblk, RNN_BLOCK, RNN_BLOCK)

    return jnp.concatenate([bd(wa), bd(wx)], axis=-1).astype(BF16)


def _layer(x2d, B, S, ovt, g_pre_mix, g_post_mix, g_pre_ffn, g_post_ffn, w_in,
           gmlp_ln_g, gmlp_ln_b, gmlp_ws, gmlp_bs,
           nsa_pe_k, nsa_pe_v, nsa_wk1, nsa_wk2, nsa_wv1, nsa_wv2,
           rnn_conv_w, rnn_conv_b, rnn_wa, rnn_ba, rnn_wx, rnn_bx, rnn_lam,
           w_br_a, w_br_b, w_br_c, w_o, w_ffn_in, w_ffn_out):
    G, DH = NSA_KV_GROUPS, NSA_HEAD_DIM
    row = lambda a: a.reshape(1, -1)
    w_tok, w_feat = _split_w_in(w_in)

    z = _in_proj(x2d, row(g_pre_mix), w_tok)
    zt = _in_proj_t(x2d, row(g_pre_mix), w_feat)

    y_a = _gmlp(z, row(gmlp_ln_g), row(gmlp_ln_b), gmlp_ws, gmlp_bs.T)
    y_c = _rglru(z, rnn_conv_w, row(rnn_conv_b), _block_diag_gates(rnn_wa, rnn_wx),
                 row(rnn_ba), row(rnn_bx), row(rnn_lam), B=B, S=S)

    def halves(cols):
        t = cols.reshape(B, S // CMP_STRIDE, CMP_STRIDE, G, DH).transpose(0, 3, 1, 2, 4)
        return t.reshape(B * G * (S // CMP_STRIDE), CMP_STRIDE * DH).astype(BF16)

    kc, vct = _compress(halves(z[:, OFF_KC:OFF_KC + NSA_KV_WIDTH]),
                        halves(z[:, OFF_KC + NSA_KV_WIDTH:OFF_KC + 2 * NSA_KV_WIDTH]),
                        nsa_pe_k.reshape(1, -1), nsa_pe_v.reshape(1, -1),
                        nsa_wk1.astype(BF16), nsa_wk2.astype(BF16), nsa_wv1.astype(BF16), nsa_wv2.T.astype(BF16))
    gt = z[:, OFF_NG:OFF_NG + N_BRANCH * NSA_HEADS].T
    y_b = _nsa(zt, kc, vct, gt, ovt, B=B, S=S)

    x1, hf = _merge(y_a, y_b, y_c, z, x2d, w_br_a.astype(BF16), w_br_b.astype(BF16), w_br_c.astype(BF16),
                    w_o.astype(BF16), row(g_post_mix), row(g_pre_ffn))
    return _ffn(hf, x1, w_ffn_in.astype(BF16), w_ffn_out.astype(BF16), row(g_post_ffn))


def kernel(x, g_pre_mix, g_post_mix, g_pre_ffn, g_post_ffn, w_in, gmlp_ln_g, gmlp_ln_b, gmlp_ws, gmlp_bs, nsa_pe_k, nsa_pe_v, nsa_wk1, nsa_wk2, nsa_wv1, nsa_wv2, rnn_conv_w, rnn_conv_b, rnn_wa, rnn_ba, rnn_wx, rnn_bx, rnn_lam, w_br_a, w_br_b, w_br_c, w_o, w_ffn_in, w_ffn_out):
    B, S, D = x.shape
    params = (g_pre_mix, g_post_mix, g_pre_ffn, g_post_ffn, w_in, gmlp_ln_g, gmlp_ln_b, gmlp_ws, gmlp_bs,
              nsa_pe_k, nsa_pe_v, nsa_wk1, nsa_wk2, nsa_wv1, nsa_wv2,
              rnn_conv_w, rnn_conv_b, rnn_wa, rnn_ba, rnn_wx, rnn_bx, rnn_lam,
              w_br_a, w_br_b, w_br_c, w_o, w_ffn_in, w_ffn_out)
    ovt = _overlap_t(S)
    x2d = x.reshape(B * S, D)
    for l in range(w_in.shape[0]):
        x2d = _layer(x2d, B, S, ovt, *(p[l] for p in params))
    return x2d.reshape(B, S, D)
```

```python
import functools

import jax
import jax.numpy as jnp
import numpy as np
from jax import lax
from jax.experimental import pallas as pl
from jax.experimental.pallas import tpu as pltpu

F32 = jnp.float32
BF16 = jnp.bfloat16

EPS = 1e-6
NEG_INF = -1e30
FORCE_SCORE = 1e4

D_MODEL = 1024
GMLP_WIDTH = 1024
GMLP_GROUPS = 4
GMLP_GROUP_DIM = GMLP_WIDTH // GMLP_GROUPS
GMLP_CHUNK = 128

NSA_HEADS = 16
NSA_KV_GROUPS = 4
NSA_HEAD_DIM = 64
NSA_HPG = NSA_HEADS // NSA_KV_GROUPS
NSA_Q_WIDTH = NSA_HEADS * NSA_HEAD_DIM
NSA_KV_WIDTH = NSA_KV_GROUPS * NSA_HEAD_DIM
N_BRANCH = 3
CMP_BLOCK = 32
CMP_STRIDE = 16
CMP_HIDDEN = 256
SEL_BLOCK = 64
SEL_TOP_N = 16
WINDOW = 512

RNN_WIDTH = 1024
RNN_HEADS = 16
RNN_HEAD_DIM = RNN_WIDTH // RNN_HEADS
CONV_WIDTH = 4
LRU_C = 8.0
RNN_BLOCK = 256

D_FF = 2816

OFF_U = 0
OFF_V = 1024
OFF_XR = 2048
OFF_RG = 3072
OFF_MG = 4096
OFF_KC = 7168
OFF_NG = OFF_KC + 2 * NSA_KV_WIDTH
D_IN_PAD = 8192
ROW_Q = 0
ROW_KV = NSA_Q_WIDTH
D_IN_T = NSA_Q_WIDTH + 4 * NSA_KV_WIDTH

LANES = 128
BF16_SUBLANES = 16
V7X_VMEM_LIMIT = 56 * 1024 * 1024


def _cparams(sem, vmem=V7X_VMEM_LIMIT):
    return pltpu.CompilerParams(dimension_semantics=sem, vmem_limit_bytes=vmem)


def _rms(x, g):
    ms = jnp.mean(x * x, axis=-1, keepdims=True)
    return x * lax.rsqrt(ms + EPS) * g


def _gelu(x):
    return jax.nn.gelu(x)


def _sigmoid(x):
    return jax.nn.sigmoid(x)


_NT = (((1,), (1,)), ((), ()))


def _in_proj_kernel(x_ref, g_ref, w_ref, o_ref, h_scr):
    @pl.when(pl.program_id(1) == 0)
    def _():
        h_scr[...] = _rms(x_ref[...], g_ref[...]).astype(BF16)

    o_ref[...] = jnp.dot(h_scr[...], w_ref[...], preferred_element_type=F32)


def _in_proj(x2d, g, w, *, tm=1024, tn=1024):
    T, D = x2d.shape
    N = w.shape[1]
    return pl.pallas_call(
        _in_proj_kernel,
        grid=(T // tm, N // tn),
        in_specs=[
            pl.BlockSpec((tm, D), lambda i, j: (i, 0)),
            pl.BlockSpec((1, D), lambda i, j: (0, 0)),
            pl.BlockSpec((D, tn), lambda i, j: (0, j)),
        ],
        out_specs=pl.BlockSpec((tm, tn), lambda i, j: (i, j)),
        out_shape=jax.ShapeDtypeStruct((T, N), F32),
        scratch_shapes=[pltpu.VMEM((tm, D), BF16)],
        compiler_params=_cparams(("parallel", "arbitrary")),
        name="in_proj",
    )(x2d, g, w)


def _in_proj_t_kernel(x_ref, g_ref, wt_ref, o_ref, h_scr):
    @pl.when(pl.program_id(1) == 0)
    def _():
        h_scr[...] = _rms(x_ref[...], g_ref[...]).astype(BF16)

    o_ref[...] = lax.dot_general(wt_ref[...], h_scr[...], _NT, preferred_element_type=F32).astype(o_ref.dtype)


def _in_proj_t(x2d, g, wt, *, tm=1024, tr=512):
    T, D = x2d.shape
    R = wt.shape[0]
    return pl.pallas_call(
        _in_proj_t_kernel,
        grid=(T // tm, R // tr),
        in_specs=[
            pl.BlockSpec((tm, D), lambda i, j: (i, 0)),
            pl.BlockSpec((1, D), lambda i, j: (0, 0)),
            pl.BlockSpec((tr, D), lambda i, j: (j, 0)),
        ],
        out_specs=pl.BlockSpec((tr, tm), lambda i, j: (j, i)),
        out_shape=jax.ShapeDtypeStruct((R, T), BF16),
        scratch_shapes=[pltpu.VMEM((tm, D), BF16)],
        compiler_params=_cparams(("parallel", "arbitrary")),
        name="in_proj_t",
    )(x2d, g, wt)


def _gmlp_kernel(u_ref, v_ref, lng_ref, lnb_ref, ws_ref, bst_ref, o_ref, *, n_chunks):
    C = GMLP_CHUNK
    row = lax.broadcasted_iota(jnp.int32, (C, C), 0)
    col = lax.broadcasted_iota(jnp.int32, (C, C), 1)
    causal = col <= row
    ws = [jnp.where(causal, ws_ref[g], 0.0).astype(BF16) for g in range(GMLP_GROUPS)]
    bst = bst_ref[...]
    for c in range(n_chunks):
        rows = slice(c * C, (c + 1) * C)
        gv = _gelu(v_ref[rows, :])
        mu = jnp.mean(gv, axis=-1, keepdims=True)
        d = gv - mu
        var = jnp.mean(d * d, axis=-1, keepdims=True)
        vn = (d * lax.rsqrt(var + EPS) * lng_ref[...] + lnb_ref[...]).astype(BF16)
        for g in range(GMLP_GROUPS):
            cols = slice(g * GMLP_GROUP_DIM, (g + 1) * GMLP_GROUP_DIM)
            mixed = jnp.dot(ws[g], vn[:, cols], preferred_element_type=F32) + bst[:, g:g + 1]
            o_ref[rows, cols] = (_gelu(u_ref[rows, cols]) * mixed).astype(o_ref.dtype)


def _gmlp(z, lng, lnb, ws, bst, *, n_chunks=4):
    T = z.shape[0]
    tm = GMLP_CHUNK * n_chunks
    W = GMLP_WIDTH
    return pl.pallas_call(
        functools.partial(_gmlp_kernel, n_chunks=n_chunks),
        grid=(T // tm,),
        in_specs=[
            pl.BlockSpec((tm, W), lambda i: (i, OFF_U // W)),
            pl.BlockSpec((tm, W), lambda i: (i, OFF_V // W)),
            pl.BlockSpec((1, W), lambda i: (0, 0)),
            pl.BlockSpec((1, W), lambda i: (0, 0)),
            pl.BlockSpec((GMLP_GROUPS, GMLP_CHUNK, GMLP_CHUNK), lambda i: (0, 0, 0)),
            pl.BlockSpec((GMLP_CHUNK, GMLP_GROUPS), lambda i: (0, 0)),
        ],
        out_specs=pl.BlockSpec((tm, W), lambda i: (i, 0)),
        out_shape=jax.ShapeDtypeStruct((T, W), BF16),
        compiler_params=_cparams(("parallel",)),
        name="gmlp",
    )(z, z, lng, lnb, ws, bst)


def _rglru_kernel(xr_ref, rg_ref, cw_ref, cb_ref, wab_ref, ba_ref, bx_ref, lam_ref, o_ref,
                  ext_scr, a_scr, b_scr, hc_scr, *, tt):
    W = RNN_WIDTH
    groups = tt // 8
    sub_row = lax.broadcasted_iota(jnp.int32, (groups, 8, RNN_BLOCK), 1)

    @pl.when(pl.program_id(1) == 0)
    def _():
        ext_scr[0:8, :] = jnp.zeros((8, W), F32)
        hc_scr[...] = jnp.zeros((8, W), F32)

    xr = xr_ref[...]
    ext_scr[8:8 + tt, :] = xr
    cw = cw_ref[...]
    xc = (cw[3:4] * xr + cw[2:3] * ext_scr[7:7 + tt, :] + cw[1:2] * ext_scr[6:6 + tt, :]
          + cw[0:1] * ext_scr[5:5 + tt, :] + cb_ref[...])
    ext_scr[0:8, :] = xr[tt - 8:tt, :]

    lam = lam_ref[...]
    neg = -lam
    softplus = jnp.maximum(neg, 0.0) + jnp.log1p(jnp.exp(-jnp.abs(neg)))
    for k in range(W // RNN_BLOCK):
        cols = slice(k * RNN_BLOCK, (k + 1) * RNN_BLOCK)
        xck = xc[:, cols]
        gates = jnp.dot(xck.astype(BF16), wab_ref[k], preferred_element_type=F32)
        r = _sigmoid(gates[:, :RNN_BLOCK] + ba_ref[:, cols])
        i = _sigmoid(gates[:, RNN_BLOCK:] + bx_ref[:, cols])
        log_a = -LRU_C * r * softplus[:, cols]
        a = jnp.exp(log_a)
        one_minus_a2 = -jnp.tanh(log_a) * (1.0 + a * a)
        b_in = jnp.sqrt(one_minus_a2) * (i * xck)
        a3 = a.reshape(groups, 8, RNN_BLOCK)
        b3 = b_in.reshape(groups, 8, RNN_BLOCK)
        for step in (1, 2, 4):
            keep = sub_row >= step
            a_prev = jnp.where(keep, pltpu.roll(a3, step, axis=1), 1.0)
            b_prev = jnp.where(keep, pltpu.roll(b3, step, axis=1), 0.0)
            b3 = a3 * b_prev + b3
            a3 = a3 * a_prev
        a_scr[:, cols] = a3.reshape(tt, RNN_BLOCK)
        b_scr[:, cols] = b3.reshape(tt, RNN_BLOCK)

    def carry_step(i, h_prev):
        base = pl.multiple_of(i * BF16_SUBLANES, BF16_SUBLANES)
        ys = []
        for half in range(BF16_SUBLANES // 8):
            rows = pl.ds(base + half * 8, 8)
            h = b_scr[rows, :] + a_scr[rows, :] * h_prev
            ys.append(h * _gelu(rg_ref[rows, :]))
            h_prev = h[7:8, :]
        o_ref[pl.ds(base, BF16_SUBLANES), :] = jnp.concatenate(ys, axis=0).astype(o_ref.dtype)
        return h_prev

    h_last = lax.fori_loop(0, tt // BF16_SUBLANES, carry_step, hc_scr[0:1, :])
    hc_scr[...] = jnp.broadcast_to(h_last, (8, W))


def _rglru(z, cw, cb, wab, ba, bx, lam, *, B, S, tt=512):
    T = z.shape[0]
    W = RNN_WIDTH
    nt = S // tt
    return pl.pallas_call(
        functools.partial(_rglru_kernel, tt=tt),
        grid=(B, nt),
        in_specs=[
            pl.BlockSpec((tt, W), lambda b, t: (b * nt + t, OFF_XR // W)),
            pl.BlockSpec((tt, W), lambda b, t: (b * nt + t, OFF_RG // W)),
            pl.BlockSpec((CONV_WIDTH, W), lambda b, t: (0, 0)),
            pl.BlockSpec((1, W), lambda b, t: (0, 0)),
            pl.BlockSpec((W // RNN_BLOCK, RNN_BLOCK, 2 * RNN_BLOCK), lambda b, t: (0, 0, 0)),
            pl.BlockSpec((1, W), lambda b, t: (0, 0)),
            pl.BlockSpec((1, W), lambda b, t: (0, 0)),
            pl.BlockSpec((1, W), lambda b, t: (0, 0)),
        ],
        out_specs=pl.BlockSpec((tt, W), lambda b, t: (b * nt + t, 0)),
        out_shape=jax.ShapeDtypeStruct((T, W), BF16),
        scratch_shapes=[
            pltpu.VMEM((tt + 8, W), F32),
            pltpu.VMEM((tt, W), F32),
            pltpu.VMEM((tt, W), F32),
            pltpu.VMEM((8, W), F32),
        ],
        compiler_params=_cparams(("parallel", "arbitrary")),
        name="rglru",
    )(z, z, cw, cb, wab, ba, bx, lam)


def _compress_hidden(h_ref, pe_ref, w1_ref):
    half = CMP_STRIDE * NSA_HEAD_DIM
    hm = h_ref[...].astype(BF16)
    rows = hm.shape[0]
    lo = jnp.dot(hm, w1_ref[0:half, :], preferred_element_type=F32)
    hi = jnp.dot(hm, w1_ref[half:2 * half, :], preferred_element_type=F32)
    pe = jnp.broadcast_to(pe_ref[...], (8, 2 * half)).astype(BF16)
    pe_term = jnp.dot(pe, w1_ref[...], preferred_element_type=F32)[0:1, :]
    return _gelu(lo + pltpu.roll(hi, rows - 1, axis=0) + pe_term).astype(BF16)


def _compress_kernel(hk_ref, hv_ref, pek_ref, pev_ref, wk1_ref, wk2_ref, wv1_ref, wv2t_ref, kc_ref, vct_ref):
    kc_ref[...] = jnp.dot(_compress_hidden(hk_ref, pek_ref, wk1_ref), wk2_ref[...], preferred_element_type=F32)
    vct_ref[...] = lax.dot_general(wv2t_ref[...], _compress_hidden(hv_ref, pev_ref, wv1_ref), _NT,
                                   preferred_element_type=F32)


def _compress(hk, hv, pek, pev, wk1, wk2, wv1, wv2t, *, tr=512):
    R, K = hk.shape
    DH = NSA_HEAD_DIM
    full = lambda shape: pl.BlockSpec(shape, lambda i: (0,) * len(shape))
    return pl.pallas_call(
        _compress_kernel,
        grid=(R // tr,),
        in_specs=[
            pl.BlockSpec((tr, K), lambda i: (i, 0)),
            pl.BlockSpec((tr, K), lambda i: (i, 0)),
            full(pek.shape), full(pev.shape), full(wk1.shape), full(wk2.shape), full(wv1.shape), full(wv2t.shape),
        ],
        out_specs=[pl.BlockSpec((tr, DH), lambda i: (i, 0)), pl.BlockSpec((DH, tr), lambda i: (0, i))],
        out_shape=[jax.ShapeDtypeStruct((R, DH), F32), jax.ShapeDtypeStruct((DH, R), F32)],
        compiler_params=_cparams(("parallel",)),
        name="nsa_compress",
    )(hk, hv, pek, pev, wk1, wk2, wv1, wv2t)


def _nsa_kernel(qt_ref, kc_ref, vct_ref, kst_ref, vst_ref, kwt_ref, vwt_ref, gt_ref, ovt_ref, o_ref,
                ks_scr, kw_scr, vs_scr, vw_scr, bias_scr, sa_scr, sb_scr, m_scr, acc_scr, *, tq, S):
    HP, DH = NSA_HPG, NSA_HEAD_DIM
    M = HP * tq
    tk = tq
    n_kt = S // tk
    n_sel = S // SEL_BLOCK
    slots = kc_ref.shape[0]
    v_rows = DH + BF16_SUBLANES
    g_idx = pl.program_id(1)
    qi = pl.program_id(2)
    t0 = qi * tq

    @pl.when(qi == 0)
    def _():
        ks_scr[...] = kst_ref[...].T
        kw_scr[...] = kwt_ref[...].T
        ones = jnp.ones((BF16_SUBLANES, tk), BF16)
        for kt in range(n_kt):
            vs_scr[kt, 0:DH, :] = vst_ref[:, kt * tk:(kt + 1) * tk]
            vs_scr[kt, DH:v_rows, :] = ones
            vw_scr[kt, 0:DH, :] = vwt_ref[:, kt * tk:(kt + 1) * tk]
            vw_scr[kt, DH:v_rows, :] = ones

    def per_head(x):
        return jnp.concatenate([x] * HP, axis=1)

    qt = jnp.concatenate([qt_ref[h * DH:(h + 1) * DH, :] for h in range(HP)], axis=1)
    qt = qt * jnp.asarray(DH ** -0.5, BF16)

    sc = jnp.dot(kc_ref[...].astype(BF16), qt, preferred_element_type=F32)
    n_idx = lax.broadcasted_iota(jnp.int32, (slots, tq), 0)
    t_idx = t0 + lax.broadcasted_iota(jnp.int32, (slots, tq), 1)
    valid = per_head(jnp.where(n_idx * CMP_STRIDE + (CMP_BLOCK - 1) <= t_idx, 1.0, 0.0)) > 0.5
    sc = jnp.where(valid, sc, NEG_INF)
    mx = jnp.max(sc, axis=0, keepdims=True)
    p = jnp.where(valid, jnp.exp(sc - mx), 0.0)
    den = jnp.sum(p, axis=0, keepdims=True)
    p_c = p * (1.0 / jnp.where(den > 0.0, den, 1.0))
    o_cmp = jnp.dot(vct_ref[...].astype(BF16), p_c.astype(BF16), preferred_element_type=F32)

    p_sum = p_c[:, 0:tq]
    for h in range(1, HP):
        p_sum = p_sum + p_c[:, h * tq:(h + 1) * tq]
    imp = jnp.dot(ovt_ref[...], p_sum, preferred_element_type=F32, precision=lax.Precision.HIGHEST)
    j_idx = lax.broadcasted_iota(jnp.int32, (n_sel, tq), 0)
    cur = (t0 + lax.broadcasted_iota(jnp.int32, (n_sel, tq), 1)) // SEL_BLOCK
    forced = (j_idx == 0) | (j_idx == cur) | (j_idx == cur - 1)
    imp = jnp.where(forced, FORCE_SCORE, jnp.where(j_idx > cur, NEG_INF, imp))
    rank = jnp.zeros((n_sel, tq), F32)
    for i in range(n_sel):
        ci = imp[i:i + 1, :]
        ge = jnp.where(ci >= imp, 1.0, 0.0)
        gt = jnp.where(ci > imp, 1.0, 0.0)
        rank = rank + jnp.where(j_idx > i, ge, gt)
    sel = jnp.where(rank < float(min(SEL_TOP_N, n_sel)), 1.0, 0.0)

    per_tile = tk // SEL_BLOCK
    kloc = lax.broadcasted_iota(jnp.int32, (tk, tq), 0)
    tloc = lax.broadcasted_iota(jnp.int32, (tk, tq), 1)
    for kt in range(n_kt):
        @pl.when(kt <= qi)
        def _():
            hit = jnp.concatenate(
                [jnp.broadcast_to(sel[kt * per_tile + jj:kt * per_tile + jj + 1, :], (SEL_BLOCK, tq))
                 for jj in range(per_tile)], axis=0)
            bias_scr[kt] = jnp.where((hit > 0.5) & (kt * tk + kloc <= t0 + tloc), 0.0, NEG_INF)

    def scores(k_scr, kt, bias, dst):
        start = pl.multiple_of(kt * tk, tk)
        s = jnp.dot(k_scr[pl.ds(start, tk), :], qt, preferred_element_type=F32)
        dst[...] = s if bias is None else s + per_head(bias)

    def absorb(v_ext, src):
        s = src[...]
        m_old = m_scr[...]
        m_new = jnp.maximum(m_old, jnp.max(s, axis=0, keepdims=True))
        alpha = jnp.exp(m_old - m_new)
        pr = jnp.exp(s - m_new).astype(BF16)
        m_scr[...] = m_new
        acc_scr[...] = alpha * acc_scr[...] + jnp.dot(v_ext, pr, preferred_element_type=F32)

    def reset():
        m_scr[...] = jnp.full((1, M), NEG_INF, F32)
        acc_scr[...] = jnp.zeros((v_rows, M), F32)

    def result():
        acc = acc_scr[...]
        return acc[0:DH, :] * (1.0 / acc[DH:DH + 1, :])

    reset()
    n_tiles = qi + 1

    def slc_scores(kt, dst):
        kt = jnp.minimum(kt, qi)
        scores(ks_scr, kt, bias_scr[kt], dst)

    slc_scores(0, sa_scr)

    def slc_pair(j, carry):
        slc_scores(2 * j + 1, sb_scr)
        absorb(vs_scr[2 * j], sa_scr)
        slc_scores(2 * j + 2, sa_scr)
        absorb(vs_scr[2 * j + 1], sb_scr)
        return carry

    lax.fori_loop(0, n_tiles // 2, slc_pair, 0)

    @pl.when(n_tiles % 2 == 1)
    def _():
        absorb(vs_scr[qi], sa_scr)

    o_slc = result()

    reset()
    n_back = WINDOW // tk
    diag_bias = jnp.where(kloc <= tloc, 0.0, NEG_INF)
    far_bias = jnp.where(kloc > tloc, 0.0, NEG_INF)

    def win_bias(d):
        return diag_bias if d == 0 else (far_bias if d == n_back else None)

    def window(n_win):
        bufs = (sa_scr, sb_scr)
        scores(kw_scr, qi, win_bias(0), bufs[0])
        for d in range(n_win):
            if d + 1 < n_win:
                scores(kw_scr, qi - (d + 1), win_bias(d + 1), bufs[(d + 1) % 2])
            absorb(vw_scr[qi - d], bufs[d % 2])

    for n_win in range(1, n_back + 2):
        @pl.when((qi == n_win - 1) if n_win <= n_back else (qi >= n_back))
        def _():
            window(n_win)

    o_win = result()

    def gate(branch):
        rows = [gt_ref[pl.ds(g_idx * (HP * N_BRANCH) + h * N_BRANCH + branch, 1), :] for h in range(HP)]
        return _sigmoid(jnp.concatenate(rows, axis=1))

    o = gate(0) * o_cmp + gate(1) * o_slc + gate(2) * o_win
    o_ref[...] = jnp.concatenate([o[:, h * tq:(h + 1) * tq].T for h in range(HP)], axis=1).astype(o_ref.dtype)


def _nsa(zt, kc, vct, gt, ovt, *, B, S, tq=256):
    G, HP, DH = NSA_KV_GROUPS, NSA_HPG, NSA_HEAD_DIM
    T = B * S
    nq = S // tq
    slots = S // CMP_STRIDE
    n_sel = S // SEL_BLOCK
    n_kt = S // tq
    v_rows = DH + BF16_SUBLANES

    def kv_spec(which):
        base = (ROW_KV + which * NSA_KV_WIDTH) // DH
        return pl.BlockSpec((DH, S), lambda b, g, i: (base + g, b))

    return pl.pallas_call(
        functools.partial(_nsa_kernel, tq=tq, S=S),
        grid=(B, G, nq),
        in_specs=[
            pl.BlockSpec((HP * DH, tq), lambda b, g, i: (g, b * nq + i)),
            pl.BlockSpec((slots, DH), lambda b, g, i: (b * G + g, 0)),
            pl.BlockSpec((DH, slots), lambda b, g, i: (0, b * G + g)),
            kv_spec(0), kv_spec(1), kv_spec(2), kv_spec(3),
            pl.BlockSpec((G * HP * N_BRANCH, tq), lambda b, g, i: (0, b * nq + i)),
            pl.BlockSpec((n_sel, slots), lambda b, g, i: (0, 0)),
        ],
        out_specs=pl.BlockSpec((tq, HP * DH), lambda b, g, i: (b * nq + i, g)),
        out_shape=jax.ShapeDtypeStruct((T, NSA_Q_WIDTH), BF16),
        scratch_shapes=[
            pltpu.VMEM((S, DH), BF16),
            pltpu.VMEM((S, DH), BF16),
            pltpu.VMEM((n_kt, v_rows, tq), BF16),
            pltpu.VMEM((n_kt, v_rows, tq), BF16),
            pltpu.VMEM((n_kt, tq, tq), F32),
            pltpu.VMEM((tq, HP * tq), F32),
            pltpu.VMEM((tq, HP * tq), F32),
            pltpu.VMEM((1, HP * tq), F32),
            pltpu.VMEM((v_rows, HP * tq), F32),
        ],
        compiler_params=_cparams(("parallel", "parallel", "arbitrary")),
        name="nsa_attention",
    )(zt, kc, vct, zt, zt, zt, zt, gt, ovt)


def _merge_kernel(ya_ref, yb_ref, yc_ref, mga_ref, mgb_ref, mgc_ref, x_ref, wa_ref, wb_ref, wc_ref, wo_ref,
                  gpost_ref, gpre_ref, x1_ref, hf_ref):
    merged = _sigmoid(mga_ref[...]) * jnp.dot(ya_ref[...], wa_ref[...], preferred_element_type=F32)
    merged = merged + _sigmoid(mgb_ref[...]) * jnp.dot(yb_ref[...], wb_ref[...], preferred_element_type=F32)
    merged = merged + _sigmoid(mgc_ref[...]) * jnp.dot(yc_ref[...], wc_ref[...], preferred_element_type=F32)
    y = jnp.dot(merged.astype(BF16), wo_ref[...], preferred_element_type=F32)
    x1 = x_ref[...] + _rms(y, gpost_ref[...])
    x1_ref[...] = x1
    hf_ref[...] = _rms(x1, gpre_ref[...]).astype(BF16)


def _merge(ya, yb, yc, z, x2d, wa, wb, wc, wo, gpost, gpre, *, tm=512):
    T, D = x2d.shape
    row = lambda c: pl.BlockSpec((tm, D), lambda i: (i, c))
    wfull = pl.BlockSpec((D, D), lambda i: (0, 0))
    vec = pl.BlockSpec((1, D), lambda i: (0, 0))
    mg0 = OFF_MG // D
    return pl.pallas_call(
        _merge_kernel,
        grid=(T // tm,),
        in_specs=[row(0), row(0), row(0), row(mg0), row(mg0 + 1), row(mg0 + 2), row(0),
                  wfull, wfull, wfull, wfull, vec, vec],
        out_specs=[row(0), row(0)],
        out_shape=[jax.ShapeDtypeStruct((T, D), F32), jax.ShapeDtypeStruct((T, D), BF16)],
        compiler_params=_cparams(("parallel",)),
        name="merge",
    )(ya, yb, yc, z, z, z, x2d, wa, wb, wc, wo, gpost, gpre)


def _ffn_kernel(hf_ref, x1_ref, win_ref, wout_ref, gpost_ref, o_ref, acc_scr, *, fc):
    hf = hf_ref[...]
    n_chunks = D_FF // fc
    for c in range(n_chunks):
        gate = jnp.dot(hf, win_ref[:, c * fc:(c + 1) * fc], preferred_element_type=F32)
        up = jnp.dot(hf, win_ref[:, D_FF + c * fc:D_FF + (c + 1) * fc], preferred_element_type=F32)
        act = (gate * _sigmoid(gate) * up).astype(BF16)
        part = jnp.dot(act, wout_ref[c * fc:(c + 1) * fc, :], preferred_element_type=F32)
        if c == 0:
            acc_scr[...] = part
        else:
            acc_scr[...] += part
    o_ref[...] = x1_ref[...] + _rms(acc_scr[...], gpost_ref[...])


def _ffn(hf, x1, win, wout, gpost, *, tm=512, fc=1408):
    T, D = x1.shape
    row = pl.BlockSpec((tm, D), lambda i: (i, 0))
    return pl.pallas_call(
        functools.partial(_ffn_kernel, fc=fc),
        grid=(T // tm,),
        in_specs=[row, row,
                  pl.BlockSpec(win.shape, lambda i: (0, 0), pipeline_mode=pl.Buffered(1)),
                  pl.BlockSpec(wout.shape, lambda i: (0, 0), pipeline_mode=pl.Buffered(1)),
                  pl.BlockSpec((1, D), lambda i: (0, 0))],
        out_specs=row,
        out_shape=jax.ShapeDtypeStruct((T, D), F32),
        scratch_shapes=[pltpu.VMEM((tm, D), F32)],
        compiler_params=_cparams(("parallel",)),
        name="ffn",
    )(hf, x1, win, wout, gpost)


def _overlap_t(S):
    slots = S // CMP_STRIDE
    n_sel = S // SEL_BLOCK
    c_start = np.arange(slots) * CMP_STRIDE
    s_start = np.arange(n_sel) * SEL_BLOCK
    overlap = np.clip(np.minimum(c_start[:, None] + CMP_BLOCK, s_start[None, :] + SEL_BLOCK)
                      - np.maximum(c_start[:, None], s_start[None, :]), 0, None) / CMP_BLOCK
    return jnp.asarray(overlap.T, F32)


def _split_w_in(w_in):
    s = np.cumsum([0, GMLP_WIDTH, GMLP_WIDTH, NSA_Q_WIDTH, 6 * NSA_KV_WIDTH, N_BRANCH * NSA_HEADS,
                   RNN_WIDTH, RNN_WIDTH, N_BRANCH * D_MODEL])
    u, v, q, kv, ng, xr, rg, mg = (w_in[:, s[i]:s[i + 1]] for i in range(8))
    kv_cmp, kv_rest = kv[:, :2 * NSA_KV_WIDTH], kv[:, 2 * NSA_KV_WIDTH:]
    used = 2 * GMLP_WIDTH + 2 * RNN_WIDTH + N_BRANCH * D_MODEL + 2 * NSA_KV_WIDTH + N_BRANCH * NSA_HEADS
    pad = jnp.zeros((w_in.shape[0], D_IN_PAD - used), w_in.dtype)
    w_tok = jnp.concatenate([u, v, xr, rg, mg, kv_cmp, ng, pad], axis=1).astype(BF16)
    w_feat = jnp.concatenate([q, kv_rest], axis=1).T.astype(BF16)
    return w_tok, w_feat


def _block_diag_gates(wa, wx):
    per = RNN_BLOCK // RNN_HEAD_DIM
    nblk = RNN_HEADS // per
    eye = jnp.eye(per, dtype=wa.dtype)

    def bd(w):
        w = w.reshape(nblk, per, RNN_HEAD_DIM, RNN_HEAD_DIM)
        return jnp.einsum('kpio,pq->kpiqo', w, eye).reshape(nblk, RNN_BLOCK, RNN_BLOCK)

    return jnp.concatenate([bd(wa), bd(wx)], axis=-1).astype(BF16)


def _layer(x2d, B, S, ovt, g_pre_mix, g_post_mix, g_pre_ffn, g_post_ffn, w_in,
           gmlp_ln_g, gmlp_ln_b, gmlp_ws, gmlp_bs,
           nsa_pe_k, nsa_pe_v, nsa_wk1, nsa_wk2, nsa_wv1, nsa_wv2,
           rnn_conv_w, rnn_conv_b, rnn_wa, rnn_ba, rnn_wx, rnn_bx, rnn_lam,
           w_br_a, w_br_b, w_br_c, w_o, w_ffn_in, w_ffn_out):
    G, DH = NSA_KV_GROUPS, NSA_HEAD_DIM
    row = lambda a: a.reshape(1, -1)
    w_tok, w_feat = _split_w_in(w_in)

    z = _in_proj(x2d, row(g_pre_mix), w_tok)
    zt = _in_proj_t(x2d, row(g_pre_mix), w_feat)

    y_a = _gmlp(z, row(gmlp_ln_g), row(gmlp_ln_b), gmlp_ws, gmlp_bs.T)
    y_c = _rglru(z, rnn_conv_w, row(rnn_conv_b), _block_diag_gates(rnn_wa, rnn_wx),
                 row(rnn_ba), row(rnn_bx), row(rnn_lam), B=B, S=S)

    def halves(cols):
        t = cols.reshape(B, S // CMP_STRIDE, CMP_STRIDE, G, DH).transpose(0, 3, 1, 2, 4)
        return t.reshape(B * G * (S // CMP_STRIDE), CMP_STRIDE * DH)

    kc, vct = _compress(halves(z[:, OFF_KC:OFF_KC + NSA_KV_WIDTH]),
                        halves(z[:, OFF_KC + NSA_KV_WIDTH:OFF_KC + 2 * NSA_KV_WIDTH]),
                        nsa_pe_k.reshape(1, -1), nsa_pe_v.reshape(1, -1),
                        nsa_wk1.astype(BF16), nsa_wk2.astype(BF16), nsa_wv1.astype(BF16), nsa_wv2.T.astype(BF16))
    gt = z[:, OFF_NG:OFF_NG + N_BRANCH * NSA_HEADS].T
    y_b = _nsa(zt, kc, vct, gt, ovt, B=B, S=S)

    x1, hf = _merge(y_a, y_b, y_c, z, x2d, w_br_a.astype(BF16), w_br_b.astype(BF16), w_br_c.astype(BF16),
                    w_o.astype(BF16), row(g_post_mix), row(g_pre_ffn))
    return _ffn(hf, x1, w_ffn_in.astype(BF16), w_ffn_out.astype(BF16), row(g_post_ffn))


def kernel(x, g_pre_mix, g_post_mix, g_pre_ffn, g_post_ffn, w_in, gmlp_ln_g, gmlp_ln_b, gmlp_ws, gmlp_bs, nsa_pe_k, nsa_pe_v, nsa_wk1, nsa_wk2, nsa_wv1, nsa_wv2, rnn_conv_w, rnn_conv_b, rnn_wa, rnn_ba, rnn_wx, rnn_bx, rnn_lam, w_br_a, w_br_b, w_br_c, w_o, w_ffn_in, w_ffn_out):
    B, S, D = x.shape
    params = (g_pre_mix, g_post_mix, g_pre_ffn, g_post_ffn, w_in, gmlp_ln_g, gmlp_ln_b, gmlp_ws, gmlp_bs,
              nsa_pe_k, nsa_pe_v, nsa_wk1, nsa_wk2, nsa_wv1, nsa_wv2,
              rnn_conv_w, rnn_conv_b, rnn_wa, rnn_ba, rnn_wx, rnn_bx, rnn_lam,
              w_br_a, w_br_b, w_br_c, w_o, w_ffn_in, w_ffn_out)
    ovt = _overlap_t(S)
    x2d = x.reshape(B * S, D)
    for l in range(w_in.shape[0]):
        x2d = _layer(x2d, B, S, ovt, *(p[l] for p in params))
    return x2d.reshape(B, S, D)
```

```python
import functools

import jax
import jax.numpy as jnp
import numpy as np
from jax import lax
from jax.experimental import pallas as pl
from jax.experimental.pallas import tpu as pltpu

F32 = jnp.float32
BF16 = jnp.bfloat16

EPS = 1e-6
NEG_INF = -1e30
FORCE_SCORE = 1e4

D_MODEL = 1024
GMLP_WIDTH = 1024
GMLP_GROUPS = 4
GMLP_GROUP_DIM = GMLP_WIDTH // GMLP_GROUPS
GMLP_CHUNK = 128

NSA_HEADS = 16
NSA_KV_GROUPS = 4
NSA_HEAD_DIM = 64
NSA_HPG = NSA_HEADS // NSA_KV_GROUPS
NSA_Q_WIDTH = NSA_HEADS * NSA_HEAD_DIM
NSA_KV_WIDTH = NSA_KV_GROUPS * NSA_HEAD_DIM
N_BRANCH = 3
CMP_BLOCK = 32
CMP_STRIDE = 16
CMP_HIDDEN = 256
SEL_BLOCK = 64
SEL_TOP_N = 16
WINDOW = 512

RNN_WIDTH = 1024
RNN_HEADS = 16
RNN_HEAD_DIM = RNN_WIDTH // RNN_HEADS
CONV_WIDTH = 4
LRU_C = 8.0
RNN_BLOCK = 256

D_FF = 2816

OFF_U = 0
OFF_V = 1024
OFF_XR = 2048
OFF_RG = 3072
OFF_MG = 4096
OFF_KC = 7168
OFF_NG = OFF_KC + 2 * NSA_KV_WIDTH
NG_PAD = 512
OFF_Q = OFF_NG + NG_PAD
OFF_KV = OFF_Q + NSA_Q_WIDTH
D_IN_PAD = OFF_KV + 4 * NSA_KV_WIDTH

LANES = 128
BF16_SUBLANES = 16
V7X_VMEM_LIMIT = 56 * 1024 * 1024


def _cparams(sem, vmem=V7X_VMEM_LIMIT):
    return pltpu.CompilerParams(dimension_semantics=sem, vmem_limit_bytes=vmem)


def _rms(x, g):
    ms = jnp.mean(x * x, axis=-1, keepdims=True)
    return x * lax.rsqrt(ms + EPS) * g


def _gelu(x):
    return jax.nn.gelu(x)


def _sigmoid(x):
    return jax.nn.sigmoid(x)


_NT = (((1,), (1,)), ((), ()))


def _in_proj_kernel(x_ref, g_ref, w_ref, o_ref, h_scr):
    @pl.when(pl.program_id(1) == 0)
    def _():
        h_scr[...] = _rms(x_ref[...], g_ref[...]).astype(BF16)

    o_ref[...] = jnp.dot(h_scr[...], w_ref[...], preferred_element_type=F32).astype(o_ref.dtype)


def _in_proj(x2d, g, w, *, tm=1024, tn=1024):
    T, D = x2d.shape
    N = w.shape[1]
    return pl.pallas_call(
        _in_proj_kernel,
        grid=(T // tm, N // tn),
        in_specs=[
            pl.BlockSpec((tm, D), lambda i, j: (i, 0)),
            pl.BlockSpec((1, D), lambda i, j: (0, 0)),
            pl.BlockSpec((D, tn), lambda i, j: (0, j)),
        ],
        out_specs=pl.BlockSpec((tm, tn), lambda i, j: (i, j)),
        out_shape=jax.ShapeDtypeStruct((T, N), BF16),
        scratch_shapes=[pltpu.VMEM((tm, D), BF16)],
        compiler_params=_cparams(("parallel", "arbitrary")),
        name="in_proj",
    )(x2d, g, w)


def _gmlp_kernel(u_ref, v_ref, lng_ref, lnb_ref, ws_ref, bst_ref, o_ref, *, n_chunks):
    C = GMLP_CHUNK
    row = lax.broadcasted_iota(jnp.int32, (C, C), 0)
    col = lax.broadcasted_iota(jnp.int32, (C, C), 1)
    causal = col <= row
    ws = [jnp.where(causal, ws_ref[g], 0.0).astype(BF16) for g in range(GMLP_GROUPS)]
    bst = bst_ref[...]
    for c in range(n_chunks):
        rows = slice(c * C, (c + 1) * C)
        gv = _gelu(v_ref[rows, :].astype(F32))
        mu = jnp.mean(gv, axis=-1, keepdims=True)
        d = gv - mu
        var = jnp.mean(d * d, axis=-1, keepdims=True)
        vn = (d * lax.rsqrt(var + EPS) * lng_ref[...] + lnb_ref[...]).astype(BF16)
        for g in range(GMLP_GROUPS):
            cols = slice(g * GMLP_GROUP_DIM, (g + 1) * GMLP_GROUP_DIM)
            mixed = jnp.dot(ws[g], vn[:, cols], preferred_element_type=F32) + bst[:, g:g + 1]
            o_ref[rows, cols] = (_gelu(u_ref[rows, cols].astype(F32)) * mixed).astype(o_ref.dtype)


def _gmlp(z, lng, lnb, ws, bst, *, n_chunks=4):
    T = z.shape[0]
    tm = GMLP_CHUNK * n_chunks
    W = GMLP_WIDTH
    return pl.pallas_call(
        functools.partial(_gmlp_kernel, n_chunks=n_chunks),
        grid=(T // tm,),
        in_specs=[
            pl.BlockSpec((tm, W), lambda i: (i, OFF_U // W)),
            pl.BlockSpec((tm, W), lambda i: (i, OFF_V // W)),
            pl.BlockSpec((1, W), lambda i: (0, 0)),
            pl.BlockSpec((1, W), lambda i: (0, 0)),
            pl.BlockSpec((GMLP_GROUPS, GMLP_CHUNK, GMLP_CHUNK), lambda i: (0, 0, 0)),
            pl.BlockSpec((GMLP_CHUNK, GMLP_GROUPS), lambda i: (0, 0)),
        ],
        out_specs=pl.BlockSpec((tm, W), lambda i: (i, 0)),
        out_shape=jax.ShapeDtypeStruct((T, W), BF16),
        compiler_params=_cparams(("parallel",)),
        name="gmlp",
    )(z, z, lng, lnb, ws, bst)


def _rglru_kernel(xr_ref, rg_ref, cw_ref, cb_ref, wab_ref, ba_ref, bx_ref, lam_ref, o_ref,
                  ext_scr, a_scr, b_scr, hc_scr, *, tt):
    W = RNN_WIDTH
    groups = tt // 8
    sub_row = lax.broadcasted_iota(jnp.int32, (groups, 8, RNN_BLOCK), 1)

    @pl.when(pl.program_id(1) == 0)
    def _():
        ext_scr[0:8, :] = jnp.zeros((8, W), F32)
        hc_scr[...] = jnp.zeros((8, W), F32)

    xr = xr_ref[...].astype(F32)
    ext_scr[8:8 + tt, :] = xr
    cw = cw_ref[...]
    xc = (cw[3:4] * xr + cw[2:3] * ext_scr[7:7 + tt, :] + cw[1:2] * ext_scr[6:6 + tt, :]
          + cw[0:1] * ext_scr[5:5 + tt, :] + cb_ref[...])
    ext_scr[0:8, :] = xr[tt - 8:tt, :]

    lam = lam_ref[...]
    neg = -lam
    softplus = jnp.maximum(neg, 0.0) + jnp.log1p(jnp.exp(-jnp.abs(neg)))
    for k in range(W // RNN_BLOCK):
        cols = slice(k * RNN_BLOCK, (k + 1) * RNN_BLOCK)
        xck = xc[:, cols]
        gates = jnp.dot(xck.astype(BF16), wab_ref[k], preferred_element_type=F32)
        r = _sigmoid(gates[:, :RNN_BLOCK] + ba_ref[:, cols])
        i = _sigmoid(gates[:, RNN_BLOCK:] + bx_ref[:, cols])
        log_a = -LRU_C * r * softplus[:, cols]
        a = jnp.exp(log_a)
        one_minus_a2 = -jnp.tanh(log_a) * (1.0 + a * a)
        b_in = jnp.sqrt(one_minus_a2) * (i * xck)
        a3 = a.reshape(groups, 8, RNN_BLOCK)
        b3 = b_in.reshape(groups, 8, RNN_BLOCK)
        for step in (1, 2, 4):
            keep = sub_row >= step
            a_prev = jnp.where(keep, pltpu.roll(a3, step, axis=1), 1.0)
            b_prev = jnp.where(keep, pltpu.roll(b3, step, axis=1), 0.0)
            b3 = a3 * b_prev + b3
            a3 = a3 * a_prev
        a_scr[:, cols] = a3.reshape(tt, RNN_BLOCK)
        b_scr[:, cols] = b3.reshape(tt, RNN_BLOCK)

    def carry_step(i, h_prev):
        base = pl.multiple_of(i * BF16_SUBLANES, BF16_SUBLANES)
        gate = _gelu(rg_ref[pl.ds(base, BF16_SUBLANES), :].astype(F32))
        ys = []
        for half in range(BF16_SUBLANES // 8):
            rows = pl.ds(base + half * 8, 8)
            h = b_scr[rows, :] + a_scr[rows, :] * h_prev
            ys.append(h * gate[half * 8:(half + 1) * 8, :])
            h_prev = h[7:8, :]
        o_ref[pl.ds(base, BF16_SUBLANES), :] = jnp.concatenate(ys, axis=0).astype(o_ref.dtype)
        return h_prev

    h_last = lax.fori_loop(0, tt // BF16_SUBLANES, carry_step, hc_scr[0:1, :])
    hc_scr[...] = jnp.broadcast_to(h_last, (8, W))


def _rglru(z, cw, cb, wab, ba, bx, lam, *, B, S, tt=512):
    T = z.shape[0]
    W = RNN_WIDTH
    nt = S // tt
    return pl.pallas_call(
        functools.partial(_rglru_kernel, tt=tt),
        grid=(B, nt),
        in_specs=[
            pl.BlockSpec((tt, W), lambda b, t: (b * nt + t, OFF_XR // W)),
            pl.BlockSpec((tt, W), lambda b, t: (b * nt + t, OFF_RG // W)),
            pl.BlockSpec((CONV_WIDTH, W), lambda b, t: (0, 0)),
            pl.BlockSpec((1, W), lambda b, t: (0, 0)),
            pl.BlockSpec((W // RNN_BLOCK, RNN_BLOCK, 2 * RNN_BLOCK), lambda b, t: (0, 0, 0)),
            pl.BlockSpec((1, W), lambda b, t: (0, 0)),
            pl.BlockSpec((1, W), lambda b, t: (0, 0)),
            pl.BlockSpec((1, W), lambda b, t: (0, 0)),
        ],
        out_specs=pl.BlockSpec((tt, W), lambda b, t: (b * nt + t, 0)),
        out_shape=jax.ShapeDtypeStruct((T, W), BF16),
        scratch_shapes=[
            pltpu.VMEM((tt + 8, W), F32),
            pltpu.VMEM((tt, W), F32),
            pltpu.VMEM((tt, W), F32),
            pltpu.VMEM((8, W), F32),
        ],
        compiler_params=_cparams(("parallel", "arbitrary")),
        name="rglru",
    )(z, z, cw, cb, wab, ba, bx, lam)


def _compress_hidden(h_ref, pe_ref, w1_ref):
    half = CMP_STRIDE * NSA_HEAD_DIM
    hm = h_ref[...].astype(BF16)
    rows = hm.shape[0]
    lo = jnp.dot(hm, w1_ref[0:half, :], preferred_element_type=F32)
    hi = jnp.dot(hm, w1_ref[half:2 * half, :], preferred_element_type=F32)
    pe = jnp.broadcast_to(pe_ref[...], (8, 2 * half)).astype(BF16)
    pe_term = jnp.dot(pe, w1_ref[...], preferred_element_type=F32)[0:1, :]
    return _gelu(lo + pltpu.roll(hi, rows - 1, axis=0) + pe_term).astype(BF16)


def _compress_kernel(hk_ref, hv_ref, pek_ref, pev_ref, wk1_ref, wk2_ref, wv1_ref, wv2t_ref, kc_ref, vct_ref):
    kc_ref[...] = jnp.dot(_compress_hidden(hk_ref, pek_ref, wk1_ref), wk2_ref[...], preferred_element_type=F32)
    vct_ref[...] = lax.dot_general(wv2t_ref[...], _compress_hidden(hv_ref, pev_ref, wv1_ref), _NT,
                                   preferred_element_type=F32)


def _compress(hk, hv, pek, pev, wk1, wk2, wv1, wv2t, *, tr=512):
    R, K = hk.shape
    DH = NSA_HEAD_DIM
    full = lambda shape: pl.BlockSpec(shape, lambda i: (0,) * len(shape))
    return pl.pallas_call(
        _compress_kernel,
        grid=(R // tr,),
        in_specs=[
            pl.BlockSpec((tr, K), lambda i: (i, 0)),
            pl.BlockSpec((tr, K), lambda i: (i, 0)),
            full(pek.shape), full(pev.shape), full(wk1.shape), full(wk2.shape), full(wv1.shape), full(wv2t.shape),
        ],
        out_specs=[pl.BlockSpec((tr, DH), lambda i: (i, 0)), pl.BlockSpec((DH, tr), lambda i: (0, i))],
        out_shape=[jax.ShapeDtypeStruct((R, DH), F32), jax.ShapeDtypeStruct((DH, R), F32)],
        compiler_params=_cparams(("parallel",)),
        name="nsa_compress",
    )(hk, hv, pek, pev, wk1, wk2, wv1, wv2t)


def _nsa_kernel(q_ref, kc_ref, vct_ref, ks_ref, vs_ref, kw_ref, vw_ref, gate_ref, ovt_ref, o_ref,
                ks_scr, kw_scr, vs_scr, vw_scr, gt_scr, bias_scr, sa_scr, sb_scr, m_scr, acc_scr, *, tq, S):
    HP, DH = NSA_HPG, NSA_HEAD_DIM
    M = HP * tq
    tk = tq
    n_kt = S // tk
    n_sel = S // SEL_BLOCK
    slots = kc_ref.shape[0]
    v_rows = DH + BF16_SUBLANES
    g_idx = pl.program_id(1)
    qi = pl.program_id(2)
    t0 = qi * tq

    for g in range(NSA_KV_GROUPS):
        @pl.when((qi == 0) & (g_idx == g))
        def _():
            lanes = slice(g * DH, (g + 1) * DH)
            ks_scr[...] = ks_ref[:, lanes]
            kw_scr[...] = kw_ref[:, lanes]
            ones = jnp.ones((BF16_SUBLANES, tk), BF16)
            for v_ref, v_scr in ((vs_ref, vs_scr), (vw_ref, vw_scr)):
                vt = v_ref[:, lanes].T
                for kt in range(n_kt):
                    v_scr[kt, 0:DH, :] = vt[:, kt * tk:(kt + 1) * tk]
                    v_scr[kt, DH:v_rows, :] = ones

    def per_head(x):
        return jnp.concatenate([x] * HP, axis=1)

    q_t = q_ref[...].T
    qt = jnp.concatenate([q_t[h * DH:(h + 1) * DH, :] for h in range(HP)], axis=1)
    qt = qt * jnp.asarray(DH ** -0.5, BF16)

    def scores(k_scr, kt, dst):
        start = pl.multiple_of(kt * tk, tk)
        dst[...] = jnp.dot(k_scr[pl.ds(start, tk), :], qt, preferred_element_type=F32)

    scores(ks_scr, 0, sa_scr)

    sc = jnp.dot(kc_ref[...].astype(BF16), qt, preferred_element_type=F32)
    n_idx = lax.broadcasted_iota(jnp.int32, (slots, tq), 0)
    t_idx = t0 + lax.broadcasted_iota(jnp.int32, (slots, tq), 1)
    valid = per_head(jnp.where(n_idx * CMP_STRIDE + (CMP_BLOCK - 1) <= t_idx, 1.0, 0.0)) > 0.5
    sc = jnp.where(valid, sc, NEG_INF)
    mx = jnp.max(sc, axis=0, keepdims=True)
    p = jnp.where(valid, jnp.exp(sc - mx), 0.0)
    den = jnp.sum(p, axis=0, keepdims=True)
    p_c = p * (1.0 / jnp.where(den > 0.0, den, 1.0))
    o_cmp = jnp.dot(vct_ref[...].astype(BF16), p_c.astype(BF16), preferred_element_type=F32)

    p_sum = p_c[:, 0:tq]
    for h in range(1, HP):
        p_sum = p_sum + p_c[:, h * tq:(h + 1) * tq]
    imp = jnp.dot(ovt_ref[...], p_sum, preferred_element_type=F32, precision=lax.Precision.HIGHEST)
    j_idx = lax.broadcasted_iota(jnp.int32, (n_sel, tq), 0)
    cur = (t0 + lax.broadcasted_iota(jnp.int32, (n_sel, tq), 1)) // SEL_BLOCK
    forced = (j_idx == 0) | (j_idx == cur) | (j_idx == cur - 1)
    imp = jnp.where(forced, FORCE_SCORE, jnp.where(j_idx > cur, NEG_INF, imp))
    rank = jnp.zeros((n_sel, tq), F32)
    for i in range(n_sel):
        ci = imp[i:i + 1, :]
        ge = jnp.where(ci >= imp, 1.0, 0.0)
        gt = jnp.where(ci > imp, 1.0, 0.0)
        rank = rank + jnp.where(j_idx > i, ge, gt)
    sel = jnp.where(rank < float(min(SEL_TOP_N, n_sel)), 1.0, 0.0)

    per_tile = tk // SEL_BLOCK
    kloc = lax.broadcasted_iota(jnp.int32, (tk, tq), 0)
    tloc = lax.broadcasted_iota(jnp.int32, (tk, tq), 1)
    for kt in range(n_kt):
        @pl.when(kt <= qi)
        def _():
            hit = jnp.concatenate(
                [jnp.broadcast_to(sel[kt * per_tile + jj:kt * per_tile + jj + 1, :], (SEL_BLOCK, tq))
                 for jj in range(per_tile)], axis=0)
            bias_scr[kt] = jnp.where((hit > 0.5) & (kt * tk + kloc <= t0 + tloc), 0.0, NEG_INF)

    def absorb(v_ext, src, bias):
        s = src[...] if bias is None else src[...] + per_head(bias)
        m_old = m_scr[...]
        m_new = jnp.maximum(m_old, jnp.max(s, axis=0, keepdims=True))
        alpha = jnp.exp(m_old - m_new)
        pr = jnp.exp(s - m_new).astype(BF16)
        m_scr[...] = m_new
        acc_scr[...] = alpha * acc_scr[...] + jnp.dot(v_ext, pr, preferred_element_type=F32)

    def reset():
        m_scr[...] = jnp.full((1, M), NEG_INF, F32)
        acc_scr[...] = jnp.zeros((v_rows, M), F32)

    def result():
        acc = acc_scr[...]
        return acc[0:DH, :] * (1.0 / acc[DH:DH + 1, :])

    reset()
    n_tiles = qi + 1

    def slc_scores(kt, dst):
        scores(ks_scr, jnp.minimum(kt, qi), dst)

    def slc_pair(j, carry):
        slc_scores(2 * j + 1, sb_scr)
        absorb(vs_scr[2 * j], sa_scr, bias_scr[2 * j])
        slc_scores(2 * j + 2, sa_scr)
        absorb(vs_scr[2 * j + 1], sb_scr, bias_scr[2 * j + 1])
        return carry

    lax.fori_loop(0, n_tiles // 2, slc_pair, 0)

    @pl.when(n_tiles % 2 == 1)
    def _():
        absorb(vs_scr[qi], sa_scr, bias_scr[qi])

    o_slc = result()

    reset()
    n_back = WINDOW // tk
    diag_bias = jnp.where(kloc <= tloc, 0.0, NEG_INF)
    far_bias = jnp.where(kloc > tloc, 0.0, NEG_INF)

    def win_bias(d):
        return diag_bias if d == 0 else (far_bias if d == n_back else None)

    def window(n_win):
        bufs = (sa_scr, sb_scr)
        scores(kw_scr, qi, bufs[0])
        for d in range(n_win):
            if d + 1 < n_win:
                scores(kw_scr, qi - (d + 1), bufs[(d + 1) % 2])
            absorb(vw_scr[qi - d], bufs[d % 2], win_bias(d))

    for n_win in range(1, n_back + 2):
        @pl.when((qi == n_win - 1) if n_win <= n_back else (qi >= n_back))
        def _():
            window(n_win)

    o_win = result()

    gt_scr[...] = gate_ref[...].astype(F32).T

    def gate(branch):
        rows = [gt_scr[pl.ds(g_idx * (HP * N_BRANCH) + h * N_BRANCH + branch, 1), :] for h in range(HP)]
        return _sigmoid(jnp.concatenate(rows, axis=1))

    o = gate(0) * o_cmp + gate(1) * o_slc + gate(2) * o_win
    o_ref[...] = jnp.concatenate([o[:, h * tq:(h + 1) * tq].T for h in range(HP)], axis=1).astype(o_ref.dtype)


def _nsa(z, kc, vct, ovt, *, B, S, tq=256):
    G, HP, DH = NSA_KV_GROUPS, NSA_HPG, NSA_HEAD_DIM
    T = B * S
    nq = S // tq
    slots = S // CMP_STRIDE
    n_sel = S // SEL_BLOCK
    n_kt = S // tq
    v_rows = DH + BF16_SUBLANES
    KVW = NSA_KV_WIDTH

    def kv_spec(which):
        return pl.BlockSpec((S, KVW), lambda b, g, i: (b, OFF_KV // KVW + which))

    return pl.pallas_call(
        functools.partial(_nsa_kernel, tq=tq, S=S),
        grid=(B, G, nq),
        in_specs=[
            pl.BlockSpec((tq, HP * DH), lambda b, g, i: (b * nq + i, OFF_Q // (HP * DH) + g)),
            pl.BlockSpec((slots, DH), lambda b, g, i: (b * G + g, 0)),
            pl.BlockSpec((DH, slots), lambda b, g, i: (0, b * G + g)),
            kv_spec(0), kv_spec(1), kv_spec(2), kv_spec(3),
            pl.BlockSpec((tq, LANES), lambda b, g, i: (b * nq + i, OFF_NG // LANES)),
            pl.BlockSpec((n_sel, slots), lambda b, g, i: (0, 0)),
        ],
        out_specs=pl.BlockSpec((tq, HP * DH), lambda b, g, i: (b * nq + i, g)),
        out_shape=jax.ShapeDtypeStruct((T, NSA_Q_WIDTH), BF16),
        scratch_shapes=[
            pltpu.VMEM((S, DH), BF16),
            pltpu.VMEM((S, DH), BF16),
            pltpu.VMEM((n_kt, v_rows, tq), BF16),
            pltpu.VMEM((n_kt, v_rows, tq), BF16),
            pltpu.VMEM((LANES, tq), F32),
            pltpu.VMEM((n_kt, tq, tq), F32),
            pltpu.VMEM((tq, HP * tq), F32),
            pltpu.VMEM((tq, HP * tq), F32),
            pltpu.VMEM((1, HP * tq), F32),
            pltpu.VMEM((v_rows, HP * tq), F32),
        ],
        compiler_params=_cparams(("parallel", "parallel", "arbitrary")),
        name="nsa_attention",
    )(z, kc, vct, z, z, z, z, z, ovt)


def _merge_kernel(ya_ref, yb_ref, yc_ref, mga_ref, mgb_ref, mgc_ref, x_ref, wa_ref, wb_ref, wc_ref, wo_ref,
                  gpost_ref, gpre_ref, x1_ref, hf_ref):
    gate = lambda ref: _sigmoid(ref[...].astype(F32))
    merged = gate(mga_ref) * jnp.dot(ya_ref[...], wa_ref[...], preferred_element_type=F32)
    merged = merged + gate(mgb_ref) * jnp.dot(yb_ref[...], wb_ref[...], preferred_element_type=F32)
    merged = merged + gate(mgc_ref) * jnp.dot(yc_ref[...], wc_ref[...], preferred_element_type=F32)
    y = jnp.dot(merged.astype(BF16), wo_ref[...], preferred_element_type=F32)
    x1 = x_ref[...] + _rms(y, gpost_ref[...])
    x1_ref[...] = x1
    hf_ref[...] = _rms(x1, gpre_ref[...]).astype(BF16)


def _merge(ya, yb, yc, z, x2d, wa, wb, wc, wo, gpost, gpre, *, tm=512):
    T, D = x2d.shape
    row = lambda c: pl.BlockSpec((tm, D), lambda i: (i, c))
    wfull = pl.BlockSpec((D, D), lambda i: (0, 0))
    vec = pl.BlockSpec((1, D), lambda i: (0, 0))
    mg0 = OFF_MG // D
    return pl.pallas_call(
        _merge_kernel,
        grid=(T // tm,),
        in_specs=[row(0), row(0), row(0), row(mg0), row(mg0 + 1), row(mg0 + 2), row(0),
                  wfull, wfull, wfull, wfull, vec, vec],
        out_specs=[row(0), row(0)],
        out_shape=[jax.ShapeDtypeStruct((T, D), F32), jax.ShapeDtypeStruct((T, D), BF16)],
        compiler_params=_cparams(("parallel",)),
        name="merge",
    )(ya, yb, yc, z, z, z, x2d, wa, wb, wc, wo, gpost, gpre)


def _ffn_kernel(hf_ref, x1_ref, win_ref, wout_ref, gpost_ref, o_ref, acc_scr, *, fc):
    hf = hf_ref[...]
    n_chunks = D_FF // fc
    for c in range(n_chunks):
        gate = jnp.dot(hf, win_ref[:, c * fc:(c + 1) * fc], preferred_element_type=F32)
        up = jnp.dot(hf, win_ref[:, D_FF + c * fc:D_FF + (c + 1) * fc], preferred_element_type=F32)
        act = (gate * _sigmoid(gate) * up).astype(BF16)
        part = jnp.dot(act, wout_ref[c * fc:(c + 1) * fc, :], preferred_element_type=F32)
        if c == 0:
            acc_scr[...] = part
        else:
            acc_scr[...] += part
    o_ref[...] = x1_ref[...] + _rms(acc_scr[...], gpost_ref[...])


def _ffn(hf, x1, win, wout, gpost, *, tm=512, fc=1408):
    T, D = x1.shape
    row = pl.BlockSpec((tm, D), lambda i: (i, 0))
    return pl.pallas_call(
        functools.partial(_ffn_kernel, fc=fc),
        grid=(T // tm,),
        in_specs=[row, row,
                  pl.BlockSpec(win.shape, lambda i: (0, 0), pipeline_mode=pl.Buffered(1)),
                  pl.BlockSpec(wout.shape, lambda i: (0, 0), pipeline_mode=pl.Buffered(1)),
                  pl.BlockSpec((1, D), lambda i: (0, 0))],
        out_specs=row,
        out_shape=jax.ShapeDtypeStruct((T, D), F32),
        scratch_shapes=[pltpu.VMEM((tm, D), F32)],
        compiler_params=_cparams(("parallel",)),
        name="ffn",
    )(hf, x1, win, wout, gpost)


def _overlap_t(S):
    slots = S // CMP_STRIDE
    n_sel = S // SEL_BLOCK
    c_start = np.arange(slots) * CMP_STRIDE
    s_start = np.arange(n_sel) * SEL_BLOCK
    overlap = np.clip(np.minimum(c_start[:, None] + CMP_BLOCK, s_start[None, :] + SEL_BLOCK)
                      - np.maximum(c_start[:, None], s_start[None, :]), 0, None) / CMP_BLOCK
    return jnp.asarray(overlap.T, F32)


def _reorder_w_in(w_in):
    s = np.cumsum([0, GMLP_WIDTH, GMLP_WIDTH, NSA_Q_WIDTH, 6 * NSA_KV_WIDTH, N_BRANCH * NSA_HEADS,
                   RNN_WIDTH, RNN_WIDTH, N_BRANCH * D_MODEL])
    u, v, q, kv, ng, xr, rg, mg = (w_in[:, s[i]:s[i + 1]] for i in range(8))
    kv_cmp, kv_rest = kv[:, :2 * NSA_KV_WIDTH], kv[:, 2 * NSA_KV_WIDTH:]
    pad = jnp.zeros((w_in.shape[0], NG_PAD - N_BRANCH * NSA_HEADS), w_in.dtype)
    return jnp.concatenate([u, v, xr, rg, mg, kv_cmp, ng, pad, q, kv_rest], axis=1).astype(BF16)


def _block_diag_gates(wa, wx):
    per = RNN_BLOCK // RNN_HEAD_DIM
    nblk = RNN_HEADS // per
    eye = jnp.eye(per, dtype=wa.dtype)

    def bd(w):
        w = w.reshape(nblk, per, RNN_HEAD_DIM, RNN_HEAD_DIM)
        return jnp.einsum('kpio,pq->kpiqo', w, eye).reshape(nblk, RNN_BLOCK, RNN_BLOCK)

    return jnp.concatenate([bd(wa), bd(wx)], axis=-1).astype(BF16)


def _layer(x2d, B, S, ovt, g_pre_mix, g_post_mix, g_pre_ffn, g_post_ffn, w_in,
           gmlp_ln_g, gmlp_ln_b, gmlp_ws, gmlp_bs,
           nsa_pe_k, nsa_pe_v, nsa_wk1, nsa_wk2, nsa_wv1, nsa_wv2,
           rnn_conv_w, rnn_conv_b, rnn_wa, rnn_ba, rnn_wx, rnn_bx, rnn_lam,
           w_br_a, w_br_b, w_br_c, w_o, w_ffn_in, w_ffn_out):
    G, DH = NSA_KV_GROUPS, NSA_HEAD_DIM
    row = lambda a: a.reshape(1, -1)
    z = _in_proj(x2d, row(g_pre_mix), _reorder_w_in(w_in))

    y_a = _gmlp(z, row(gmlp_ln_g), row(gmlp_ln_b), gmlp_ws, gmlp_bs.T)
    y_c = _rglru(z, rnn_conv_w, row(rnn_conv_b), _block_diag_gates(rnn_wa, rnn_wx),
                 row(rnn_ba), row(rnn_bx), row(rnn_lam), B=B, S=S)

    def halves(cols):
        t = cols.reshape(B, S // CMP_STRIDE, CMP_STRIDE, G, DH).transpose(0, 3, 1, 2, 4)
        return t.reshape(B * G * (S // CMP_STRIDE), CMP_STRIDE * DH)

    kc, vct = _compress(halves(z[:, OFF_KC:OFF_KC + NSA_KV_WIDTH]),
                        halves(z[:, OFF_KC + NSA_KV_WIDTH:OFF_KC + 2 * NSA_KV_WIDTH]),
                        nsa_pe_k.reshape(1, -1), nsa_pe_v.reshape(1, -1),
                        nsa_wk1.astype(BF16), nsa_wk2.astype(BF16), nsa_wv1.astype(BF16), nsa_wv2.T.astype(BF16))
    y_b = _nsa(z, kc, vct, ovt, B=B, S=S)

    x1, hf = _merge(y_a, y_b, y_c, z, x2d, w_br_a.astype(BF16), w_br_b.astype(BF16), w_br_c.astype(BF16),
                    w_o.astype(BF16), row(g_post_mix), row(g_pre_ffn))
    return _ffn(hf, x1, w_ffn_in.astype(BF16), w_ffn_out.astype(BF16), row(g_post_ffn))


def kernel(x, g_pre_mix, g_post_mix, g_pre_ffn, g_post_ffn, w_in, gmlp_ln_g, gmlp_ln_b, gmlp_ws, gmlp_bs, nsa_pe_k, nsa_pe_v, nsa_wk1, nsa_wk2, nsa_wv1, nsa_wv2, rnn_conv_w, rnn_conv_b, rnn_wa, rnn_ba, rnn_wx, rnn_bx, rnn_lam, w_br_a, w_br_b, w_br_c, w_o, w_ffn_in, w_ffn_out):
    B, S, D = x.shape
    params = (g_pre_mix, g_post_mix, g_pre_ffn, g_post_ffn, w_in, gmlp_ln_g, gmlp_ln_b, gmlp_ws, gmlp_bs,
              nsa_pe_k, nsa_pe_v, nsa_wk1, nsa_wk2, nsa_wv1, nsa_wv2,
              rnn_conv_w, rnn_conv_b, rnn_wa, rnn_ba, rnn_wx, rnn_bx, rnn_lam,
              w_br_a, w_br_b, w_br_c, w_o, w_ffn_in, w_ffn_out)
    ovt = _overlap_t(S)
    x2d = x.reshape(B * S, D)
    for l in range(w_in.shape[0]):
        x2d = _layer(x2d, B, S, ovt, *(p[l] for p in params))
    return x2d.reshape(B, S, D)
```

```python
import functools

import jax
import jax.numpy as jnp
import numpy as np
from jax import lax
from jax.experimental import pallas as pl
from jax.experimental.pallas import tpu as pltpu

F32 = jnp.float32
BF16 = jnp.bfloat16

EPS = 1e-6
NEG_INF = -1e30
FORCE_SCORE = 1e4

D_MODEL = 1024
GMLP_WIDTH = 1024
GMLP_GROUPS = 4
GMLP_GROUP_DIM = GMLP_WIDTH // GMLP_GROUPS
GMLP_CHUNK = 128

NSA_HEADS = 16
NSA_KV_GROUPS = 4
NSA_HEAD_DIM = 64
NSA_HPG = NSA_HEADS // NSA_KV_GROUPS
NSA_Q_WIDTH = NSA_HEADS * NSA_HEAD_DIM
NSA_KV_WIDTH = NSA_KV_GROUPS * NSA_HEAD_DIM
N_BRANCH = 3
CMP_BLOCK = 32
CMP_STRIDE = 16
CMP_HIDDEN = 256
SEL_BLOCK = 64
SEL_TOP_N = 16
WINDOW = 512
Q_SCALE = NSA_HEAD_DIM ** -0.5 * float(np.log2(np.e))

RNN_WIDTH = 1024
RNN_HEADS = 16
RNN_HEAD_DIM = RNN_WIDTH // RNN_HEADS
CONV_WIDTH = 4
LRU_C = 8.0
RNN_BLOCK = 256

D_FF = 2816

OFF_U = 0
OFF_V = 1024
OFF_XR = 2048
OFF_RG = 3072
OFF_MG = 4096
OFF_KC = 7168
OFF_NG = OFF_KC + 2 * NSA_KV_WIDTH
NG_PAD = 512
OFF_Q = OFF_NG + NG_PAD
OFF_KV = OFF_Q + NSA_Q_WIDTH
D_IN_PAD = OFF_KV + 4 * NSA_KV_WIDTH

LANES = 128
BF16_SUBLANES = 16
V7X_VMEM_LIMIT = 56 * 1024 * 1024


def _cparams(sem, vmem=V7X_VMEM_LIMIT):
    return pltpu.CompilerParams(dimension_semantics=sem, vmem_limit_bytes=vmem)


def _rms(x, g):
    ms = jnp.mean(x * x, axis=-1, keepdims=True)
    return x * lax.rsqrt(ms + EPS) * g


def _gelu(x):
    return jax.nn.gelu(x)


def _sigmoid(x):
    return jax.nn.sigmoid(x)


_NT = (((1,), (1,)), ((), ()))


def _in_proj_kernel(x_ref, g_ref, w_ref, o_ref, h_scr):
    @pl.when(pl.program_id(1) == 0)
    def _():
        h_scr[...] = _rms(x_ref[...], g_ref[...]).astype(BF16)

    o_ref[...] = jnp.dot(h_scr[...], w_ref[...], preferred_element_type=F32).astype(o_ref.dtype)


def _in_proj(x2d, g, w, *, tm=1024, tn=1024):
    T, D = x2d.shape
    N = w.shape[1]
    return pl.pallas_call(
        _in_proj_kernel,
        grid=(T // tm, N // tn),
        in_specs=[
            pl.BlockSpec((tm, D), lambda i, j: (i, 0)),
            pl.BlockSpec((1, D), lambda i, j: (0, 0)),
            pl.BlockSpec((D, tn), lambda i, j: (0, j)),
        ],
        out_specs=pl.BlockSpec((tm, tn), lambda i, j: (i, j)),
        out_shape=jax.ShapeDtypeStruct((T, N), BF16),
        scratch_shapes=[pltpu.VMEM((tm, D), BF16)],
        compiler_params=_cparams(("parallel", "arbitrary")),
        name="in_proj",
    )(x2d, g, w)


def _gmlp_kernel(u_ref, v_ref, lng_ref, lnb_ref, ws_ref, bst_ref, o_ref, *, n_chunks):
    C = GMLP_CHUNK
    row = lax.broadcasted_iota(jnp.int32, (C, C), 0)
    col = lax.broadcasted_iota(jnp.int32, (C, C), 1)
    causal = col <= row
    ws = [jnp.where(causal, ws_ref[g], 0.0).astype(BF16) for g in range(GMLP_GROUPS)]
    bst = bst_ref[...]
    for c in range(n_chunks):
        rows = slice(c * C, (c + 1) * C)
        gv = _gelu(v_ref[rows, :].astype(F32))
        mu = jnp.mean(gv, axis=-1, keepdims=True)
        d = gv - mu
        var = jnp.mean(d * d, axis=-1, keepdims=True)
        vn = (d * lax.rsqrt(var + EPS) * lng_ref[...] + lnb_ref[...]).astype(BF16)
        for g in range(GMLP_GROUPS):
            cols = slice(g * GMLP_GROUP_DIM, (g + 1) * GMLP_GROUP_DIM)
            mixed = jnp.dot(ws[g], vn[:, cols], preferred_element_type=F32) + bst[:, g:g + 1]
            o_ref[rows, cols] = (_gelu(u_ref[rows, cols].astype(F32)) * mixed).astype(o_ref.dtype)


def _gmlp(z, lng, lnb, ws, bst, *, n_chunks=4):
    T = z.shape[0]
    tm = GMLP_CHUNK * n_chunks
    W = GMLP_WIDTH
    return pl.pallas_call(
        functools.partial(_gmlp_kernel, n_chunks=n_chunks),
        grid=(T // tm,),
        in_specs=[
            pl.BlockSpec((tm, W), lambda i: (i, OFF_U // W)),
            pl.BlockSpec((tm, W), lambda i: (i, OFF_V // W)),
            pl.BlockSpec((1, W), lambda i: (0, 0)),
            pl.BlockSpec((1, W), lambda i: (0, 0)),
            pl.BlockSpec((GMLP_GROUPS, GMLP_CHUNK, GMLP_CHUNK), lambda i: (0, 0, 0)),
            pl.BlockSpec((GMLP_CHUNK, GMLP_GROUPS), lambda i: (0, 0)),
        ],
        out_specs=pl.BlockSpec((tm, W), lambda i: (i, 0)),
        out_shape=jax.ShapeDtypeStruct((T, W), BF16),
        compiler_params=_cparams(("parallel",)),
        name="gmlp",
    )(z, z, lng, lnb, ws, bst)


def _rglru_kernel(xr_ref, rg_ref, cw_ref, cb_ref, wab_ref, ba_ref, bx_ref, lam_ref, o_ref,
                  ext_scr, a_scr, b_scr, hc_scr, *, tt):
    W = RNN_WIDTH
    groups = tt // 8
    sub_row = lax.broadcasted_iota(jnp.int32, (groups, 8, RNN_BLOCK), 1)

    @pl.when(pl.program_id(1) == 0)
    def _():
        ext_scr[0:8, :] = jnp.zeros((8, W), F32)
        hc_scr[...] = jnp.zeros((8, W), F32)

    xr = xr_ref[...].astype(F32)
    ext_scr[8:8 + tt, :] = xr
    cw = cw_ref[...]
    xc = (cw[3:4] * xr + cw[2:3] * ext_scr[7:7 + tt, :] + cw[1:2] * ext_scr[6:6 + tt, :]
          + cw[0:1] * ext_scr[5:5 + tt, :] + cb_ref[...])
    ext_scr[0:8, :] = xr[tt - 8:tt, :]

    lam = lam_ref[...]
    neg = -lam
    softplus = jnp.maximum(neg, 0.0) + jnp.log1p(jnp.exp(-jnp.abs(neg)))
    for k in range(W // RNN_BLOCK):
        cols = slice(k * RNN_BLOCK, (k + 1) * RNN_BLOCK)
        xck = xc[:, cols]
        gates = jnp.dot(xck.astype(BF16), wab_ref[k], preferred_element_type=F32)
        r = _sigmoid(gates[:, :RNN_BLOCK] + ba_ref[:, cols])
        i = _sigmoid(gates[:, RNN_BLOCK:] + bx_ref[:, cols])
        log_a = -LRU_C * r * softplus[:, cols]
        a = jnp.exp(log_a)
        one_minus_a2 = -jnp.tanh(log_a) * (1.0 + a * a)
        b_in = jnp.sqrt(one_minus_a2) * (i * xck)
        a3 = a.reshape(groups, 8, RNN_BLOCK)
        b3 = b_in.reshape(groups, 8, RNN_BLOCK)
        for step in (1, 2, 4):
            keep = sub_row >= step
            a_prev = jnp.where(keep, pltpu.roll(a3, step, axis=1), 1.0)
            b_prev = jnp.where(keep, pltpu.roll(b3, step, axis=1), 0.0)
            b3 = a3 * b_prev + b3
            a3 = a3 * a_prev
        a_scr[:, cols] = a3.reshape(tt, RNN_BLOCK)
        b_scr[:, cols] = b3.reshape(tt, RNN_BLOCK)

    def carry_step(i, h_prev):
        base = pl.multiple_of(i * BF16_SUBLANES, BF16_SUBLANES)
        gate = _gelu(rg_ref[pl.ds(base, BF16_SUBLANES), :].astype(F32))
        ys = []
        for half in range(BF16_SUBLANES // 8):
            rows = pl.ds(base + half * 8, 8)
            h = b_scr[rows, :] + a_scr[rows, :] * h_prev
            ys.append(h * gate[half * 8:(half + 1) * 8, :])
            h_prev = h[7:8, :]
        o_ref[pl.ds(base, BF16_SUBLANES), :] = jnp.concatenate(ys, axis=0).astype(o_ref.dtype)
        return h_prev

    h_last = lax.fori_loop(0, tt // BF16_SUBLANES, carry_step, hc_scr[0:1, :])
    hc_scr[...] = jnp.broadcast_to(h_last, (8, W))


def _rglru(z, cw, cb, wab, ba, bx, lam, *, B, S, tt=512):
    T = z.shape[0]
    W = RNN_WIDTH
    nt = S // tt
    return pl.pallas_call(
        functools.partial(_rglru_kernel, tt=tt),
        grid=(B, nt),
        in_specs=[
            pl.BlockSpec((tt, W), lambda b, t: (b * nt + t, OFF_XR // W)),
            pl.BlockSpec((tt, W), lambda b, t: (b * nt + t, OFF_RG // W)),
            pl.BlockSpec((CONV_WIDTH, W), lambda b, t: (0, 0)),
            pl.BlockSpec((1, W), lambda b, t: (0, 0)),
            pl.BlockSpec((W // RNN_BLOCK, RNN_BLOCK, 2 * RNN_BLOCK), lambda b, t: (0, 0, 0)),
            pl.BlockSpec((1, W), lambda b, t: (0, 0)),
            pl.BlockSpec((1, W), lambda b, t: (0, 0)),
            pl.BlockSpec((1, W), lambda b, t: (0, 0)),
        ],
        out_specs=pl.BlockSpec((tt, W), lambda b, t: (b * nt + t, 0)),
        out_shape=jax.ShapeDtypeStruct((T, W), BF16),
        scratch_shapes=[
            pltpu.VMEM((tt + 8, W), F32),
            pltpu.VMEM((tt, W), F32),
            pltpu.VMEM((tt, W), F32),
            pltpu.VMEM((8, W), F32),
        ],
        compiler_params=_cparams(("parallel", "arbitrary")),
        name="rglru",
    )(z, z, cw, cb, wab, ba, bx, lam)


def _compress_hidden(x_refs, pe_ref, w1_ref):
    DH = NSA_HEAD_DIM
    half = CMP_STRIDE * DH
    hm = jnp.concatenate(
        [jnp.concatenate([x[:, g * DH:(g + 1) * DH] for x in x_refs], axis=1) for g in range(NSA_KV_GROUPS)], axis=0)
    rows = hm.shape[0]
    lo = jnp.dot(hm, w1_ref[0:half, :], preferred_element_type=F32)
    hi = jnp.dot(hm, w1_ref[half:2 * half, :], preferred_element_type=F32)
    pe = jnp.broadcast_to(pe_ref[...], (8, 2 * half)).astype(BF16)
    pe_term = jnp.dot(pe, w1_ref[...], preferred_element_type=F32)[0:1, :]
    return _gelu(lo + pltpu.roll(hi, rows - 1, axis=0) + pe_term).astype(BF16)


def _compress_kernel(*refs):
    xk, xv = refs[:CMP_STRIDE], refs[CMP_STRIDE:2 * CMP_STRIDE]
    pek_ref, pev_ref, wk1_ref, wk2_ref, wv1_ref, wv2t_ref, kc_ref, vct_ref = refs[2 * CMP_STRIDE:]
    kc_ref[...] = jnp.dot(_compress_hidden(xk, pek_ref, wk1_ref), wk2_ref[...], preferred_element_type=F32)
    vct_ref[...] = lax.dot_general(wv2t_ref[...], _compress_hidden(xv, pev_ref, wv1_ref), _NT,
                                   preferred_element_type=F32)


def _compress(z, pek, pev, wk1, wk2, wv1, wv2t, *, B, S):
    G, DH = NSA_KV_GROUPS, NSA_HEAD_DIM
    slots = S // CMP_STRIDE
    KVW = NSA_KV_WIDTH
    z16 = z.reshape(z.shape[0] // CMP_STRIDE, CMP_STRIDE * D_IN_PAD)

    def phase_spec(which, l):
        col = l * (D_IN_PAD // KVW) + OFF_KC // KVW + which
        return pl.BlockSpec((slots, KVW), lambda b: (b, col))

    full = lambda shape: pl.BlockSpec(shape, lambda b: (0,) * len(shape))
    phases = [phase_spec(which, l) for which in range(2) for l in range(CMP_STRIDE)]
    return pl.pallas_call(
        _compress_kernel,
        grid=(B,),
        in_specs=phases + [full(pek.shape), full(pev.shape), full(wk1.shape), full(wk2.shape), full(wv1.shape),
                           full(wv2t.shape)],
        out_specs=[pl.BlockSpec((G * slots, DH), lambda b: (b, 0)), pl.BlockSpec((DH, G * slots), lambda b: (0, b))],
        out_shape=[jax.ShapeDtypeStruct((B * G * slots, DH), F32), jax.ShapeDtypeStruct((DH, B * G * slots), F32)],
        compiler_params=_cparams(("parallel",)),
        name="nsa_compress",
    )(*([z16] * (2 * CMP_STRIDE)), pek, pev, wk1, wk2, wv1, wv2t)


def _nsa_kernel(q_ref, kc_ref, vct_ref, ks_ref, vs_ref, kw_ref, vw_ref, gate_ref, ovt_ref, o_ref,
                ks_scr, kw_scr, vs_scr, vw_scr, gt_scr, bias_scr, sa_scr, sb_scr, m_scr, acc_scr, *, tq, S):
    HP, DH = NSA_HPG, NSA_HEAD_DIM
    M = HP * tq
    tk = tq
    n_kt = S // tk
    n_sel = S // SEL_BLOCK
    slots = kc_ref.shape[0]
    v_rows = DH + BF16_SUBLANES
    g_idx = pl.program_id(1)
    qi = pl.program_id(2)
    t0 = qi * tq

    for g in range(NSA_KV_GROUPS):
        @pl.when((qi == 0) & (g_idx == g))
        def _():
            lanes = slice(g * DH, (g + 1) * DH)
            ks_scr[...] = ks_ref[:, lanes]
            kw_scr[...] = kw_ref[:, lanes]
            ones = jnp.ones((BF16_SUBLANES, tk), BF16)
            for v_ref, v_scr in ((vs_ref, vs_scr), (vw_ref, vw_scr)):
                vt = v_ref[:, lanes].T
                for kt in range(n_kt):
                    v_scr[kt, 0:DH, :] = vt[:, kt * tk:(kt + 1) * tk]
                    v_scr[kt, DH:v_rows, :] = ones

    def per_head(x):
        return jnp.concatenate([x] * HP, axis=1)

    q_t = q_ref[...].T
    qt = jnp.concatenate([q_t[h * DH:(h + 1) * DH, :] for h in range(HP)], axis=1)

    def scores(k_scr, kt, dst):
        start = pl.multiple_of(kt * tk, tk)
        dst[...] = jnp.dot(k_scr[pl.ds(start, tk), :], qt, preferred_element_type=F32)

    scores(ks_scr, 0, sa_scr)

    sc = jnp.dot(kc_ref[...].astype(BF16), qt, preferred_element_type=F32)
    n_idx = lax.broadcasted_iota(jnp.int32, (slots, tq), 0)
    t_idx = t0 + lax.broadcasted_iota(jnp.int32, (slots, tq), 1)
    valid = per_head(jnp.where(n_idx * CMP_STRIDE + (CMP_BLOCK - 1) <= t_idx, 1.0, 0.0)) > 0.5
    sc = jnp.where(valid, sc, NEG_INF)
    mx = jnp.max(sc, axis=0, keepdims=True)
    p = jnp.where(valid, jnp.exp2(sc - mx), 0.0)
    den = jnp.sum(p, axis=0, keepdims=True)
    p_c = p * (1.0 / jnp.where(den > 0.0, den, 1.0))
    o_cmp = jnp.dot(vct_ref[...].astype(BF16), p_c.astype(BF16), preferred_element_type=F32)

    p_sum = p_c[:, 0:tq]
    for h in range(1, HP):
        p_sum = p_sum + p_c[:, h * tq:(h + 1) * tq]
    imp = jnp.dot(ovt_ref[...], p_sum, preferred_element_type=F32, precision=lax.Precision.HIGHEST)
    j_idx = lax.broadcasted_iota(jnp.int32, (n_sel, tq), 0)
    cur = (t0 + lax.broadcasted_iota(jnp.int32, (n_sel, tq), 1)) // SEL_BLOCK
    forced = (j_idx == 0) | (j_idx == cur) | (j_idx == cur - 1)
    imp = jnp.where(forced, FORCE_SCORE, jnp.where(j_idx > cur, NEG_INF, imp))
    rank = jnp.zeros((n_sel, tq), F32)
    for i in range(n_sel):
        ci = imp[i:i + 1, :]
        ge = jnp.where(ci >= imp, 1.0, 0.0)
        gt = jnp.where(ci > imp, 1.0, 0.0)
        rank = rank + jnp.where(j_idx > i, ge, gt)
    sel = jnp.where(rank < float(min(SEL_TOP_N, n_sel)), 1.0, 0.0)

    per_tile = tk // SEL_BLOCK
    kloc = lax.broadcasted_iota(jnp.int32, (tk, tq), 0)
    tloc = lax.broadcasted_iota(jnp.int32, (tk, tq), 1)
    for kt in range(n_kt):
        @pl.when(kt <= qi)
        def _():
            hit = jnp.concatenate(
                [jnp.broadcast_to(sel[kt * per_tile + jj:kt * per_tile + jj + 1, :], (SEL_BLOCK, tq))
                 for jj in range(per_tile)], axis=0)
            bias_scr[kt] = jnp.where((hit > 0.5) & (kt * tk + kloc <= t0 + tloc), 0.0, NEG_INF)

    def absorb(v_ext, src, bias):
        s = src[...] if bias is None else src[...] + per_head(bias)
        m_old = m_scr[...]
        m_new = jnp.maximum(m_old, jnp.max(s, axis=0, keepdims=True))
        alpha = jnp.exp2(m_old - m_new)
        pr = jnp.exp2(s - m_new).astype(BF16)
        m_scr[...] = m_new
        acc_scr[...] = alpha * acc_scr[...] + jnp.dot(v_ext, pr, preferred_element_type=F32)

    def reset():
        m_scr[...] = jnp.full((1, M), NEG_INF, F32)
        acc_scr[...] = jnp.zeros((v_rows, M), F32)

    def result():
        acc = acc_scr[...]
        return acc[0:DH, :] * (1.0 / acc[DH:DH + 1, :])

    reset()
    n_tiles = qi + 1

    def slc_scores(kt, dst):
        scores(ks_scr, jnp.minimum(kt, qi), dst)

    def slc_pair(j, carry):
        slc_scores(2 * j + 1, sb_scr)
        absorb(vs_scr[2 * j], sa_scr, bias_scr[2 * j])
        slc_scores(2 * j + 2, sa_scr)
        absorb(vs_scr[2 * j + 1], sb_scr, bias_scr[2 * j + 1])
        return carry

    lax.fori_loop(0, n_tiles // 2, slc_pair, 0)

    @pl.when(n_tiles % 2 == 1)
    def _():
        absorb(vs_scr[qi], sa_scr, bias_scr[qi])

    o_slc = result()

    reset()
    n_back = WINDOW // tk
    diag_bias = jnp.where(kloc <= tloc, 0.0, NEG_INF)
    far_bias = jnp.where(kloc > tloc, 0.0, NEG_INF)

    def win_bias(d):
        return diag_bias if d == 0 else (far_bias if d == n_back else None)

    def window(n_win):
        bufs = (sa_scr, sb_scr)
        scores(kw_scr, qi, bufs[0])
        for d in range(n_win):
            if d + 1 < n_win:
                scores(kw_scr, qi - (d + 1), bufs[(d + 1) % 2])
            absorb(vw_scr[qi - d], bufs[d % 2], win_bias(d))

    for n_win in range(1, n_back + 2):
        @pl.when((qi == n_win - 1) if n_win <= n_back else (qi >= n_back))
        def _():
            window(n_win)

    o_win = result()

    gt_scr[...] = gate_ref[...].astype(F32).T

    def gate(branch):
        rows = [gt_scr[pl.ds(g_idx * (HP * N_BRANCH) + h * N_BRANCH + branch, 1), :] for h in range(HP)]
        return _sigmoid(jnp.concatenate(rows, axis=1))

    o = gate(0) * o_cmp + gate(1) * o_slc + gate(2) * o_win
    o_ref[...] = jnp.concatenate([o[:, h * tq:(h + 1) * tq].T for h in range(HP)], axis=1).astype(o_ref.dtype)


def _nsa(z, kc, vct, ovt, *, B, S, tq=256):
    G, HP, DH = NSA_KV_GROUPS, NSA_HPG, NSA_HEAD_DIM
    T = B * S
    nq = S // tq
    slots = S // CMP_STRIDE
    n_sel = S // SEL_BLOCK
    n_kt = S // tq
    v_rows = DH + BF16_SUBLANES
    KVW = NSA_KV_WIDTH

    def kv_spec(which):
        return pl.BlockSpec((S, KVW), lambda b, g, i: (b, OFF_KV // KVW + which))

    return pl.pallas_call(
        functools.partial(_nsa_kernel, tq=tq, S=S),
        grid=(B, G, nq),
        in_specs=[
            pl.BlockSpec((tq, HP * DH), lambda b, g, i: (b * nq + i, OFF_Q // (HP * DH) + g)),
            pl.BlockSpec((slots, DH), lambda b, g, i: (b * G + g, 0)),
            pl.BlockSpec((DH, slots), lambda b, g, i: (0, b * G + g)),
            kv_spec(0), kv_spec(1), kv_spec(2), kv_spec(3),
            pl.BlockSpec((tq, LANES), lambda b, g, i: (b * nq + i, OFF_NG // LANES)),
            pl.BlockSpec((n_sel, slots), lambda b, g, i: (0, 0)),
        ],
        out_specs=pl.BlockSpec((tq, HP * DH), lambda b, g, i: (b * nq + i, g)),
        out_shape=jax.ShapeDtypeStruct((T, NSA_Q_WIDTH), BF16),
        scratch_shapes=[
            pltpu.VMEM((S, DH), BF16),
            pltpu.VMEM((S, DH), BF16),
            pltpu.VMEM((n_kt, v_rows, tq), BF16),
            pltpu.VMEM((n_kt, v_rows, tq), BF16),
            pltpu.VMEM((LANES, tq), F32),
            pltpu.VMEM((n_kt, tq, tq), F32),
            pltpu.VMEM((tq, HP * tq), F32),
            pltpu.VMEM((tq, HP * tq), F32),
            pltpu.VMEM((1, HP * tq), F32),
            pltpu.VMEM((v_rows, HP * tq), F32),
        ],
        compiler_params=_cparams(("parallel", "parallel", "arbitrary")),
        name="nsa_attention",
    )(z, kc, vct, z, z, z, z, z, ovt)


def _merge_kernel(ya_ref, yb_ref, yc_ref, mga_ref, mgb_ref, mgc_ref, x_ref, wa_ref, wb_ref, wc_ref, wo_ref,
                  gpost_ref, gpre_ref, x1_ref, hf_ref):
    gate = lambda ref: _sigmoid(ref[...].astype(F32))
    merged = gate(mga_ref) * jnp.dot(ya_ref[...], wa_ref[...], preferred_element_type=F32)
    merged = merged + gate(mgb_ref) * jnp.dot(yb_ref[...], wb_ref[...], preferred_element_type=F32)
    merged = merged + gate(mgc_ref) * jnp.dot(yc_ref[...], wc_ref[...], preferred_element_type=F32)
    y = jnp.dot(merged.astype(BF16), wo_ref[...], preferred_element_type=F32)
    x1 = x_ref[...] + _rms(y, gpost_ref[...])
    x1_ref[...] = x1
    hf_ref[...] = _rms(x1, gpre_ref[...]).astype(BF16)


def _merge(ya, yb, yc, z, x2d, wa, wb, wc, wo, gpost, gpre, *, tm=512):
    T, D = x2d.shape
    row = lambda c: pl.BlockSpec((tm, D), lambda i: (i, c))
    wfull = pl.BlockSpec((D, D), lambda i: (0, 0))
    vec = pl.BlockSpec((1, D), lambda i: (0, 0))
    mg0 = OFF_MG // D
    return pl.pallas_call(
        _merge_kernel,
        grid=(T // tm,),
        in_specs=[row(0), row(0), row(0), row(mg0), row(mg0 + 1), row(mg0 + 2), row(0),
                  wfull, wfull, wfull, wfull, vec, vec],
        out_specs=[row(0), row(0)],
        out_shape=[jax.ShapeDtypeStruct((T, D), F32), jax.ShapeDtypeStruct((T, D), BF16)],
        compiler_params=_cparams(("parallel",)),
        name="merge",
    )(ya, yb, yc, z, z, z, x2d, wa, wb, wc, wo, gpost, gpre)


def _ffn_kernel(hf_ref, x1_ref, win_ref, wout_ref, gpost_ref, o_ref, acc_scr, *, fc):
    hf = hf_ref[...]
    n_chunks = D_FF // fc
    for c in range(n_chunks):
        gate = jnp.dot(hf, win_ref[:, c * fc:(c + 1) * fc], preferred_element_type=F32)
        up = jnp.dot(hf, win_ref[:, D_FF + c * fc:D_FF + (c + 1) * fc], preferred_element_type=F32)
        act = (gate * _sigmoid(gate) * up).astype(BF16)
        part = jnp.dot(act, wout_ref[c * fc:(c + 1) * fc, :], preferred_element_type=F32)
        if c == 0:
            acc_scr[...] = part
        else:
            acc_scr[...] += part
    o_ref[...] = x1_ref[...] + _rms(acc_scr[...], gpost_ref[...])


def _ffn(hf, x1, win, wout, gpost, *, tm=512, fc=1408):
    T, D = x1.shape
    row = pl.BlockSpec((tm, D), lambda i: (i, 0))
    return pl.pallas_call(
        functools.partial(_ffn_kernel, fc=fc),
        grid=(T // tm,),
        in_specs=[row, row,
                  pl.BlockSpec(win.shape, lambda i: (0, 0), pipeline_mode=pl.Buffered(1)),
                  pl.BlockSpec(wout.shape, lambda i: (0, 0), pipeline_mode=pl.Buffered(1)),
                  pl.BlockSpec((1, D), lambda i: (0, 0))],
        out_specs=row,
        out_shape=jax.ShapeDtypeStruct((T, D), F32),
        scratch_shapes=[pltpu.VMEM((tm, D), F32)],
        compiler_params=_cparams(("parallel",)),
        name="ffn",
    )(hf, x1, win, wout, gpost)


def _overlap_t(S):
    slots = S // CMP_STRIDE
    n_sel = S // SEL_BLOCK
    c_start = np.arange(slots) * CMP_STRIDE
    s_start = np.arange(n_sel) * SEL_BLOCK
    overlap = np.clip(np.minimum(c_start[:, None] + CMP_BLOCK, s_start[None, :] + SEL_BLOCK)
                      - np.maximum(c_start[:, None], s_start[None, :]), 0, None) / CMP_BLOCK
    return jnp.asarray(overlap.T, F32)


def _reorder_w_in(w_in):
    s = np.cumsum([0, GMLP_WIDTH, GMLP_WIDTH, NSA_Q_WIDTH, 6 * NSA_KV_WIDTH, N_BRANCH * NSA_HEADS,
                   RNN_WIDTH, RNN_WIDTH, N_BRANCH * D_MODEL])
    u, v, q, kv, ng, xr, rg, mg = (w_in[:, s[i]:s[i + 1]] for i in range(8))
    kv_cmp, kv_rest = kv[:, :2 * NSA_KV_WIDTH], kv[:, 2 * NSA_KV_WIDTH:]
    q = q * Q_SCALE
    pad = jnp.zeros((w_in.shape[0], NG_PAD - N_BRANCH * NSA_HEADS), w_in.dtype)
    return jnp.concatenate([u, v, xr, rg, mg, kv_cmp, ng, pad, q, kv_rest], axis=1).astype(BF16)


def _block_diag_gates(wa, wx):
    per = RNN_BLOCK // RNN_HEAD_DIM
    nblk = RNN_HEADS // per
    eye = jnp.eye(per, dtype=wa.dtype)

    def bd(w):
        w = w.reshape(nblk, per, RNN_HEAD_DIM, RNN_HEAD_DIM)
        return jnp.einsum('kpio,pq->kpiqo', w, eye).reshape(nblk, RNN_BLOCK, RNN_BLOCK)

    return jnp.concatenate([bd(wa), bd(wx)], axis=-1).astype(BF16)


def _layer(x2d, B, S, ovt, g_pre_mix, g_post_mix, g_pre_ffn, g_post_ffn, w_in,
           gmlp_ln_g, gmlp_ln_b, gmlp_ws, gmlp_bs,
           nsa_pe_k, nsa_pe_v, nsa_wk1, nsa_wk2, nsa_wv1, nsa_wv2,
           rnn_conv_w, rnn_conv_b, rnn_wa, rnn_ba, rnn_wx, rnn_bx, rnn_lam,
           w_br_a, w_br_b, w_br_c, w_o, w_ffn_in, w_ffn_out):
    row = lambda a: a.reshape(1, -1)
    z = _in_proj(x2d, row(g_pre_mix), _reorder_w_in(w_in))

    y_a = _gmlp(z, row(gmlp_ln_g), row(gmlp_ln_b), gmlp_ws, gmlp_bs.T)
    y_c = _rglru(z, rnn_conv_w, row(rnn_conv_b), _block_diag_gates(rnn_wa, rnn_wx),
                 row(rnn_ba), row(rnn_bx), row(rnn_lam), B=B, S=S)

    kc, vct = _compress(z, nsa_pe_k.reshape(1, -1), nsa_pe_v.reshape(1, -1), nsa_wk1.astype(BF16),
                        nsa_wk2.astype(BF16), nsa_wv1.astype(BF16), nsa_wv2.T.astype(BF16), B=B, S=S)
    y_b = _nsa(z, kc, vct, ovt, B=B, S=S)

    x1, hf = _merge(y_a, y_b, y_c, z, x2d, w_br_a.astype(BF16), w_br_b.astype(BF16), w_br_c.astype(BF16),
                    w_o.astype(BF16), row(g_post_mix), row(g_pre_ffn))
    return _ffn(hf, x1, w_ffn_in.astype(BF16), w_ffn_out.astype(BF16), row(g_post_ffn))


def kernel(x, g_pre_mix, g_post_mix, g_pre_ffn, g_post_ffn, w_in, gmlp_ln_g, gmlp_ln_b, gmlp_ws, gmlp_bs, nsa_pe_k, nsa_pe_v, nsa_wk1, nsa_wk2, nsa_wv1, nsa_wv2, rnn_conv_w, rnn_conv_b, rnn_wa, rnn_ba, rnn_wx, rnn_bx, rnn_lam, w_br_a, w_br_b, w_br_c, w_o, w_ffn_in, w_ffn_out):
    B, S, D = x.shape
    params = (g_pre_mix, g_post_mix, g_pre_ffn, g_post_ffn, w_in, gmlp_ln_g, gmlp_ln_b, gmlp_ws, gmlp_bs,
              nsa_pe_k, nsa_pe_v, nsa_wk1, nsa_wk2, nsa_wv1, nsa_wv2,
              rnn_conv_w, rnn_conv_b, rnn_wa, rnn_ba, rnn_wx, rnn_bx, rnn_lam,
              w_br_a, w_br_b, w_br_c, w_o, w_ffn_in, w_ffn_out)
    ovt = _overlap_t(S)
    x2d = x.reshape(B * S, D)
    for l in range(w_in.shape[0]):
        x2d = _layer(x2d, B, S, ovt, *(p[l] for p in params))
    return x2d.reshape(B, S, D)
```

```python
import functools

import jax
import jax.numpy as jnp
import numpy as np
from jax import lax
from jax.experimental import pallas as pl
from jax.experimental.pallas import tpu as pltpu

F32 = jnp.float32
BF16 = jnp.bfloat16

EPS = 1e-6
NEG_INF = -1e30
FORCE_SCORE = 1e4

D_MODEL = 1024
GMLP_WIDTH = 1024
GMLP_GROUPS = 4
GMLP_GROUP_DIM = GMLP_WIDTH // GMLP_GROUPS
GMLP_CHUNK = 128

NSA_HEADS = 16
NSA_KV_GROUPS = 4
NSA_HEAD_DIM = 64
NSA_HPG = NSA_HEADS // NSA_KV_GROUPS
NSA_Q_WIDTH = NSA_HEADS * NSA_HEAD_DIM
NSA_KV_WIDTH = NSA_KV_GROUPS * NSA_HEAD_DIM
N_BRANCH = 3
CMP_BLOCK = 32
CMP_STRIDE = 16
CMP_HIDDEN = 256
SEL_BLOCK = 64
SEL_TOP_N = 16
WINDOW = 512
Q_SCALE = NSA_HEAD_DIM ** -0.5 * float(np.log2(np.e))

RNN_WIDTH = 1024
RNN_HEADS = 16
RNN_HEAD_DIM = RNN_WIDTH // RNN_HEADS
CONV_WIDTH = 4
LRU_C = 8.0
RNN_BLOCK = 256

D_FF = 2816

OFF_U = 0
OFF_V = 1024
OFF_XR = 2048
OFF_RG = 3072
OFF_MG = 4096
OFF_KC = 7168
OFF_NG = OFF_KC + 2 * NSA_KV_WIDTH
NG_PAD = 512
OFF_Q = OFF_NG + NG_PAD
OFF_KV = OFF_Q + NSA_Q_WIDTH
D_IN_PAD = OFF_KV + 4 * NSA_KV_WIDTH

LANES = 128
BF16_SUBLANES = 16
V7X_VMEM_LIMIT = 56 * 1024 * 1024


def _cparams(sem, vmem=V7X_VMEM_LIMIT):
    return pltpu.CompilerParams(dimension_semantics=sem, vmem_limit_bytes=vmem)


def _rms(x, g):
    ms = jnp.mean(x * x, axis=-1, keepdims=True)
    return x * lax.rsqrt(ms + EPS) * g


def _gelu(x):
    return jax.nn.gelu(x)


def _sigmoid(x):
    return jax.nn.sigmoid(x)


_NT = (((1,), (1,)), ((), ()))


def _in_proj_kernel(x_ref, g_ref, w_ref, o_ref, h_scr):
    @pl.when(pl.program_id(1) == 0)
    def _():
        h_scr[...] = _rms(x_ref[...], g_ref[...]).astype(BF16)

    o_ref[...] = jnp.dot(h_scr[...], w_ref[...], preferred_element_type=F32).astype(o_ref.dtype)


def _in_proj(x2d, g, w, *, tm=1024, tn=2048):
    T, D = x2d.shape
    N = w.shape[1]
    return pl.pallas_call(
        _in_proj_kernel,
        grid=(T // tm, N // tn),
        in_specs=[
            pl.BlockSpec((tm, D), lambda i, j: (i, 0)),
            pl.BlockSpec((1, D), lambda i, j: (0, 0)),
            pl.BlockSpec((D, tn), lambda i, j: (0, j)),
        ],
        out_specs=pl.BlockSpec((tm, tn), lambda i, j: (i, j)),
        out_shape=jax.ShapeDtypeStruct((T, N), BF16),
        scratch_shapes=[pltpu.VMEM((tm, D), BF16)],
        compiler_params=_cparams(("parallel", "arbitrary")),
        name="in_proj",
    )(x2d, g, w)


def _gmlp_kernel(u_ref, v_ref, lng_ref, lnb_ref, ws_ref, bst_ref, o_ref, *, n_chunks):
    C = GMLP_CHUNK
    row = lax.broadcasted_iota(jnp.int32, (C, C), 0)
    col = lax.broadcasted_iota(jnp.int32, (C, C), 1)
    causal = col <= row
    ws = [jnp.where(causal, ws_ref[g], 0.0).astype(BF16) for g in range(GMLP_GROUPS)]
    bst = bst_ref[...]
    for c in range(n_chunks):
        rows = slice(c * C, (c + 1) * C)
        gv = _gelu(v_ref[rows, :].astype(F32))
        mu = jnp.mean(gv, axis=-1, keepdims=True)
        d = gv - mu
        var = jnp.mean(d * d, axis=-1, keepdims=True)
        vn = (d * lax.rsqrt(var + EPS) * lng_ref[...] + lnb_ref[...]).astype(BF16)
        for g in range(GMLP_GROUPS):
            cols = slice(g * GMLP_GROUP_DIM, (g + 1) * GMLP_GROUP_DIM)
            mixed = jnp.dot(ws[g], vn[:, cols], preferred_element_type=F32) + bst[:, g:g + 1]
            o_ref[rows, cols] = (_gelu(u_ref[rows, cols].astype(F32)) * mixed).astype(o_ref.dtype)


def _gmlp(z, lng, lnb, ws, bst, *, n_chunks=8):
    T = z.shape[0]
    tm = GMLP_CHUNK * n_chunks
    W = GMLP_WIDTH
    return pl.pallas_call(
        functools.partial(_gmlp_kernel, n_chunks=n_chunks),
        grid=(T // tm,),
        in_specs=[
            pl.BlockSpec((tm, W), lambda i: (i, OFF_U // W)),
            pl.BlockSpec((tm, W), lambda i: (i, OFF_V // W)),
            pl.BlockSpec((1, W), lambda i: (0, 0)),
            pl.BlockSpec((1, W), lambda i: (0, 0)),
            pl.BlockSpec((GMLP_GROUPS, GMLP_CHUNK, GMLP_CHUNK), lambda i: (0, 0, 0)),
            pl.BlockSpec((GMLP_CHUNK, GMLP_GROUPS), lambda i: (0, 0)),
        ],
        out_specs=pl.BlockSpec((tm, W), lambda i: (i, 0)),
        out_shape=jax.ShapeDtypeStruct((T, W), BF16),
        compiler_params=_cparams(("parallel",)),
        name="gmlp",
    )(z, z, lng, lnb, ws, bst)


def _rglru_kernel(xr_ref, rg_ref, cw_ref, cb_ref, wab_ref, ba_ref, bx_ref, lam_ref, o_ref,
                  ext_scr, a_scr, b_scr, hc_scr, *, tt):
    W = RNN_WIDTH
    groups = tt // 8
    sub_row = lax.broadcasted_iota(jnp.int32, (groups, 8, RNN_BLOCK), 1)

    @pl.when(pl.program_id(1) == 0)
    def _():
        ext_scr[0:8, :] = jnp.zeros((8, W), F32)
        hc_scr[...] = jnp.zeros((8, W), F32)

    xr = xr_ref[...].astype(F32)
    ext_scr[8:8 + tt, :] = xr
    cw = cw_ref[...]
    xc = (cw[3:4] * xr + cw[2:3] * ext_scr[7:7 + tt, :] + cw[1:2] * ext_scr[6:6 + tt, :]
          + cw[0:1] * ext_scr[5:5 + tt, :] + cb_ref[...])
    ext_scr[0:8, :] = xr[tt - 8:tt, :]

    lam = lam_ref[...]
    neg = -lam
    softplus = jnp.maximum(neg, 0.0) + jnp.log1p(jnp.exp(-jnp.abs(neg)))
    for k in range(W // RNN_BLOCK):
        cols = slice(k * RNN_BLOCK, (k + 1) * RNN_BLOCK)
        xck = xc[:, cols]
        gates = jnp.dot(xck.astype(BF16), wab_ref[k], preferred_element_type=F32)
        r = _sigmoid(gates[:, :RNN_BLOCK] + ba_ref[:, cols])
        i = _sigmoid(gates[:, RNN_BLOCK:] + bx_ref[:, cols])
        log_a = -LRU_C * r * softplus[:, cols]
        a = jnp.exp(log_a)
        one_minus_a2 = -jnp.tanh(log_a) * (1.0 + a * a)
        b_in = jnp.sqrt(one_minus_a2) * (i * xck)
        a3 = a.reshape(groups, 8, RNN_BLOCK)
        b3 = b_in.reshape(groups, 8, RNN_BLOCK)
        for step in (1, 2, 4):
            keep = sub_row >= step
            a_prev = jnp.where(keep, pltpu.roll(a3, step, axis=1), 1.0)
            b_prev = jnp.where(keep, pltpu.roll(b3, step, axis=1), 0.0)
            b3 = a3 * b_prev + b3
            a3 = a3 * a_prev
        a_scr[:, cols] = a3.reshape(tt, RNN_BLOCK)
        b_scr[:, cols] = b3.reshape(tt, RNN_BLOCK)

    def carry_step(i, h_prev):
        base = pl.multiple_of(i * BF16_SUBLANES, BF16_SUBLANES)
        gate = _gelu(rg_ref[pl.ds(base, BF16_SUBLANES), :].astype(F32))
        ys = []
        for half in range(BF16_SUBLANES // 8):
            rows = pl.ds(base + half * 8, 8)
            h = b_scr[rows, :] + a_scr[rows, :] * h_prev
            ys.append(h * gate[half * 8:(half + 1) * 8, :])
            h_prev = h[7:8, :]
        o_ref[pl.ds(base, BF16_SUBLANES), :] = jnp.concatenate(ys, axis=0).astype(o_ref.dtype)
        return h_prev

    h_last = lax.fori_loop(0, tt // BF16_SUBLANES, carry_step, hc_scr[0:1, :])
    hc_scr[...] = jnp.broadcast_to(h_last, (8, W))


def _rglru(z, cw, cb, wab, ba, bx, lam, *, B, S, tt=512):
    T = z.shape[0]
    W = RNN_WIDTH
    nt = S // tt
    return pl.pallas_call(
        functools.partial(_rglru_kernel, tt=tt),
        grid=(B, nt),
        in_specs=[
            pl.BlockSpec((tt, W), lambda b, t: (b * nt + t, OFF_XR // W)),
            pl.BlockSpec((tt, W), lambda b, t: (b * nt + t, OFF_RG // W)),
            pl.BlockSpec((CONV_WIDTH, W), lambda b, t: (0, 0)),
            pl.BlockSpec((1, W), lambda b, t: (0, 0)),
            pl.BlockSpec((W // RNN_BLOCK, RNN_BLOCK, 2 * RNN_BLOCK), lambda b, t: (0, 0, 0)),
            pl.BlockSpec((1, W), lambda b, t: (0, 0)),
            pl.BlockSpec((1, W), lambda b, t: (0, 0)),
            pl.BlockSpec((1, W), lambda b, t: (0, 0)),
        ],
        out_specs=pl.BlockSpec((tt, W), lambda b, t: (b * nt + t, 0)),
        out_shape=jax.ShapeDtypeStruct((T, W), BF16),
        scratch_shapes=[
            pltpu.VMEM((tt + 8, W), F32),
            pltpu.VMEM((tt, W), F32),
            pltpu.VMEM((tt, W), F32),
            pltpu.VMEM((8, W), F32),
        ],
        compiler_params=_cparams(("parallel", "arbitrary")),
        name="rglru",
    )(z, z, cw, cb, wab, ba, bx, lam)


def _compress_hidden(x_ref, pe_ref, w1_ref, x_scr):
    DH = NSA_HEAD_DIM
    half = CMP_STRIDE * DH
    slots = x_ref.shape[0] // CMP_STRIDE
    x = x_ref[...].astype(F32)
    for c in range(NSA_KV_WIDTH // LANES):
        x_scr[c] = x[:, c * LANES:(c + 1) * LANES]
    phase = [[x_scr[c, pl.ds(l, slots, stride=CMP_STRIDE), :] for c in range(NSA_KV_WIDTH // LANES)]
             for l in range(CMP_STRIDE)]
    per_block = LANES // DH

    def group_rows(g):
        lanes = slice((g % per_block) * DH, (g % per_block + 1) * DH)
        return jnp.concatenate([phase[l][g // per_block][:, lanes] for l in range(CMP_STRIDE)], axis=1)

    hm = jnp.concatenate([group_rows(g) for g in range(NSA_KV_GROUPS)], axis=0).astype(BF16)
    rows = hm.shape[0]
    lo = jnp.dot(hm, w1_ref[0:half, :], preferred_element_type=F32)
    hi = jnp.dot(hm, w1_ref[half:2 * half, :], preferred_element_type=F32)
    pe = jnp.broadcast_to(pe_ref[...], (8, 2 * half)).astype(BF16)
    pe_term = jnp.dot(pe, w1_ref[...], preferred_element_type=F32)[0:1, :]
    return _gelu(lo + pltpu.roll(hi, rows - 1, axis=0) + pe_term).astype(BF16)


def _compress_kernel(xk_ref, xv_ref, pek_ref, pev_ref, wk1_ref, wk2_ref, wv1_ref, wv2t_ref, kc_ref, vct_ref, x_scr):
    kc_ref[...] = jnp.dot(_compress_hidden(xk_ref, pek_ref, wk1_ref, x_scr), wk2_ref[...],
                          preferred_element_type=F32)
    vct_ref[...] = lax.dot_general(wv2t_ref[...], _compress_hidden(xv_ref, pev_ref, wv1_ref, x_scr), _NT,
                                   preferred_element_type=F32)


def _compress(z, pek, pev, wk1, wk2, wv1, wv2t, *, B, S):
    G, DH = NSA_KV_GROUPS, NSA_HEAD_DIM
    slots = S // CMP_STRIDE
    KVW = NSA_KV_WIDTH
    full = lambda shape: pl.BlockSpec(shape, lambda b: (0,) * len(shape))
    return pl.pallas_call(
        _compress_kernel,
        grid=(B,),
        in_specs=[pl.BlockSpec((S, KVW), lambda b: (b, OFF_KC // KVW)),
                  pl.BlockSpec((S, KVW), lambda b: (b, OFF_KC // KVW + 1)),
                  full(pek.shape), full(pev.shape), full(wk1.shape), full(wk2.shape), full(wv1.shape),
                  full(wv2t.shape)],
        out_specs=[pl.BlockSpec((G * slots, DH), lambda b: (b, 0)), pl.BlockSpec((DH, G * slots), lambda b: (0, b))],
        out_shape=[jax.ShapeDtypeStruct((B * G * slots, DH), F32), jax.ShapeDtypeStruct((DH, B * G * slots), F32)],
        scratch_shapes=[pltpu.VMEM((KVW // LANES, S, LANES), F32)],
        compiler_params=_cparams(("parallel",)),
        name="nsa_compress",
    )(z, z, pek, pev, wk1, wk2, wv1, wv2t)


def _nsa_kernel(q_ref, kc_ref, vct_ref, ks_ref, vs_ref, kw_ref, vw_ref, gate_ref, ovt_ref, o_ref,
                ks_scr, kw_scr, vs_scr, vw_scr, gt_scr, bias_scr, sa_scr, sb_scr, m_scr, acc_scr, *, tq, S):
    HP, DH = NSA_HPG, NSA_HEAD_DIM
    M = HP * tq
    tk = tq
    n_kt = S // tk
    n_sel = S // SEL_BLOCK
    slots = kc_ref.shape[0]
    v_rows = DH + BF16_SUBLANES
    g_idx = pl.program_id(1)
    qi = pl.program_id(2)
    t0 = qi * tq

    for g in range(NSA_KV_GROUPS):
        @pl.when((qi == 0) & (g_idx == g))
        def _():
            lanes = slice(g * DH, (g + 1) * DH)
            ks_scr[...] = ks_ref[:, lanes]
            kw_scr[...] = kw_ref[:, lanes]
            ones = jnp.ones((BF16_SUBLANES, tk), BF16)
            for v_ref, v_scr in ((vs_ref, vs_scr), (vw_ref, vw_scr)):
                vt = v_ref[:, lanes].T
                for kt in range(n_kt):
                    v_scr[kt, 0:DH, :] = vt[:, kt * tk:(kt + 1) * tk]
                    v_scr[kt, DH:v_rows, :] = ones

    def per_head(x):
        return jnp.concatenate([x] * HP, axis=1)

    q_t = q_ref[...].T
    qt = jnp.concatenate([q_t[h * DH:(h + 1) * DH, :] for h in range(HP)], axis=1)

    def scores(k_scr, kt, dst):
        start = pl.multiple_of(kt * tk, tk)
        dst[...] = jnp.dot(k_scr[pl.ds(start, tk), :], qt, preferred_element_type=F32)

    scores(ks_scr, 0, sa_scr)

    sc = jnp.dot(kc_ref[...].astype(BF16), qt, preferred_element_type=F32)
    n_idx = lax.broadcasted_iota(jnp.int32, (slots, tq), 0)
    t_idx = t0 + lax.broadcasted_iota(jnp.int32, (slots, tq), 1)
    valid = per_head(jnp.where(n_idx * CMP_STRIDE + (CMP_BLOCK - 1) <= t_idx, 1.0, 0.0)) > 0.5
    sc = jnp.where(valid, sc, NEG_INF)
    mx = jnp.max(sc, axis=0, keepdims=True)
    p = jnp.where(valid, jnp.exp2(sc - mx), 0.0)
    den = jnp.sum(p, axis=0, keepdims=True)
    p_c = p * (1.0 / jnp.where(den > 0.0, den, 1.0))
    o_cmp = jnp.dot(vct_ref[...].astype(BF16), p_c.astype(BF16), preferred_element_type=F32)

    p_sum = p_c[:, 0:tq]
    for h in range(1, HP):
        p_sum = p_sum + p_c[:, h * tq:(h + 1) * tq]
    imp = jnp.dot(ovt_ref[...], p_sum, preferred_element_type=F32, precision=lax.Precision.HIGHEST)
    j_idx = lax.broadcasted_iota(jnp.int32, (n_sel, tq), 0)
    cur = (t0 + lax.broadcasted_iota(jnp.int32, (n_sel, tq), 1)) // SEL_BLOCK
    forced = (j_idx == 0) | (j_idx == cur) | (j_idx == cur - 1)
    imp = jnp.where(forced, FORCE_SCORE, jnp.where(j_idx > cur, NEG_INF, imp))
    rank = jnp.zeros((n_sel, tq), F32)
    for i in range(n_sel):
        ci = imp[i:i + 1, :]
        ge = jnp.where(ci >= imp, 1.0, 0.0)
        gt = jnp.where(ci > imp, 1.0, 0.0)
        rank = rank + jnp.where(j_idx > i, ge, gt)
    sel = jnp.where(rank < float(min(SEL_TOP_N, n_sel)), 1.0, 0.0)

    per_tile = tk // SEL_BLOCK
    kloc = lax.broadcasted_iota(jnp.int32, (tk, tq), 0)
    tloc = lax.broadcasted_iota(jnp.int32, (tk, tq), 1)
    for kt in range(n_kt):
        @pl.when(kt <= qi)
        def _():
            hit = jnp.concatenate(
                [jnp.broadcast_to(sel[kt * per_tile + jj:kt * per_tile + jj + 1, :], (SEL_BLOCK, tq))
                 for jj in range(per_tile)], axis=0)
            bias_scr[kt] = jnp.where((hit > 0.5) & (kt * tk + kloc <= t0 + tloc), 0.0, NEG_INF)

    def absorb(v_ext, src, bias):
        s = src[...] if bias is None else src[...] + per_head(bias)
        m_old = m_scr[...]
        m_new = jnp.maximum(m_old, jnp.max(s, axis=0, keepdims=True))
        alpha = jnp.exp2(m_old - m_new)
        pr = jnp.exp2(s - m_new).astype(BF16)
        m_scr[...] = m_new
        acc_scr[...] = alpha * acc_scr[...] + jnp.dot(v_ext, pr, preferred_element_type=F32)

    def reset():
        m_scr[...] = jnp.full((1, M), NEG_INF, F32)
        acc_scr[...] = jnp.zeros((v_rows, M), F32)

    def result():
        acc = acc_scr[...]
        return acc[0:DH, :] * (1.0 / acc[DH:DH + 1, :])

    reset()
    n_tiles = qi + 1

    def slc_scores(kt, dst):
        scores(ks_scr, jnp.minimum(kt, qi), dst)

    def slc_pair(j, carry):
        slc_scores(2 * j + 1, sb_scr)
        absorb(vs_scr[2 * j], sa_scr, bias_scr[2 * j])
        slc_scores(2 * j + 2, sa_scr)
        absorb(vs_scr[2 * j + 1], sb_scr, bias_scr[2 * j + 1])
        return carry

    lax.fori_loop(0, n_tiles // 2, slc_pair, 0)

    @pl.when(n_tiles % 2 == 1)
    def _():
        absorb(vs_scr[qi], sa_scr, bias_scr[qi])

    o_slc = result()

    reset()
    n_back = WINDOW // tk
    diag_bias = jnp.where(kloc <= tloc, 0.0, NEG_INF)
    far_bias = jnp.where(kloc > tloc, 0.0, NEG_INF)

    def win_bias(d):
        return diag_bias if d == 0 else (far_bias if d == n_back else None)

    def window(n_win):
        bufs = (sa_scr, sb_scr)
        scores(kw_scr, qi, bufs[0])
        for d in range(n_win):
            if d + 1 < n_win:
                scores(kw_scr, qi - (d + 1), bufs[(d + 1) % 2])
            absorb(vw_scr[qi - d], bufs[d % 2], win_bias(d))

    for n_win in range(1, n_back + 2):
        @pl.when((qi == n_win - 1) if n_win <= n_back else (qi >= n_back))
        def _():
            window(n_win)

    o_win = result()

    gt_scr[...] = gate_ref[...].astype(F32).T

    def gate(branch):
        rows = [gt_scr[pl.ds(g_idx * (HP * N_BRANCH) + h * N_BRANCH + branch, 1), :] for h in range(HP)]
        return _sigmoid(jnp.concatenate(rows, axis=1))

    o = gate(0) * o_cmp + gate(1) * o_slc + gate(2) * o_win
    o_ref[...] = jnp.concatenate([o[:, h * tq:(h + 1) * tq].T for h in range(HP)], axis=1).astype(o_ref.dtype)


def _nsa(z, kc, vct, ovt, *, B, S, tq=256):
    G, HP, DH = NSA_KV_GROUPS, NSA_HPG, NSA_HEAD_DIM
    T = B * S
    nq = S // tq
    slots = S // CMP_STRIDE
    n_sel = S // SEL_BLOCK
    n_kt = S // tq
    v_rows = DH + BF16_SUBLANES
    KVW = NSA_KV_WIDTH

    def kv_spec(which):
        return pl.BlockSpec((S, KVW), lambda b, g, i: (b, OFF_KV // KVW + which))

    return pl.pallas_call(
        functools.partial(_nsa_kernel, tq=tq, S=S),
        grid=(B, G, nq),
        in_specs=[
            pl.BlockSpec((tq, HP * DH), lambda b, g, i: (b * nq + i, OFF_Q // (HP * DH) + g)),
            pl.BlockSpec((slots, DH), lambda b, g, i: (b * G + g, 0)),
            pl.BlockSpec((DH, slots), lambda b, g, i: (0, b * G + g)),
            kv_spec(0), kv_spec(1), kv_spec(2), kv_spec(3),
            pl.BlockSpec((tq, LANES), lambda b, g, i: (b * nq + i, OFF_NG // LANES)),
            pl.BlockSpec((n_sel, slots), lambda b, g, i: (0, 0)),
        ],
        out_specs=pl.BlockSpec((tq, HP * DH), lambda b, g, i: (b * nq + i, g)),
        out_shape=jax.ShapeDtypeStruct((T, NSA_Q_WIDTH), BF16),
        scratch_shapes=[
            pltpu.VMEM((S, DH), BF16),
            pltpu.VMEM((S, DH), BF16),
            pltpu.VMEM((n_kt, v_rows, tq), BF16),
            pltpu.VMEM((n_kt, v_rows, tq), BF16),
            pltpu.VMEM((LANES, tq), F32),
            pltpu.VMEM((n_kt, tq, tq), F32),
            pltpu.VMEM((tq, HP * tq), F32),
            pltpu.VMEM((tq, HP * tq), F32),
            pltpu.VMEM((1, HP * tq), F32),
            pltpu.VMEM((v_rows, HP * tq), F32),
        ],
        compiler_params=_cparams(("parallel", "parallel", "arbitrary")),
        name="nsa_attention",
    )(z, kc, vct, z, z, z, z, z, ovt)


def _merge_kernel(ya_ref, yb_ref, yc_ref, mga_ref, mgb_ref, mgc_ref, x_ref, wa_ref, wb_ref, wc_ref, wo_ref,
                  gpost_ref, gpre_ref, x1_ref, hf_ref):
    gate = lambda ref: _sigmoid(ref[...].astype(F32))
    merged = gate(mga_ref) * jnp.dot(ya_ref[...], wa_ref[...], preferred_element_type=F32)
    merged = merged + gate(mgb_ref) * jnp.dot(yb_ref[...], wb_ref[...], preferred_element_type=F32)
    merged = merged + gate(mgc_ref) * jnp.dot(yc_ref[...], wc_ref[...], preferred_element_type=F32)
    y = jnp.dot(merged.astype(BF16), wo_ref[...], preferred_element_type=F32)
    x1 = x_ref[...] + _rms(y, gpost_ref[...])
    x1_ref[...] = x1
    hf_ref[...] = _rms(x1, gpre_ref[...]).astype(BF16)


def _merge(ya, yb, yc, z, x2d, wa, wb, wc, wo, gpost, gpre, *, tm=512):
    T, D = x2d.shape
    row = lambda c: pl.BlockSpec((tm, D), lambda i: (i, c))
    wfull = pl.BlockSpec((D, D), lambda i: (0, 0))
    vec = pl.BlockSpec((1, D), lambda i: (0, 0))
    mg0 = OFF_MG // D
    return pl.pallas_call(
        _merge_kernel,
        grid=(T // tm,),
        in_specs=[row(0), row(0), row(0), row(mg0), row(mg0 + 1), row(mg0 + 2), row(0),
                  wfull, wfull, wfull, wfull, vec, vec],
        out_specs=[row(0), row(0)],
        out_shape=[jax.ShapeDtypeStruct((T, D), F32), jax.ShapeDtypeStruct((T, D), BF16)],
        compiler_params=_cparams(("parallel",)),
        name="merge",
    )(ya, yb, yc, z, z, z, x2d, wa, wb, wc, wo, gpost, gpre)


def _ffn_kernel(hf_ref, x1_ref, win_ref, wout_ref, gpost_ref, o_ref, acc_scr, *, fc):
    hf = hf_ref[...]
    n_chunks = D_FF // fc
    for c in range(n_chunks):
        gate = jnp.dot(hf, win_ref[:, c * fc:(c + 1) * fc], preferred_element_type=F32)
        up = jnp.dot(hf, win_ref[:, D_FF + c * fc:D_FF + (c + 1) * fc], preferred_element_type=F32)
        act = (gate * _sigmoid(gate) * up).astype(BF16)
        part = jnp.dot(act, wout_ref[c * fc:(c + 1) * fc, :], preferred_element_type=F32)
        if c == 0:
            acc_scr[...] = part
        else:
            acc_scr[...] += part
    o_ref[...] = x1_ref[...] + _rms(acc_scr[...], gpost_ref[...])


def _ffn(hf, x1, win, wout, gpost, *, tm=512, fc=1408):
    T, D = x1.shape
    row = pl.BlockSpec((tm, D), lambda i: (i, 0))
    return pl.pallas_call(
        functools.partial(_ffn_kernel, fc=fc),
        grid=(T // tm,),
        in_specs=[row, row,
                  pl.BlockSpec(win.shape, lambda i: (0, 0), pipeline_mode=pl.Buffered(1)),
                  pl.BlockSpec(wout.shape, lambda i: (0, 0), pipeline_mode=pl.Buffered(1)),
                  pl.BlockSpec((1, D), lambda i: (0, 0))],
        out_specs=row,
        out_shape=jax.ShapeDtypeStruct((T, D), F32),
        scratch_shapes=[pltpu.VMEM((tm, D), F32)],
        compiler_params=_cparams(("parallel",)),
        name="ffn",
    )(hf, x1, win, wout, gpost)


def _overlap_t(S):
    slots = S // CMP_STRIDE
    n_sel = S // SEL_BLOCK
    c_start = np.arange(slots) * CMP_STRIDE
    s_start = np.arange(n_sel) * SEL_BLOCK
    overlap = np.clip(np.minimum(c_start[:, None] + CMP_BLOCK, s_start[None, :] + SEL_BLOCK)
                      - np.maximum(c_start[:, None], s_start[None, :]), 0, None) / CMP_BLOCK
    return jnp.asarray(overlap.T, F32)


def _reorder_w_in(w_in):
    s = np.cumsum([0, GMLP_WIDTH, GMLP_WIDTH, NSA_Q_WIDTH, 6 * NSA_KV_WIDTH, N_BRANCH * NSA_HEADS,
                   RNN_WIDTH, RNN_WIDTH, N_BRANCH * D_MODEL])
    u, v, q, kv, ng, xr, rg, mg = (w_in[:, s[i]:s[i + 1]] for i in range(8))
    kv_cmp, kv_rest = kv[:, :2 * NSA_KV_WIDTH], kv[:, 2 * NSA_KV_WIDTH:]
    q = q * Q_SCALE
    pad = jnp.zeros((w_in.shape[0], NG_PAD - N_BRANCH * NSA_HEADS), w_in.dtype)
    return jnp.concatenate([u, v, xr, rg, mg, kv_cmp, ng, pad, q, kv_rest], axis=1).astype(BF16)


def _block_diag_gates(wa, wx):
    per = RNN_BLOCK // RNN_HEAD_DIM
    nblk = RNN_HEADS // per
    eye = jnp.eye(per, dtype=wa.dtype)

    def bd(w):
        w = w.reshape(nblk, per, RNN_HEAD_DIM, RNN_HEAD_DIM)
        return jnp.einsum('kpio,pq->kpiqo', w, eye).reshape(nblk, RNN_BLOCK, RNN_BLOCK)

    return jnp.concatenate([bd(wa), bd(wx)], axis=-1).astype(BF16)


def _layer(x2d, B, S, ovt, g_pre_mix, g_post_mix, g_pre_ffn, g_post_ffn, w_in,
           gmlp_ln_g, gmlp_ln_b, gmlp_ws, gmlp_bs,
           nsa_pe_k, nsa_pe_v, nsa_wk1, nsa_wk2, nsa_wv1, nsa_wv2,
           rnn_conv_w, rnn_conv_b, rnn_wa, rnn_ba, rnn_wx, rnn_bx, rnn_lam,
           w_br_a, w_br_b, w_br_c, w_o, w_ffn_in, w_ffn_out):
    row = lambda a: a.reshape(1, -1)
    z = _in_proj(x2d, row(g_pre_mix), _reorder_w_in(w_in))

    y_a = _gmlp(z, row(gmlp_ln_g), row(gmlp_ln_b), gmlp_ws, gmlp_bs.T)
    y_c = _rglru(z, rnn_conv_w, row(rnn_conv_b), _block_diag_gates(rnn_wa, rnn_wx),
                 row(rnn_ba), row(rnn_bx), row(rnn_lam), B=B, S=S)

    kc, vct = _compress(z, nsa_pe_k.reshape(1, -1), nsa_pe_v.reshape(1, -1), nsa_wk1.astype(BF16),
                        nsa_wk2.astype(BF16), nsa_wv1.astype(BF16), nsa_wv2.T.astype(BF16), B=B, S=S)
    y_b = _nsa(z, kc, vct, ovt, B=B, S=S)

    x1, hf = _merge(y_a, y_b, y_c, z, x2d, w_br_a.astype(BF16), w_br_b.astype(BF16), w_br_c.astype(BF16),
                    w_o.astype(BF16), row(g_post_mix), row(g_pre_ffn))
    return _ffn(hf, x1, w_ffn_in.astype(BF16), w_ffn_out.astype(BF16), row(g_post_ffn))


def kernel(x, g_pre_mix, g_post_mix, g_pre_ffn, g_post_ffn, w_in, gmlp_ln_g, gmlp_ln_b, gmlp_ws, gmlp_bs, nsa_pe_k, nsa_pe_v, nsa_wk1, nsa_wk2, nsa_wv1, nsa_wv2, rnn_conv_w, rnn_conv_b, rnn_wa, rnn_ba, rnn_wx, rnn_bx, rnn_lam, w_br_a, w_br_b, w_br_c, w_o, w_ffn_in, w_ffn_out):
    B, S, D = x.shape
    params = (g_pre_mix, g_post_mix, g_pre_ffn, g_post_ffn, w_in, gmlp_ln_g, gmlp_ln_b, gmlp_ws, gmlp_bs,
              nsa_pe_k, nsa_pe_v, nsa_wk1, nsa_wk2, nsa_wv1, nsa_wv2,
              rnn_conv_w, rnn_conv_b, rnn_wa, rnn_ba, rnn_wx, rnn_bx, rnn_lam,
              w_br_a, w_br_b, w_br_c, w_o, w_ffn_in, w_ffn_out)
    ovt = _overlap_t(S)
    x2d = x.reshape(B * S, D)
    for l in range(w_in.shape[0]):
        x2d = _layer(x2d, B, S, ovt, *(p[l] for p in params))
    return x2d.reshape(B, S, D)
```

```python
import functools

import jax
import jax.numpy as jnp
import numpy as np
from jax import lax
from jax.experimental import pallas as pl
from jax.experimental.pallas import tpu as pltpu

F32 = jnp.float32
BF16 = jnp.bfloat16

EPS = 1e-6
NEG_INF = -1e30
FORCE_SCORE = 1e4

D_MODEL = 1024
GMLP_WIDTH = 1024
GMLP_GROUPS = 4
GMLP_GROUP_DIM = GMLP_WIDTH // GMLP_GROUPS
GMLP_CHUNK = 128

NSA_HEADS = 16
NSA_KV_GROUPS = 4
NSA_HEAD_DIM = 64
NSA_HPG = NSA_HEADS // NSA_KV_GROUPS
NSA_Q_WIDTH = NSA_HEADS * NSA_HEAD_DIM
NSA_KV_WIDTH = NSA_KV_GROUPS * NSA_HEAD_DIM
N_BRANCH = 3
CMP_BLOCK = 32
CMP_STRIDE = 16
CMP_HIDDEN = 256
SEL_BLOCK = 64
SEL_TOP_N = 16
WINDOW = 512
Q_SCALE = NSA_HEAD_DIM ** -0.5 * float(np.log2(np.e))

RNN_WIDTH = 1024
RNN_HEADS = 16
RNN_HEAD_DIM = RNN_WIDTH // RNN_HEADS
CONV_WIDTH = 4
LRU_C = 8.0
RNN_BLOCK = 256

D_FF = 2816

OFF_U = 0
OFF_V = 1024
OFF_XR = 2048
OFF_RG = 3072
OFF_MG = 4096
OFF_KC = 7168
OFF_NG = OFF_KC + 2 * NSA_KV_WIDTH
NG_PAD = 512
OFF_Q = OFF_NG + NG_PAD
OFF_KV = OFF_Q + NSA_Q_WIDTH
D_IN_PAD = OFF_KV + 4 * NSA_KV_WIDTH

LANES = 128
V7X_MXU_WIDTH = 256
BF16_SUBLANES = 16
V7X_VMEM_LIMIT = 56 * 1024 * 1024


def _cparams(sem, vmem=V7X_VMEM_LIMIT):
    return pltpu.CompilerParams(dimension_semantics=sem, vmem_limit_bytes=vmem)


def _rms(x, g):
    ms = jnp.mean(x * x, axis=-1, keepdims=True)
    return x * lax.rsqrt(ms + EPS) * g


def _gelu(x):
    return jax.nn.gelu(x)


def _sigmoid(x):
    return jax.nn.sigmoid(x)


_NT = (((1,), (1,)), ((), ()))


def _in_proj_kernel(x_ref, g_ref, w_ref, o_ref, h_scr):
    @pl.when(pl.program_id(1) == 0)
    def _():
        h_scr[...] = _rms(x_ref[...], g_ref[...]).astype(BF16)

    o_ref[...] = jnp.dot(h_scr[...], w_ref[...], preferred_element_type=F32).astype(o_ref.dtype)


def _in_proj(x2d, g, w, *, tm=1024, tn=2048):
    T, D = x2d.shape
    N = w.shape[1]
    return pl.pallas_call(
        _in_proj_kernel,
        grid=(T // tm, N // tn),
        in_specs=[
            pl.BlockSpec((tm, D), lambda i, j: (i, 0)),
            pl.BlockSpec((1, D), lambda i, j: (0, 0)),
            pl.BlockSpec((D, tn), lambda i, j: (0, j)),
        ],
        out_specs=pl.BlockSpec((tm, tn), lambda i, j: (i, j)),
        out_shape=jax.ShapeDtypeStruct((T, N), BF16),
        scratch_shapes=[pltpu.VMEM((tm, D), BF16)],
        compiler_params=_cparams(("parallel", "arbitrary")),
        name="in_proj",
    )(x2d, g, w)


def _gmlp_kernel(u_ref, v_ref, lng_ref, lnb_ref, ws_ref, bst_ref, o_ref, *, n_chunks):
    C = GMLP_CHUNK
    row = lax.broadcasted_iota(jnp.int32, (C, C), 0)
    col = lax.broadcasted_iota(jnp.int32, (C, C), 1)
    causal = col <= row
    ws = [jnp.where(causal, ws_ref[g], 0.0).astype(BF16) for g in range(GMLP_GROUPS)]
    bst = bst_ref[...]
    for c in range(n_chunks):
        rows = slice(c * C, (c + 1) * C)
        gv = _gelu(v_ref[rows, :].astype(F32))
        mu = jnp.mean(gv, axis=-1, keepdims=True)
        d = gv - mu
        var = jnp.mean(d * d, axis=-1, keepdims=True)
        vn = (d * lax.rsqrt(var + EPS) * lng_ref[...] + lnb_ref[...]).astype(BF16)
        for g in range(GMLP_GROUPS):
            cols = slice(g * GMLP_GROUP_DIM, (g + 1) * GMLP_GROUP_DIM)
            mixed = jnp.dot(ws[g], vn[:, cols], preferred_element_type=F32) + bst[:, g:g + 1]
            o_ref[rows, cols] = (_gelu(u_ref[rows, cols].astype(F32)) * mixed).astype(o_ref.dtype)


def _gmlp(z, lng, lnb, ws, bst, *, n_chunks=8):
    T = z.shape[0]
    tm = GMLP_CHUNK * n_chunks
    W = GMLP_WIDTH
    return pl.pallas_call(
        functools.partial(_gmlp_kernel, n_chunks=n_chunks),
        grid=(T // tm,),
        in_specs=[
            pl.BlockSpec((tm, W), lambda i: (i, OFF_U // W)),
            pl.BlockSpec((tm, W), lambda i: (i, OFF_V // W)),
            pl.BlockSpec((1, W), lambda i: (0, 0)),
            pl.BlockSpec((1, W), lambda i: (0, 0)),
            pl.BlockSpec((GMLP_GROUPS, GMLP_CHUNK, GMLP_CHUNK), lambda i: (0, 0, 0)),
            pl.BlockSpec((GMLP_CHUNK, GMLP_GROUPS), lambda i: (0, 0)),
        ],
        out_specs=pl.BlockSpec((tm, W), lambda i: (i, 0)),
        out_shape=jax.ShapeDtypeStruct((T, W), BF16),
        compiler_params=_cparams(("parallel",)),
        name="gmlp",
    )(z, z, lng, lnb, ws, bst)


def _rglru_kernel(xr_ref, rg_ref, cw_ref, cb_ref, wab_ref, ba_ref, bx_ref, lam_ref, o_ref,
                  ext_scr, a_scr, b_scr, hc_scr, *, tt):
    W = RNN_WIDTH
    groups = tt // 8
    sub_row = lax.broadcasted_iota(jnp.int32, (groups, 8, RNN_BLOCK), 1)

    @pl.when(pl.program_id(1) == 0)
    def _():
        ext_scr[0:8, :] = jnp.zeros((8, W), F32)
        hc_scr[...] = jnp.zeros((8, W), F32)

    xr = xr_ref[...].astype(F32)
    ext_scr[8:8 + tt, :] = xr
    cw = cw_ref[...]
    xc = (cw[3:4] * xr + cw[2:3] * ext_scr[7:7 + tt, :] + cw[1:2] * ext_scr[6:6 + tt, :]
          + cw[0:1] * ext_scr[5:5 + tt, :] + cb_ref[...])
    ext_scr[0:8, :] = xr[tt - 8:tt, :]

    lam = lam_ref[...]
    neg = -lam
    softplus = jnp.maximum(neg, 0.0) + jnp.log1p(jnp.exp(-jnp.abs(neg)))
    for k in range(W // RNN_BLOCK):
        cols = slice(k * RNN_BLOCK, (k + 1) * RNN_BLOCK)
        xck = xc[:, cols]
        gates = jnp.dot(xck.astype(BF16), wab_ref[k], preferred_element_type=F32)
        r = _sigmoid(gates[:, :RNN_BLOCK] + ba_ref[:, cols])
        i = _sigmoid(gates[:, RNN_BLOCK:] + bx_ref[:, cols])
        log_a = -LRU_C * r * softplus[:, cols]
        a = jnp.exp(log_a)
        one_minus_a2 = -jnp.tanh(log_a) * (1.0 + a * a)
        b_in = jnp.sqrt(one_minus_a2) * (i * xck)
        a3 = a.reshape(groups, 8, RNN_BLOCK)
        b3 = b_in.reshape(groups, 8, RNN_BLOCK)
        for step in (1, 2, 4):
            keep = sub_row >= step
            a_prev = jnp.where(keep, pltpu.roll(a3, step, axis=1), 1.0)
            b_prev = jnp.where(keep, pltpu.roll(b3, step, axis=1), 0.0)
            b3 = a3 * b_prev + b3
            a3 = a3 * a_prev
        a_scr[:, cols] = a3.reshape(tt, RNN_BLOCK)
        b_scr[:, cols] = b3.reshape(tt, RNN_BLOCK)

    def carry_step(i, h_prev):
        base = pl.multiple_of(i * BF16_SUBLANES, BF16_SUBLANES)
        gate = _gelu(rg_ref[pl.ds(base, BF16_SUBLANES), :].astype(F32))
        ys = []
        for half in range(BF16_SUBLANES // 8):
            rows = pl.ds(base + half * 8, 8)
            h = b_scr[rows, :] + a_scr[rows, :] * h_prev
            ys.append(h * gate[half * 8:(half + 1) * 8, :])
            h_prev = h[7:8, :]
        o_ref[pl.ds(base, BF16_SUBLANES), :] = jnp.concatenate(ys, axis=0).astype(o_ref.dtype)
        return h_prev

    h_last = lax.fori_loop(0, tt // BF16_SUBLANES, carry_step, hc_scr[0:1, :])
    hc_scr[...] = jnp.broadcast_to(h_last, (8, W))


def _rglru(z, cw, cb, wab, ba, bx, lam, *, B, S, tt=512):
    T = z.shape[0]
    W = RNN_WIDTH
    nt = S // tt
    return pl.pallas_call(
        functools.partial(_rglru_kernel, tt=tt),
        grid=(B, nt),
        in_specs=[
            pl.BlockSpec((tt, W), lambda b, t: (b * nt + t, OFF_XR // W)),
            pl.BlockSpec((tt, W), lambda b, t: (b * nt + t, OFF_RG // W)),
            pl.BlockSpec((CONV_WIDTH, W), lambda b, t: (0, 0)),
            pl.BlockSpec((1, W), lambda b, t: (0, 0)),
            pl.BlockSpec((W // RNN_BLOCK, RNN_BLOCK, 2 * RNN_BLOCK), lambda b, t: (0, 0, 0)),
            pl.BlockSpec((1, W), lambda b, t: (0, 0)),
            pl.BlockSpec((1, W), lambda b, t: (0, 0)),
            pl.BlockSpec((1, W), lambda b, t: (0, 0)),
        ],
        out_specs=pl.BlockSpec((tt, W), lambda b, t: (b * nt + t, 0)),
        out_shape=jax.ShapeDtypeStruct((T, W), BF16),
        scratch_shapes=[
            pltpu.VMEM((tt + 8, W), F32),
            pltpu.VMEM((tt, W), F32),
            pltpu.VMEM((tt, W), F32),
            pltpu.VMEM((8, W), F32),
        ],
        compiler_params=_cparams(("parallel", "arbitrary")),
        name="rglru",
    )(z, z, cw, cb, wab, ba, bx, lam)


def _compress_hidden(x_ref, pe_ref, w1_ref, x_scr):
    DH = NSA_HEAD_DIM
    half = CMP_STRIDE * DH
    slots = x_ref.shape[0] // CMP_STRIDE
    x = x_ref[...].astype(F32)
    for c in range(NSA_KV_WIDTH // LANES):
        x_scr[c] = x[:, c * LANES:(c + 1) * LANES]
    phase = [[x_scr[c, pl.ds(l, slots, stride=CMP_STRIDE), :] for c in range(NSA_KV_WIDTH // LANES)]
             for l in range(CMP_STRIDE)]
    per_block = LANES // DH

    def group_rows(g):
        lanes = slice((g % per_block) * DH, (g % per_block + 1) * DH)
        return jnp.concatenate([phase[l][g // per_block][:, lanes] for l in range(CMP_STRIDE)], axis=1)

    hm = jnp.concatenate([group_rows(g) for g in range(NSA_KV_GROUPS)], axis=0).astype(BF16)
    rows = hm.shape[0]
    lo = jnp.dot(hm, w1_ref[0:half, :], preferred_element_type=F32)
    hi = jnp.dot(hm, w1_ref[half:2 * half, :], preferred_element_type=F32)
    pe = jnp.broadcast_to(pe_ref[...], (8, 2 * half)).astype(BF16)
    pe_term = jnp.dot(pe, w1_ref[...], preferred_element_type=F32)[0:1, :]
    return _gelu(lo + pltpu.roll(hi, rows - 1, axis=0) + pe_term).astype(BF16)


def _compress_kernel(xk_ref, xv_ref, pek_ref, pev_ref, wk1_ref, wk2_ref, wv1_ref, wv2t_ref, kc_ref, vct_ref, x_scr):
    kc_ref[...] = jnp.dot(_compress_hidden(xk_ref, pek_ref, wk1_ref, x_scr), wk2_ref[...],
                          preferred_element_type=F32)
    vct_ref[...] = lax.dot_general(wv2t_ref[...], _compress_hidden(xv_ref, pev_ref, wv1_ref, x_scr), _NT,
                                   preferred_element_type=F32)


def _compress(z, pek, pev, wk1, wk2, wv1, wv2t, *, B, S):
    G, DH = NSA_KV_GROUPS, NSA_HEAD_DIM
    slots = S // CMP_STRIDE
    KVW = NSA_KV_WIDTH
    full = lambda shape: pl.BlockSpec(shape, lambda b: (0,) * len(shape))
    return pl.pallas_call(
        _compress_kernel,
        grid=(B,),
        in_specs=[pl.BlockSpec((S, KVW), lambda b: (b, OFF_KC // KVW)),
                  pl.BlockSpec((S, KVW), lambda b: (b, OFF_KC // KVW + 1)),
                  full(pek.shape), full(pev.shape), full(wk1.shape), full(wk2.shape), full(wv1.shape),
                  full(wv2t.shape)],
        out_specs=[pl.BlockSpec((G * slots, DH), lambda b: (b, 0)), pl.BlockSpec((DH, G * slots), lambda b: (0, b))],
        out_shape=[jax.ShapeDtypeStruct((B * G * slots, DH), F32), jax.ShapeDtypeStruct((DH, B * G * slots), F32)],
        scratch_shapes=[pltpu.VMEM((KVW // LANES, S, LANES), F32)],
        compiler_params=_cparams(("parallel",)),
        name="nsa_compress",
    )(z, z, pek, pev, wk1, wk2, wv1, wv2t)


def _nsa_kernel(q_ref, kc_ref, vct_ref, ks_ref, vs_ref, kw_ref, vw_ref, gate_ref, o_ref,
                ks_scr, kw_scr, vs_scr, vw_scr, gt_scr, ps_scr, sel_scr, sa_scr, sb_scr, m_scr, acc_scr, *, tq, S):
    HP, DH = NSA_HPG, NSA_HEAD_DIM
    M = HP * tq
    tk = tq
    n_kt = S // tk
    n_sel = S // SEL_BLOCK
    slots = kc_ref.shape[0]
    v_rows = DH + BF16_SUBLANES
    g_idx = pl.program_id(1)
    qi = pl.program_id(2)
    t0 = qi * tq

    for g in range(NSA_KV_GROUPS):
        @pl.when((qi == 0) & (g_idx == g))
        def _():
            lanes = slice(g * DH, (g + 1) * DH)
            ks_scr[...] = ks_ref[:, lanes]
            kw_scr[...] = kw_ref[:, lanes]
            ones = jnp.ones((BF16_SUBLANES, tk), BF16)
            for v_ref, v_scr in ((vs_ref, vs_scr), (vw_ref, vw_scr)):
                vt = v_ref[:, lanes].T
                for kt in range(n_kt):
                    v_scr[kt, 0:DH, :] = vt[:, kt * tk:(kt + 1) * tk]
                    v_scr[kt, DH:v_rows, :] = ones

    def per_head(x):
        return jnp.concatenate([x] * HP, axis=1)

    q_t = q_ref[...].T
    qt = jnp.concatenate([q_t[h * DH:(h + 1) * DH, :] for h in range(HP)], axis=1)

    def scores(k_scr, kt, dst):
        start = pl.multiple_of(kt * tk, tk)
        dst[...] = jnp.dot(k_scr[pl.ds(start, tk), :], qt, preferred_element_type=F32)

    sc = jnp.dot(kc_ref[...].astype(BF16), qt, preferred_element_type=F32)
    scores(ks_scr, 0, sa_scr)
    n_idx = lax.broadcasted_iota(jnp.int32, (slots, tq), 0)
    t_idx = t0 + lax.broadcasted_iota(jnp.int32, (slots, tq), 1)
    valid = per_head(jnp.where(n_idx * CMP_STRIDE + (CMP_BLOCK - 1) <= t_idx, 1.0, 0.0)) > 0.5
    sc = jnp.where(valid, sc, NEG_INF)
    mx = jnp.max(sc, axis=0, keepdims=True)
    p = jnp.where(valid, jnp.exp2(sc - mx), 0.0)
    den = jnp.sum(p, axis=0, keepdims=True)
    p_c = p * (1.0 / jnp.where(den > 0.0, den, 1.0))
    o_cmp = jnp.dot(vct_ref[...].astype(BF16), p_c.astype(BF16), preferred_element_type=F32)

    p_sum = p_c[:, 0:tq]
    for h in range(1, HP):
        p_sum = p_sum + p_c[:, h * tq:(h + 1) * tq]
    per_sel = SEL_BLOCK // CMP_STRIDE
    for c in range(tq // LANES):
        ps_scr[c] = p_sum[:, c * LANES:(c + 1) * LANES]
    every = [jnp.concatenate([ps_scr[c, pl.ds(r, n_sel, stride=per_sel), :] for c in range(tq // LANES)], axis=1)
             for r in range(per_sel)]
    j_idx = lax.broadcasted_iota(jnp.int32, (n_sel, tq), 0)
    before = jnp.where(j_idx >= 1, pltpu.roll(every[per_sel - 1], 1, axis=0), 0.0)
    imp = every[0]
    for r in range(1, per_sel - 1):
        imp = imp + every[r]
    imp = imp + 0.5 * every[per_sel - 1] + 0.5 * before
    cur = (t0 + lax.broadcasted_iota(jnp.int32, (n_sel, tq), 1)) // SEL_BLOCK
    forced = (j_idx == 0) | (j_idx == cur) | (j_idx == cur - 1)
    imp = jnp.where(forced, FORCE_SCORE, jnp.where(j_idx > cur, NEG_INF, imp))
    rank = jnp.zeros((n_sel, tq), F32)
    for i in range(n_sel):
        ci = imp[i:i + 1, :]
        ge = jnp.where(ci >= imp, 1.0, 0.0)
        gt = jnp.where(ci > imp, 1.0, 0.0)
        rank = rank + jnp.where(j_idx > i, ge, gt)
    sel_scr[...] = jnp.where(rank < float(min(SEL_TOP_N, n_sel)), 1.0, 0.0)

    per_tile = tk // SEL_BLOCK
    kloc = lax.broadcasted_iota(jnp.int32, (tk, tq), 0)
    tloc = lax.broadcasted_iota(jnp.int32, (tk, tq), 1)
    key_ahead = kloc - tloc

    def slc_bias(kt):
        hit = jnp.concatenate(
            [jnp.broadcast_to(sel_scr[pl.ds(kt * per_tile + jj, 1), :], (SEL_BLOCK, tq)) for jj in range(per_tile)],
            axis=0)
        return jnp.where((hit > 0.5) & (key_ahead <= (qi - kt) * tk), 0.0, NEG_INF)

    def absorb(v_ext, src, bias):
        s = src[...] if bias is None else src[...] + per_head(bias)
        m_old = m_scr[...]
        m_new = jnp.maximum(m_old, jnp.max(s, axis=0, keepdims=True))
        alpha = jnp.exp2(m_old - m_new)
        pr = jnp.exp2(s - m_new).astype(BF16)
        m_scr[...] = m_new
        acc_scr[...] = alpha * acc_scr[...] + jnp.dot(v_ext, pr, preferred_element_type=F32)

    def reset():
        m_scr[...] = jnp.full((1, M), NEG_INF, F32)
        acc_scr[...] = jnp.zeros((v_rows, M), F32)

    def result():
        acc = acc_scr[...]
        return acc[0:DH, :] * (1.0 / acc[DH:DH + 1, :])

    reset()
    n_tiles = qi + 1

    def slc_scores(kt, dst):
        scores(ks_scr, jnp.minimum(kt, qi), dst)

    def slc_pair(j, carry):
        slc_scores(2 * j + 1, sb_scr)
        absorb(vs_scr[2 * j], sa_scr, slc_bias(2 * j))
        slc_scores(2 * j + 2, sa_scr)
        absorb(vs_scr[2 * j + 1], sb_scr, slc_bias(2 * j + 1))
        return carry

    lax.fori_loop(0, n_tiles // 2, slc_pair, 0)

    @pl.when(n_tiles % 2 == 1)
    def _():
        absorb(vs_scr[qi], sa_scr, slc_bias(qi))

    o_slc = result()

    reset()
    n_back = WINDOW // tk
    diag_bias = jnp.where(kloc <= tloc, 0.0, NEG_INF)
    far_bias = jnp.where(kloc > tloc, 0.0, NEG_INF)

    def win_bias(d):
        return diag_bias if d == 0 else (far_bias if d == n_back else None)

    def window(n_win):
        bufs = (sa_scr, sb_scr)
        scores(kw_scr, qi, bufs[0])
        for d in range(n_win):
            if d + 1 < n_win:
                scores(kw_scr, qi - (d + 1), bufs[(d + 1) % 2])
            absorb(vw_scr[qi - d], bufs[d % 2], win_bias(d))

    for n_win in range(1, n_back + 2):
        @pl.when((qi == n_win - 1) if n_win <= n_back else (qi >= n_back))
        def _():
            window(n_win)

    o_win = result()

    gt_scr[...] = gate_ref[...].astype(F32).T

    def gate(branch):
        rows = [gt_scr[pl.ds(g_idx * (HP * N_BRANCH) + h * N_BRANCH + branch, 1), :] for h in range(HP)]
        return _sigmoid(jnp.concatenate(rows, axis=1))

    o = gate(0) * o_cmp + gate(1) * o_slc + gate(2) * o_win
    o_ref[...] = jnp.concatenate([o[:, h * tq:(h + 1) * tq].T for h in range(HP)], axis=1).astype(o_ref.dtype)


def _nsa(z, kc, vct, *, B, S, tq=256):
    G, HP, DH = NSA_KV_GROUPS, NSA_HPG, NSA_HEAD_DIM
    T = B * S
    nq = S // tq
    slots = S // CMP_STRIDE
    n_sel = S // SEL_BLOCK
    n_kt = S // tq
    v_rows = DH + BF16_SUBLANES
    KVW = NSA_KV_WIDTH

    def kv_spec(which):
        return pl.BlockSpec((S, KVW), lambda b, g, i: (b, OFF_KV // KVW + which))

    return pl.pallas_call(
        functools.partial(_nsa_kernel, tq=tq, S=S),
        grid=(B, G, nq),
        in_specs=[
            pl.BlockSpec((tq, HP * DH), lambda b, g, i: (b * nq + i, OFF_Q // (HP * DH) + g)),
            pl.BlockSpec((slots, DH), lambda b, g, i: (b * G + g, 0)),
            pl.BlockSpec((DH, slots), lambda b, g, i: (0, b * G + g)),
            kv_spec(0), kv_spec(1), kv_spec(2), kv_spec(3),
            pl.BlockSpec((tq, LANES), lambda b, g, i: (b * nq + i, OFF_NG // LANES)),
        ],
        out_specs=pl.BlockSpec((tq, HP * DH), lambda b, g, i: (b * nq + i, g)),
        out_shape=jax.ShapeDtypeStruct((T, NSA_Q_WIDTH), BF16),
        scratch_shapes=[
            pltpu.VMEM((S, DH), BF16),
            pltpu.VMEM((S, DH), BF16),
            pltpu.VMEM((n_kt, v_rows, tq), BF16),
            pltpu.VMEM((n_kt, v_rows, tq), BF16),
            pltpu.VMEM((LANES, tq), F32),
            pltpu.VMEM((tq // LANES, slots, LANES), F32),
            pltpu.VMEM((n_sel, tq), F32),
            pltpu.VMEM((tq, HP * tq), F32),
            pltpu.VMEM((tq, HP * tq), F32),
            pltpu.VMEM((1, HP * tq), F32),
            pltpu.VMEM((v_rows, HP * tq), F32),
        ],
        compiler_params=_cparams(("parallel", "parallel", "arbitrary")),
        name="nsa_attention",
    )(z, kc, vct, z, z, z, z, z)


def _merge_kernel(ya_ref, yb_ref, yc_ref, mga_ref, mgb_ref, mgc_ref, x_ref, wa_ref, wb_ref, wc_ref, wo_ref,
                  gpost_ref, gpre_ref, x1_ref, hf_ref):
    gate = lambda ref: _sigmoid(ref[...].astype(F32))
    merged = gate(mga_ref) * jnp.dot(ya_ref[...], wa_ref[...], preferred_element_type=F32)
    merged = merged + gate(mgb_ref) * jnp.dot(yb_ref[...], wb_ref[...], preferred_element_type=F32)
    merged = merged + gate(mgc_ref) * jnp.dot(yc_ref[...], wc_ref[...], preferred_element_type=F32)
    y = jnp.dot(merged.astype(BF16), wo_ref[...], preferred_element_type=F32)
    x1 = x_ref[...] + _rms(y, gpost_ref[...])
    x1_ref[...] = x1
    hf_ref[...] = _rms(x1, gpre_ref[...]).astype(BF16)


def _merge(ya, yb, yc, z, x2d, wa, wb, wc, wo, gpost, gpre, *, tm=512):
    T, D = x2d.shape
    row = lambda c: pl.BlockSpec((tm, D), lambda i: (i, c))
    wfull = pl.BlockSpec((D, D), lambda i: (0, 0))
    vec = pl.BlockSpec((1, D), lambda i: (0, 0))
    mg0 = OFF_MG // D
    return pl.pallas_call(
        _merge_kernel,
        grid=(T // tm,),
        in_specs=[row(0), row(0), row(0), row(mg0), row(mg0 + 1), row(mg0 + 2), row(0),
                  wfull, wfull, wfull, wfull, vec, vec],
        out_specs=[row(0), row(0)],
        out_shape=[jax.ShapeDtypeStruct((T, D), F32), jax.ShapeDtypeStruct((T, D), BF16)],
        compiler_params=_cparams(("parallel",)),
        name="merge",
    )(ya, yb, yc, z, z, z, x2d, wa, wb, wc, wo, gpost, gpre)


def _ffn_kernel(hf_ref, x1_ref, win_ref, wout_ref, gpost_ref, o_ref, acc_scr, *, bounds):
    hf = hf_ref[...]
    for c, (lo, hi) in enumerate(zip(bounds[:-1], bounds[1:])):
        gate = jnp.dot(hf, win_ref[:, lo:hi], preferred_element_type=F32)
        up = jnp.dot(hf, win_ref[:, D_FF + lo:D_FF + hi], preferred_element_type=F32)
        act = (gate * _sigmoid(gate) * up).astype(BF16)
        part = jnp.dot(act, wout_ref[lo:hi, :], preferred_element_type=F32)
        if c == 0:
            acc_scr[...] = part
        else:
            acc_scr[...] += part
    o_ref[...] = x1_ref[...] + _rms(acc_scr[...], gpost_ref[...])


def _ffn(hf, x1, win, wout, gpost, *, tm=512):
    T, D = x1.shape
    blocks = D_FF // V7X_MXU_WIDTH
    bounds = (0, (blocks + 1) // 2 * V7X_MXU_WIDTH, D_FF)
    row = pl.BlockSpec((tm, D), lambda i: (i, 0))
    return pl.pallas_call(
        functools.partial(_ffn_kernel, bounds=bounds),
        grid=(T // tm,),
        in_specs=[row, row,
                  pl.BlockSpec(win.shape, lambda i: (0, 0), pipeline_mode=pl.Buffered(1)),
                  pl.BlockSpec(wout.shape, lambda i: (0, 0), pipeline_mode=pl.Buffered(1)),
                  pl.BlockSpec((1, D), lambda i: (0, 0))],
        out_specs=row,
        out_shape=jax.ShapeDtypeStruct((T, D), F32),
        scratch_shapes=[pltpu.VMEM((tm, D), F32)],
        compiler_params=_cparams(("parallel",)),
        name="ffn",
    )(hf, x1, win, wout, gpost)


def _reorder_w_in(w_in):
    s = np.cumsum([0, GMLP_WIDTH, GMLP_WIDTH, NSA_Q_WIDTH, 6 * NSA_KV_WIDTH, N_BRANCH * NSA_HEADS,
                   RNN_WIDTH, RNN_WIDTH, N_BRANCH * D_MODEL])
    u, v, q, kv, ng, xr, rg, mg = (w_in[:, s[i]:s[i + 1]] for i in range(8))
    kv_cmp, kv_rest = kv[:, :2 * NSA_KV_WIDTH], kv[:, 2 * NSA_KV_WIDTH:]
    q = q * Q_SCALE
    pad = jnp.zeros((w_in.shape[0], NG_PAD - N_BRANCH * NSA_HEADS), w_in.dtype)
    return jnp.concatenate([u, v, xr, rg, mg, kv_cmp, ng, pad, q, kv_rest], axis=1).astype(BF16)


def _block_diag_gates(wa, wx):
    per = RNN_BLOCK // RNN_HEAD_DIM
    nblk = RNN_HEADS // per
    eye = jnp.eye(per, dtype=wa.dtype)

    def bd(w):
        w = w.reshape(nblk, per, RNN_HEAD_DIM, RNN_HEAD_DIM)
        return jnp.einsum('kpio,pq->kpiqo', w, eye).reshape(nblk, RNN_BLOCK, RNN_BLOCK)

    return jnp.concatenate([bd(wa), bd(wx)], axis=-1).astype(BF16)


def _layer(x2d, B, S, g_pre_mix, g_post_mix, g_pre_ffn, g_post_ffn, w_in,
           gmlp_ln_g, gmlp_ln_b, gmlp_ws, gmlp_bs,
           nsa_pe_k, nsa_pe_v, nsa_wk1, nsa_wk2, nsa_wv1, nsa_wv2,
           rnn_conv_w, rnn_conv_b, rnn_wa, rnn_ba, rnn_wx, rnn_bx, rnn_lam,
           w_br_a, w_br_b, w_br_c, w_o, w_ffn_in, w_ffn_out):
    row = lambda a: a.reshape(1, -1)
    z = _in_proj(x2d, row(g_pre_mix), _reorder_w_in(w_in))

    y_a = _gmlp(z, row(gmlp_ln_g), row(gmlp_ln_b), gmlp_ws, gmlp_bs.T)
    y_c = _rglru(z, rnn_conv_w, row(rnn_conv_b), _block_diag_gates(rnn_wa, rnn_wx),
                 row(rnn_ba), row(rnn_bx), row(rnn_lam), B=B, S=S)

    kc, vct = _compress(z, nsa_pe_k.reshape(1, -1), nsa_pe_v.reshape(1, -1), nsa_wk1.astype(BF16),
                        nsa_wk2.astype(BF16), nsa_wv1.astype(BF16), nsa_wv2.T.astype(BF16), B=B, S=S)
    y_b = _nsa(z, kc, vct, B=B, S=S)

    x1, hf = _merge(y_a, y_b, y_c, z, x2d, w_br_a.astype(BF16), w_br_b.astype(BF16), w_br_c.astype(BF16),
                    w_o.astype(BF16), row(g_post_mix), row(g_pre_ffn))
    return _ffn(hf, x1, w_ffn_in.astype(BF16), w_ffn_out.astype(BF16), row(g_post_ffn))


def kernel(x, g_pre_mix, g_post_mix, g_pre_ffn, g_post_ffn, w_in, gmlp_ln_g, gmlp_ln_b, gmlp_ws, gmlp_bs, nsa_pe_k, nsa_pe_v, nsa_wk1, nsa_wk2, nsa_wv1, nsa_wv2, rnn_conv_w, rnn_conv_b, rnn_wa, rnn_ba, rnn_wx, rnn_bx, rnn_lam, w_br_a, w_br_b, w_br_c, w_o, w_ffn_in, w_ffn_out):
    B, S, D = x.shape
    params = (g_pre_mix, g_post_mix, g_pre_ffn, g_post_ffn, w_in, gmlp_ln_g, gmlp_ln_b, gmlp_ws, gmlp_bs,
              nsa_pe_k, nsa_pe_v, nsa_wk1, nsa_wk2, nsa_wv1, nsa_wv2,
              rnn_conv_w, rnn_conv_b, rnn_wa, rnn_ba, rnn_wx, rnn_bx, rnn_lam,
              w_br_a, w_br_b, w_br_c, w_o, w_ffn_in, w_ffn_out)
    x2d = x.reshape(B * S, D)
    for l in range(w_in.shape[0]):
        x2d = _layer(x2d, B, S, *(p[l] for p in params))
    return x2d.reshape(B, S, D)
```

```python
import functools

import jax
import jax.numpy as jnp
import numpy as np
from jax import lax
from jax.experimental import pallas as pl
from jax.experimental.pallas import tpu as pltpu

F32 = jnp.float32
BF16 = jnp.bfloat16

EPS = 1e-6
NEG_INF = -1e30
FORCE_SCORE = 1e4

D_MODEL = 1024
GMLP_WIDTH = 1024
GMLP_GROUPS = 4
GMLP_GROUP_DIM = GMLP_WIDTH // GMLP_GROUPS
GMLP_CHUNK = 128

NSA_HEADS = 16
NSA_KV_GROUPS = 4
NSA_HEAD_DIM = 64
NSA_HPG = NSA_HEADS // NSA_KV_GROUPS
NSA_Q_WIDTH = NSA_HEADS * NSA_HEAD_DIM
NSA_KV_WIDTH = NSA_KV_GROUPS * NSA_HEAD_DIM
N_BRANCH = 3
CMP_BLOCK = 32
CMP_STRIDE = 16
CMP_HIDDEN = 256
SEL_BLOCK = 64
SEL_TOP_N = 16
WINDOW = 512
Q_SCALE = NSA_HEAD_DIM ** -0.5 * float(np.log2(np.e))

RNN_WIDTH = 1024
RNN_HEADS = 16
RNN_HEAD_DIM = RNN_WIDTH // RNN_HEADS
CONV_WIDTH = 4
LRU_C = 8.0
RNN_BLOCK = 256

D_FF = 2816

OFF_U = 0
OFF_V = 1024
OFF_XR = 2048
OFF_RG = 3072
OFF_MG = 4096
OFF_KC = 7168
OFF_NG = OFF_KC + 2 * NSA_KV_WIDTH
NG_PAD = 512
OFF_Q = OFF_NG + NG_PAD
OFF_KV = OFF_Q + NSA_Q_WIDTH
D_IN_PAD = OFF_KV + 4 * NSA_KV_WIDTH

LANES = 128
V7X_MXU_WIDTH = 256
BF16_SUBLANES = 16
V7X_VMEM_LIMIT = 56 * 1024 * 1024


def _cparams(sem, vmem=V7X_VMEM_LIMIT):
    return pltpu.CompilerParams(dimension_semantics=sem, vmem_limit_bytes=vmem)


def _rms(x, g):
    ms = jnp.mean(x * x, axis=-1, keepdims=True)
    return x * lax.rsqrt(ms + EPS) * g


def _gelu(x):
    return jax.nn.gelu(x)


def _sigmoid(x):
    return jax.nn.sigmoid(x)


_NT = (((1,), (1,)), ((), ()))


def _in_proj_kernel(x_ref, g_ref, w_ref, o_ref, h_scr):
    @pl.when(pl.program_id(1) == 0)
    def _():
        h_scr[...] = _rms(x_ref[...], g_ref[...]).astype(BF16)

    o_ref[...] = jnp.dot(h_scr[...], w_ref[...], preferred_element_type=F32).astype(o_ref.dtype)


def _in_proj(x2d, g, w, *, tm=1024, tn=2048):
    T, D = x2d.shape
    N = w.shape[1]
    return pl.pallas_call(
        _in_proj_kernel,
        grid=(T // tm, N // tn),
        in_specs=[
            pl.BlockSpec((tm, D), lambda i, j: (i, 0)),
            pl.BlockSpec((1, D), lambda i, j: (0, 0)),
            pl.BlockSpec((D, tn), lambda i, j: (0, j)),
        ],
        out_specs=pl.BlockSpec((tm, tn), lambda i, j: (i, j)),
        out_shape=jax.ShapeDtypeStruct((T, N), BF16),
        scratch_shapes=[pltpu.VMEM((tm, D), BF16)],
        compiler_params=_cparams(("parallel", "arbitrary")),
        name="in_proj",
    )(x2d, g, w)


def _gmlp_kernel(u_ref, v_ref, lng_ref, lnb_ref, ws_ref, bst_ref, o_ref, *, n_chunks):
    C = GMLP_CHUNK
    row = lax.broadcasted_iota(jnp.int32, (C, C), 0)
    col = lax.broadcasted_iota(jnp.int32, (C, C), 1)
    causal = col <= row
    ws = [jnp.where(causal, ws_ref[g], 0.0).astype(BF16) for g in range(GMLP_GROUPS)]
    bst = bst_ref[...]
    for c in range(n_chunks):
        rows = slice(c * C, (c + 1) * C)
        gv = _gelu(v_ref[rows, :].astype(F32))
        mu = jnp.mean(gv, axis=-1, keepdims=True)
        d = gv - mu
        var = jnp.mean(d * d, axis=-1, keepdims=True)
        vn = (d * lax.rsqrt(var + EPS) * lng_ref[...] + lnb_ref[...]).astype(BF16)
        for g in range(GMLP_GROUPS):
            cols = slice(g * GMLP_GROUP_DIM, (g + 1) * GMLP_GROUP_DIM)
            mixed = jnp.dot(ws[g], vn[:, cols], preferred_element_type=F32) + bst[:, g:g + 1]
            o_ref[rows, cols] = (_gelu(u_ref[rows, cols].astype(F32)) * mixed).astype(o_ref.dtype)


def _gmlp(z, lng, lnb, ws, bst, *, n_chunks=8):
    T = z.shape[0]
    tm = GMLP_CHUNK * n_chunks
    W = GMLP_WIDTH
    return pl.pallas_call(
        functools.partial(_gmlp_kernel, n_chunks=n_chunks),
        grid=(T // tm,),
        in_specs=[
            pl.BlockSpec((tm, W), lambda i: (i, OFF_U // W)),
            pl.BlockSpec((tm, W), lambda i: (i, OFF_V // W)),
            pl.BlockSpec((1, W), lambda i: (0, 0)),
            pl.BlockSpec((1, W), lambda i: (0, 0)),
            pl.BlockSpec((GMLP_GROUPS, GMLP_CHUNK, GMLP_CHUNK), lambda i: (0, 0, 0)),
            pl.BlockSpec((GMLP_CHUNK, GMLP_GROUPS), lambda i: (0, 0)),
        ],
        out_specs=pl.BlockSpec((tm, W), lambda i: (i, 0)),
        out_shape=jax.ShapeDtypeStruct((T, W), BF16),
        compiler_params=_cparams(("parallel",)),
        name="gmlp",
    )(z, z, lng, lnb, ws, bst)


def _rglru_kernel(xr_ref, rg_ref, cw_ref, cb_ref, wab_ref, ba_ref, bx_ref, lam_ref, o_ref,
                  ext_scr, a_scr, b_scr, hc_scr, *, tt):
    W = RNN_WIDTH
    groups = tt // 8
    sub_row = lax.broadcasted_iota(jnp.int32, (groups, 8, RNN_BLOCK), 1)

    @pl.when(pl.program_id(1) == 0)
    def _():
        ext_scr[0:8, :] = jnp.zeros((8, W), F32)
        hc_scr[...] = jnp.zeros((8, W), F32)

    xr = xr_ref[...].astype(F32)
    ext_scr[8:8 + tt, :] = xr
    cw = cw_ref[...]
    xc = (cw[3:4] * xr + cw[2:3] * ext_scr[7:7 + tt, :] + cw[1:2] * ext_scr[6:6 + tt, :]
          + cw[0:1] * ext_scr[5:5 + tt, :] + cb_ref[...])
    ext_scr[0:8, :] = xr[tt - 8:tt, :]

    lam = lam_ref[...]
    neg = -lam
    softplus = jnp.maximum(neg, 0.0) + jnp.log1p(jnp.exp(-jnp.abs(neg)))
    for k in range(W // RNN_BLOCK):
        cols = slice(k * RNN_BLOCK, (k + 1) * RNN_BLOCK)
        xck = xc[:, cols]
        gates = jnp.dot(xck.astype(BF16), wab_ref[k], preferred_element_type=F32)
        r = _sigmoid(gates[:, :RNN_BLOCK] + ba_ref[:, cols])
        i = _sigmoid(gates[:, RNN_BLOCK:] + bx_ref[:, cols])
        log_a = -LRU_C * r * softplus[:, cols]
        a = jnp.exp(log_a)
        one_minus_a2 = -jnp.tanh(log_a) * (1.0 + a * a)
        b_in = jnp.sqrt(one_minus_a2) * (i * xck)
        a3 = a.reshape(groups, 8, RNN_BLOCK)
        b3 = b_in.reshape(groups, 8, RNN_BLOCK)
        for step in (1, 2, 4):
            keep = sub_row >= step
            a_prev = jnp.where(keep, pltpu.roll(a3, step, axis=1), 1.0)
            b_prev = jnp.where(keep, pltpu.roll(b3, step, axis=1), 0.0)
            b3 = a3 * b_prev + b3
            a3 = a3 * a_prev
        a_scr[:, cols] = a3.reshape(tt, RNN_BLOCK)
        b_scr[:, cols] = b3.reshape(tt, RNN_BLOCK)

    def carry_step(i, h_prev):
        base = pl.multiple_of(i * BF16_SUBLANES, BF16_SUBLANES)
        gate = _gelu(rg_ref[pl.ds(base, BF16_SUBLANES), :].astype(F32))
        ys = []
        for half in range(BF16_SUBLANES // 8):
            rows = pl.ds(base + half * 8, 8)
            h = b_scr[rows, :] + a_scr[rows, :] * h_prev
            ys.append(h * gate[half * 8:(half + 1) * 8, :])
            h_prev = h[7:8, :]
        o_ref[pl.ds(base, BF16_SUBLANES), :] = jnp.concatenate(ys, axis=0).astype(o_ref.dtype)
        return h_prev

    h_last = lax.fori_loop(0, tt // BF16_SUBLANES, carry_step, hc_scr[0:1, :])
    hc_scr[...] = jnp.broadcast_to(h_last, (8, W))


def _rglru(z, cw, cb, wab, ba, bx, lam, *, B, S, tt=512):
    T = z.shape[0]
    W = RNN_WIDTH
    nt = S // tt
    return pl.pallas_call(
        functools.partial(_rglru_kernel, tt=tt),
        grid=(B, nt),
        in_specs=[
            pl.BlockSpec((tt, W), lambda b, t: (b * nt + t, OFF_XR // W)),
            pl.BlockSpec((tt, W), lambda b, t: (b * nt + t, OFF_RG // W)),
            pl.BlockSpec((CONV_WIDTH, W), lambda b, t: (0, 0)),
            pl.BlockSpec((1, W), lambda b, t: (0, 0)),
            pl.BlockSpec((W // RNN_BLOCK, RNN_BLOCK, 2 * RNN_BLOCK), lambda b, t: (0, 0, 0)),
            pl.BlockSpec((1, W), lambda b, t: (0, 0)),
            pl.BlockSpec((1, W), lambda b, t: (0, 0)),
            pl.BlockSpec((1, W), lambda b, t: (0, 0)),
        ],
        out_specs=pl.BlockSpec((tt, W), lambda b, t: (b * nt + t, 0)),
        out_shape=jax.ShapeDtypeStruct((T, W), BF16),
        scratch_shapes=[
            pltpu.VMEM((tt + 8, W), F32),
            pltpu.VMEM((tt, W), F32),
            pltpu.VMEM((tt, W), F32),
            pltpu.VMEM((8, W), F32),
        ],
        compiler_params=_cparams(("parallel", "arbitrary")),
        name="rglru",
    )(z, z, cw, cb, wab, ba, bx, lam)


def _compress_hidden(x_ref, pe_ref, w1_ref, x_scr):
    DH = NSA_HEAD_DIM
    half = CMP_STRIDE * DH
    slots = x_ref.shape[0] // CMP_STRIDE
    x = x_ref[...].astype(F32)
    for c in range(NSA_KV_WIDTH // LANES):
        x_scr[c] = x[:, c * LANES:(c + 1) * LANES]
    phase = [[x_scr[c, pl.ds(l, slots, stride=CMP_STRIDE), :] for c in range(NSA_KV_WIDTH // LANES)]
             for l in range(CMP_STRIDE)]
    per_block = LANES // DH

    def group_rows(g):
        lanes = slice((g % per_block) * DH, (g % per_block + 1) * DH)
        return jnp.concatenate([phase[l][g // per_block][:, lanes] for l in range(CMP_STRIDE)], axis=1)

    hm = jnp.concatenate([group_rows(g) for g in range(NSA_KV_GROUPS)], axis=0).astype(BF16)
    rows = hm.shape[0]
    lo = jnp.dot(hm, w1_ref[0:half, :], preferred_element_type=F32)
    hi = jnp.dot(hm, w1_ref[half:2 * half, :], preferred_element_type=F32)
    pe = jnp.broadcast_to(pe_ref[...], (8, 2 * half)).astype(BF16)
    pe_term = jnp.dot(pe, w1_ref[...], preferred_element_type=F32)[0:1, :]
    return _gelu(lo + pltpu.roll(hi, rows - 1, axis=0) + pe_term).astype(BF16)


def _compress_kernel(xk_ref, xv_ref, pek_ref, pev_ref, wk1_ref, wk2_ref, wv1_ref, wv2t_ref, kc_ref, vct_ref, x_scr):
    kc_ref[...] = jnp.dot(_compress_hidden(xk_ref, pek_ref, wk1_ref, x_scr), wk2_ref[...],
                          preferred_element_type=F32)
    vct_ref[...] = lax.dot_general(wv2t_ref[...], _compress_hidden(xv_ref, pev_ref, wv1_ref, x_scr), _NT,
                                   preferred_element_type=F32)


def _compress(z, pek, pev, wk1, wk2, wv1, wv2t, *, B, S):
    G, DH = NSA_KV_GROUPS, NSA_HEAD_DIM
    slots = S // CMP_STRIDE
    KVW = NSA_KV_WIDTH
    full = lambda shape: pl.BlockSpec(shape, lambda b: (0,) * len(shape))
    return pl.pallas_call(
        _compress_kernel,
        grid=(B,),
        in_specs=[pl.BlockSpec((S, KVW), lambda b: (b, OFF_KC // KVW)),
                  pl.BlockSpec((S, KVW), lambda b: (b, OFF_KC // KVW + 1)),
                  full(pek.shape), full(pev.shape), full(wk1.shape), full(wk2.shape), full(wv1.shape),
                  full(wv2t.shape)],
        out_specs=[pl.BlockSpec((G * slots, DH), lambda b: (b, 0)), pl.BlockSpec((DH, G * slots), lambda b: (0, b))],
        out_shape=[jax.ShapeDtypeStruct((B * G * slots, DH), F32), jax.ShapeDtypeStruct((DH, B * G * slots), F32)],
        scratch_shapes=[pltpu.VMEM((KVW // LANES, S, LANES), F32)],
        compiler_params=_cparams(("parallel",)),
        name="nsa_compress",
    )(z, z, pek, pev, wk1, wk2, wv1, wv2t)


def _nsa_kernel(q_ref, kc_ref, vct_ref, ks_ref, vs_ref, kw_ref, vw_ref, gate_ref, o_ref,
                ks_scr, kw_scr, vs_scr, vw_scr, *tile_scratch, tq, S):
    DH = NSA_HEAD_DIM
    tk = tq
    v_rows = DH + BF16_SUBLANES
    g_idx = pl.program_id(1)

    for g in range(NSA_KV_GROUPS):
        @pl.when(g_idx == g)
        def _():
            lanes = slice(g * DH, (g + 1) * DH)
            ks_scr[...] = ks_ref[:, lanes]
            kw_scr[...] = kw_ref[:, lanes]
            ones = jnp.ones((BF16_SUBLANES, tk), BF16)
            for v_ref, v_scr in ((vs_ref, vs_scr), (vw_ref, vw_scr)):
                vt = v_ref[:, lanes].T
                for kt in range(S // tk):
                    v_scr[kt, 0:DH, :] = vt[:, kt * tk:(kt + 1) * tk]
                    v_scr[kt, DH:v_rows, :] = ones

    def q_tile(qi, carry):
        rows = pl.ds(pl.multiple_of(qi * tq, tq), tq)
        _nsa_q_tile(qi, g_idx, q_ref.at[rows, :], kc_ref, vct_ref, gate_ref.at[rows, :], o_ref.at[rows, :],
                    ks_scr, kw_scr, vs_scr, vw_scr, *tile_scratch, tq=tq, S=S)
        return carry

    lax.fori_loop(0, S // tq, q_tile, 0)


def _nsa_q_tile(qi, g_idx, q_ref, kc_ref, vct_ref, gate_ref, o_ref,
                ks_scr, kw_scr, vs_scr, vw_scr, gt_scr, ps_scr, sel_scr, sa_scr, sb_scr, m_scr, acc_scr, *, tq, S):
    HP, DH = NSA_HPG, NSA_HEAD_DIM
    M = HP * tq
    tk = tq
    n_sel = S // SEL_BLOCK
    slots = kc_ref.shape[0]
    v_rows = DH + BF16_SUBLANES
    t0 = qi * tq

    def per_head(x):
        return jnp.concatenate([x] * HP, axis=1)

    q_t = q_ref[...].T
    qt = jnp.concatenate([q_t[h * DH:(h + 1) * DH, :] for h in range(HP)], axis=1)

    def scores(k_scr, kt, dst):
        start = pl.multiple_of(kt * tk, tk)
        dst[...] = jnp.dot(k_scr[pl.ds(start, tk), :], qt, preferred_element_type=F32)

    sc = jnp.dot(kc_ref[...].astype(BF16), qt, preferred_element_type=F32)
    scores(ks_scr, 0, sa_scr)
    n_idx = lax.broadcasted_iota(jnp.int32, (slots, tq), 0)
    t_idx = t0 + lax.broadcasted_iota(jnp.int32, (slots, tq), 1)
    valid = per_head(jnp.where(n_idx * CMP_STRIDE + (CMP_BLOCK - 1) <= t_idx, 1.0, 0.0)) > 0.5
    sc = jnp.where(valid, sc, NEG_INF)
    mx = jnp.max(sc, axis=0, keepdims=True)
    p = jnp.where(valid, jnp.exp2(sc - mx), 0.0)
    den = jnp.sum(p, axis=0, keepdims=True)
    p_c = p * (1.0 / jnp.where(den > 0.0, den, 1.0))
    o_cmp = jnp.dot(vct_ref[...].astype(BF16), p_c.astype(BF16), preferred_element_type=F32)

    p_sum = p_c[:, 0:tq]
    for h in range(1, HP):
        p_sum = p_sum + p_c[:, h * tq:(h + 1) * tq]
    per_sel = SEL_BLOCK // CMP_STRIDE
    for c in range(tq // LANES):
        ps_scr[c] = p_sum[:, c * LANES:(c + 1) * LANES]
    every = [jnp.concatenate([ps_scr[c, pl.ds(r, n_sel, stride=per_sel), :] for c in range(tq // LANES)], axis=1)
             for r in range(per_sel)]
    j_idx = lax.broadcasted_iota(jnp.int32, (n_sel, tq), 0)
    before = jnp.where(j_idx >= 1, pltpu.roll(every[per_sel - 1], 1, axis=0), 0.0)
    imp = every[0]
    for r in range(1, per_sel - 1):
        imp = imp + every[r]
    imp = imp + 0.5 * every[per_sel - 1] + 0.5 * before
    cur = (t0 + lax.broadcasted_iota(jnp.int32, (n_sel, tq), 1)) // SEL_BLOCK
    forced = (j_idx == 0) | (j_idx == cur) | (j_idx == cur - 1)
    imp = jnp.where(forced, FORCE_SCORE, jnp.where(j_idx > cur, NEG_INF, imp))
    rank = jnp.zeros((n_sel, tq), F32)
    for i in range(n_sel):
        ci = imp[i:i + 1, :]
        ge = jnp.where(ci >= imp, 1.0, 0.0)
        gt = jnp.where(ci > imp, 1.0, 0.0)
        rank = rank + jnp.where(j_idx > i, ge, gt)
    sel_scr[...] = jnp.where(rank < float(min(SEL_TOP_N, n_sel)), 1.0, 0.0)

    per_tile = tk // SEL_BLOCK
    kloc = lax.broadcasted_iota(jnp.int32, (tk, tq), 0)
    tloc = lax.broadcasted_iota(jnp.int32, (tk, tq), 1)
    key_ahead = kloc - tloc

    def slc_bias(kt):
        hit = jnp.concatenate(
            [jnp.broadcast_to(sel_scr[pl.ds(kt * per_tile + jj, 1), :], (SEL_BLOCK, tq)) for jj in range(per_tile)],
            axis=0)
        return jnp.where((hit > 0.5) & (key_ahead <= (qi - kt) * tk), 0.0, NEG_INF)

    def absorb(v_ext, src, bias):
        s = src[...] if bias is None else src[...] + per_head(bias)
        m_old = m_scr[...]
        m_new = jnp.maximum(m_old, jnp.max(s, axis=0, keepdims=True))
        alpha = jnp.exp2(m_old - m_new)
        pr = jnp.exp2(s - m_new).astype(BF16)
        m_scr[...] = m_new
        acc_scr[...] = alpha * acc_scr[...] + jnp.dot(v_ext, pr, preferred_element_type=F32)

    def reset():
        m_scr[...] = jnp.full((1, M), NEG_INF, F32)
        acc_scr[...] = jnp.zeros((v_rows, M), F32)

    def result():
        acc = acc_scr[...]
        return acc[0:DH, :] * (1.0 / acc[DH:DH + 1, :])

    reset()
    n_tiles = qi + 1

    def slc_scores(kt, dst):
        scores(ks_scr, jnp.minimum(kt, qi), dst)

    def slc_pair(j, carry):
        slc_scores(2 * j + 1, sb_scr)
        absorb(vs_scr[2 * j], sa_scr, slc_bias(2 * j))
        slc_scores(2 * j + 2, sa_scr)
        absorb(vs_scr[2 * j + 1], sb_scr, slc_bias(2 * j + 1))
        return carry

    lax.fori_loop(0, n_tiles // 2, slc_pair, 0)

    @pl.when(n_tiles % 2 == 1)
    def _():
        absorb(vs_scr[qi], sa_scr, slc_bias(qi))

    o_slc = result()

    reset()
    n_back = WINDOW // tk
    diag_bias = jnp.where(kloc <= tloc, 0.0, NEG_INF)
    far_bias = jnp.where(kloc > tloc, 0.0, NEG_INF)

    def win_bias(d):
        return diag_bias if d == 0 else (far_bias if d == n_back else None)

    def window(n_win):
        bufs = (sa_scr, sb_scr)
        scores(kw_scr, qi, bufs[0])
        for d in range(n_win):
            if d + 1 < n_win:
                scores(kw_scr, qi - (d + 1), bufs[(d + 1) % 2])
            absorb(vw_scr[qi - d], bufs[d % 2], win_bias(d))

    for n_win in range(1, n_back + 2):
        @pl.when((qi == n_win - 1) if n_win <= n_back else (qi >= n_back))
        def _():
            window(n_win)

    o_win = result()

    gt_scr[...] = gate_ref[...].astype(F32).T

    def gate(branch):
        rows = [gt_scr[pl.ds(g_idx * (HP * N_BRANCH) + h * N_BRANCH + branch, 1), :] for h in range(HP)]
        return _sigmoid(jnp.concatenate(rows, axis=1))

    o = gate(0) * o_cmp + gate(1) * o_slc + gate(2) * o_win
    o_ref[...] = jnp.concatenate([o[:, h * tq:(h + 1) * tq].T for h in range(HP)], axis=1).astype(o_ref.dtype)


def _nsa(z, kc, vct, *, B, S, tq=256):
    G, HP, DH = NSA_KV_GROUPS, NSA_HPG, NSA_HEAD_DIM
    T = B * S
    slots = S // CMP_STRIDE
    n_sel = S // SEL_BLOCK
    n_kt = S // tq
    v_rows = DH + BF16_SUBLANES
    KVW = NSA_KV_WIDTH

    def kv_spec(which):
        return pl.BlockSpec((S, KVW), lambda b, g: (b, OFF_KV // KVW + which))

    return pl.pallas_call(
        functools.partial(_nsa_kernel, tq=tq, S=S),
        grid=(B, G),
        in_specs=[
            pl.BlockSpec((S, HP * DH), lambda b, g: (b, OFF_Q // (HP * DH) + g)),
            pl.BlockSpec((slots, DH), lambda b, g: (b * G + g, 0)),
            pl.BlockSpec((DH, slots), lambda b, g: (0, b * G + g)),
            kv_spec(0), kv_spec(1), kv_spec(2), kv_spec(3),
            pl.BlockSpec((S, LANES), lambda b, g: (b, OFF_NG // LANES)),
        ],
        out_specs=pl.BlockSpec((S, HP * DH), lambda b, g: (b, g)),
        out_shape=jax.ShapeDtypeStruct((T, NSA_Q_WIDTH), BF16),
        scratch_shapes=[
            pltpu.VMEM((S, DH), BF16),
            pltpu.VMEM((S, DH), BF16),
            pltpu.VMEM((n_kt, v_rows, tq), BF16),
            pltpu.VMEM((n_kt, v_rows, tq), BF16),
            pltpu.VMEM((LANES, tq), F32),
            pltpu.VMEM((tq // LANES, slots, LANES), F32),
            pltpu.VMEM((n_sel, tq), F32),
            pltpu.VMEM((tq, HP * tq), F32),
            pltpu.VMEM((tq, HP * tq), F32),
            pltpu.VMEM((1, HP * tq), F32),
            pltpu.VMEM((v_rows, HP * tq), F32),
        ],
        compiler_params=_cparams(("parallel", "parallel")),
        name="nsa_attention",
    )(z, kc, vct, z, z, z, z, z)


def _merge_kernel(ya_ref, yb_ref, yc_ref, mga_ref, mgb_ref, mgc_ref, x_ref, wa_ref, wb_ref, wc_ref, wo_ref,
                  gpost_ref, gpre_ref, x1_ref, hf_ref):
    gate = lambda ref: _sigmoid(ref[...].astype(F32))
    merged = gate(mga_ref) * jnp.dot(ya_ref[...], wa_ref[...], preferred_element_type=F32)
    merged = merged + gate(mgb_ref) * jnp.dot(yb_ref[...], wb_ref[...], preferred_element_type=F32)
    merged = merged + gate(mgc_ref) * jnp.dot(yc_ref[...], wc_ref[...], preferred_element_type=F32)
    y = jnp.dot(merged.astype(BF16), wo_ref[...], preferred_element_type=F32)
    x1 = x_ref[...] + _rms(y, gpost_ref[...])
    x1_ref[...] = x1
    hf_ref[...] = _rms(x1, gpre_ref[...]).astype(BF16)


def _merge(ya, yb, yc, z, x2d, wa, wb, wc, wo, gpost, gpre, *, tm=512):
    T, D = x2d.shape
    row = lambda c: pl.BlockSpec((tm, D), lambda i: (i, c))
    wfull = pl.BlockSpec((D, D), lambda i: (0, 0))
    vec = pl.BlockSpec((1, D), lambda i: (0, 0))
    mg0 = OFF_MG // D
    return pl.pallas_call(
        _merge_kernel,
        grid=(T // tm,),
        in_specs=[row(0), row(0), row(0), row(mg0), row(mg0 + 1), row(mg0 + 2), row(0),
                  wfull, wfull, wfull, wfull, vec, vec],
        out_specs=[row(0), row(0)],
        out_shape=[jax.ShapeDtypeStruct((T, D), F32), jax.ShapeDtypeStruct((T, D), BF16)],
        compiler_params=_cparams(("parallel",)),
        name="merge",
    )(ya, yb, yc, z, z, z, x2d, wa, wb, wc, wo, gpost, gpre)


def _ffn_kernel(hf_ref, x1_ref, win_ref, wout_ref, gpost_ref, o_ref, acc_scr, *, bounds):
    hf = hf_ref[...]
    for c, (lo, hi) in enumerate(zip(bounds[:-1], bounds[1:])):
        gate = jnp.dot(hf, win_ref[:, lo:hi], preferred_element_type=F32)
        up = jnp.dot(hf, win_ref[:, D_FF + lo:D_FF + hi], preferred_element_type=F32)
        act = (gate * _sigmoid(gate) * up).astype(BF16)
        part = jnp.dot(act, wout_ref[lo:hi, :], preferred_element_type=F32)
        if c == 0:
            acc_scr[...] = part
        else:
            acc_scr[...] += part
    o_ref[...] = x1_ref[...] + _rms(acc_scr[...], gpost_ref[...])


def _ffn(hf, x1, win, wout, gpost, *, tm=512):
    T, D = x1.shape
    blocks = D_FF // V7X_MXU_WIDTH
    bounds = (0, (blocks + 1) // 2 * V7X_MXU_WIDTH, D_FF)
    row = pl.BlockSpec((tm, D), lambda i: (i, 0))
    return pl.pallas_call(
        functools.partial(_ffn_kernel, bounds=bounds),
        grid=(T // tm,),
        in_specs=[row, row,
                  pl.BlockSpec(win.shape, lambda i: (0, 0), pipeline_mode=pl.Buffered(1)),
                  pl.BlockSpec(wout.shape, lambda i: (0, 0), pipeline_mode=pl.Buffered(1)),
                  pl.BlockSpec((1, D), lambda i: (0, 0))],
        out_specs=row,
        out_shape=jax.ShapeDtypeStruct((T, D), F32),
        scratch_shapes=[pltpu.VMEM((tm, D), F32)],
        compiler_params=_cparams(("parallel",)),
        name="ffn",
    )(hf, x1, win, wout, gpost)


def _reorder_w_in(w_in):
    s = np.cumsum([0, GMLP_WIDTH, GMLP_WIDTH, NSA_Q_WIDTH, 6 * NSA_KV_WIDTH, N_BRANCH * NSA_HEADS,
                   RNN_WIDTH, RNN_WIDTH, N_BRANCH * D_MODEL])
    u, v, q, kv, ng, xr, rg, mg = (w_in[:, s[i]:s[i + 1]] for i in range(8))
    kv_cmp, kv_rest = kv[:, :2 * NSA_KV_WIDTH], kv[:, 2 * NSA_KV_WIDTH:]
    q = q * Q_SCALE
    pad = jnp.zeros((w_in.shape[0], NG_PAD - N_BRANCH * NSA_HEADS), w_in.dtype)
    return jnp.concatenate([u, v, xr, rg, mg, kv_cmp, ng, pad, q, kv_rest], axis=1).astype(BF16)


def _block_diag_gates(wa, wx):
    per = RNN_BLOCK // RNN_HEAD_DIM
    nblk = RNN_HEADS // per
    eye = jnp.eye(per, dtype=wa.dtype)

    def bd(w):
        w = w.reshape(nblk, per, RNN_HEAD_DIM, RNN_HEAD_DIM)
        return jnp.einsum('kpio,pq->kpiqo', w, eye).reshape(nblk, RNN_BLOCK, RNN_BLOCK)

    return jnp.concatenate([bd(wa), bd(wx)], axis=-1).astype(BF16)


def _layer(x2d, B, S, g_pre_mix, g_post_mix, g_pre_ffn, g_post_ffn, w_in,
           gmlp_ln_g, gmlp_ln_b, gmlp_ws, gmlp_bs,
           nsa_pe_k, nsa_pe_v, nsa_wk1, nsa_wk2, nsa_wv1, nsa_wv2,
           rnn_conv_w, rnn_conv_b, rnn_wa, rnn_ba, rnn_wx, rnn_bx, rnn_lam,
           w_br_a, w_br_b, w_br_c, w_o, w_ffn_in, w_ffn_out):
    row = lambda a: a.reshape(1, -1)
    z = _in_proj(x2d, row(g_pre_mix), _reorder_w_in(w_in))

    y_a = _gmlp(z, row(gmlp_ln_g), row(gmlp_ln_b), gmlp_ws, gmlp_bs.T)
    y_c = _rglru(z, rnn_conv_w, row(rnn_conv_b), _block_diag_gates(rnn_wa, rnn_wx),
                 row(rnn_ba), row(rnn_bx), row(rnn_lam), B=B, S=S)

    kc, vct = _compress(z, nsa_pe_k.reshape(1, -1), nsa_pe_v.reshape(1, -1), nsa_wk1.astype(BF16),
                        nsa_wk2.astype(BF16), nsa_wv1.astype(BF16), nsa_wv2.T.astype(BF16), B=B, S=S)
    y_b = _nsa(z, kc, vct, B=B, S=S)

    x1, hf = _merge(y_a, y_b, y_c, z, x2d, w_br_a.astype(BF16), w_br_b.astype(BF16), w_br_c.astype(BF16),
                    w_o.astype(BF16), row(g_post_mix), row(g_pre_ffn))
    return _ffn(hf, x1, w_ffn_in.astype(BF16), w_ffn_out.astype(BF16), row(g_post_ffn))


def kernel(x, g_pre_mix, g_post_mix, g_pre_ffn, g_post_ffn, w_in, gmlp_ln_g, gmlp_ln_b, gmlp_ws, gmlp_bs, nsa_pe_k, nsa_pe_v, nsa_wk1, nsa_wk2, nsa_wv1, nsa_wv2, rnn_conv_w, rnn_conv_b, rnn_wa, rnn_ba, rnn_wx, rnn_bx, rnn_lam, w_br_a, w_br_b, w_br_c, w_o, w_ffn_in, w_ffn_out):
    B, S, D = x.shape
    params = (g_pre_mix, g_post_mix, g_pre_ffn, g_post_ffn, w_in, gmlp_ln_g, gmlp_ln_b, gmlp_ws, gmlp_bs,
              nsa_pe_k, nsa_pe_v, nsa_wk1, nsa_wk2, nsa_wv1, nsa_wv2,
              rnn_conv_w, rnn_conv_b, rnn_wa, rnn_ba, rnn_wx, rnn_bx, rnn_lam,
              w_br_a, w_br_b, w_br_c, w_o, w_ffn_in, w_ffn_out)
    x2d = x.reshape(B * S, D)
    for l in range(w_in.shape[0]):
        x2d = _layer(x2d, B, S, *(p[l] for p in params))
    return x2d.reshape(B, S, D)
```

```python
import functools

import jax
import jax.numpy as jnp
import numpy as np
from jax import lax
from jax.experimental import pallas as pl
from jax.experimental.pallas import tpu as pltpu

F32 = jnp.float32
BF16 = jnp.bfloat16

EPS = 1e-6
NEG_INF = -1e30
FORCE_SCORE = 1e4

D_MODEL = 1024
GMLP_WIDTH = 1024
GMLP_GROUPS = 4
GMLP_GROUP_DIM = GMLP_WIDTH // GMLP_GROUPS
GMLP_CHUNK = 128

NSA_HEADS = 16
NSA_KV_GROUPS = 4
NSA_HEAD_DIM = 64
NSA_HPG = NSA_HEADS // NSA_KV_GROUPS
NSA_Q_WIDTH = NSA_HEADS * NSA_HEAD_DIM
NSA_KV_WIDTH = NSA_KV_GROUPS * NSA_HEAD_DIM
N_BRANCH = 3
CMP_BLOCK = 32
CMP_STRIDE = 16
CMP_HIDDEN = 256
SEL_BLOCK = 64
SEL_TOP_N = 16
WINDOW = 512
Q_SCALE = NSA_HEAD_DIM ** -0.5 * float(np.log2(np.e))

RNN_WIDTH = 1024
RNN_HEADS = 16
RNN_HEAD_DIM = RNN_WIDTH // RNN_HEADS
CONV_WIDTH = 4
LRU_C = 8.0
RNN_BLOCK = 256

D_FF = 2816

OFF_U = 0
OFF_V = 1024
OFF_XR = 2048
OFF_RG = 3072
OFF_MG = 4096
OFF_KC = 7168
OFF_NG = OFF_KC + 2 * NSA_KV_WIDTH
NG_PAD = 512
OFF_Q = OFF_NG + NG_PAD
OFF_KV = OFF_Q + NSA_Q_WIDTH
D_IN_PAD = OFF_KV + 4 * NSA_KV_WIDTH

LANES = 128
V7X_MXU_WIDTH = 256
BF16_SUBLANES = 16
V7X_VMEM_LIMIT = 56 * 1024 * 1024


def _cparams(sem, vmem=V7X_VMEM_LIMIT):
    return pltpu.CompilerParams(dimension_semantics=sem, vmem_limit_bytes=vmem)


def _rms(x, g):
    ms = jnp.mean(x * x, axis=-1, keepdims=True)
    return x * lax.rsqrt(ms + EPS) * g


def _gelu(x):
    return jax.nn.gelu(x)


def _sigmoid(x):
    return jax.nn.sigmoid(x)


_NT = (((1,), (1,)), ((), ()))


def _in_proj_kernel(x_ref, g_ref, w_ref, o_ref, h_scr):
    @pl.when(pl.program_id(1) == 0)
    def _():
        h_scr[...] = _rms(x_ref[...], g_ref[...]).astype(BF16)

    o_ref[...] = jnp.dot(h_scr[...], w_ref[...], preferred_element_type=F32).astype(o_ref.dtype)


def _in_proj(x2d, g, w, *, tm=1024, tn=2048):
    T, D = x2d.shape
    N = w.shape[1]
    return pl.pallas_call(
        _in_proj_kernel,
        grid=(T // tm, N // tn),
        in_specs=[
            pl.BlockSpec((tm, D), lambda i, j: (i, 0)),
            pl.BlockSpec((1, D), lambda i, j: (0, 0)),
            pl.BlockSpec((D, tn), lambda i, j: (0, j)),
        ],
        out_specs=pl.BlockSpec((tm, tn), lambda i, j: (i, j)),
        out_shape=jax.ShapeDtypeStruct((T, N), BF16),
        scratch_shapes=[pltpu.VMEM((tm, D), BF16)],
        compiler_params=_cparams(("parallel", "arbitrary")),
        name="in_proj",
    )(x2d, g, w)


def _gmlp_kernel(u_ref, v_ref, lng_ref, lnb_ref, ws_ref, bst_ref, o_ref, *, n_chunks):
    C = GMLP_CHUNK
    row = lax.broadcasted_iota(jnp.int32, (C, C), 0)
    col = lax.broadcasted_iota(jnp.int32, (C, C), 1)
    causal = col <= row
    ws = [jnp.where(causal, ws_ref[g], 0.0).astype(BF16) for g in range(GMLP_GROUPS)]
    bst = bst_ref[...]
    for c in range(n_chunks):
        rows = slice(c * C, (c + 1) * C)
        gv = _gelu(v_ref[rows, :].astype(F32))
        mu = jnp.mean(gv, axis=-1, keepdims=True)
        d = gv - mu
        var = jnp.mean(d * d, axis=-1, keepdims=True)
        vn = (d * lax.rsqrt(var + EPS) * lng_ref[...] + lnb_ref[...]).astype(BF16)
        for g in range(GMLP_GROUPS):
            cols = slice(g * GMLP_GROUP_DIM, (g + 1) * GMLP_GROUP_DIM)
            mixed = jnp.dot(ws[g], vn[:, cols], preferred_element_type=F32) + bst[:, g:g + 1]
            o_ref[rows, cols] = (_gelu(u_ref[rows, cols].astype(F32)) * mixed).astype(o_ref.dtype)


def _gmlp(z, lng, lnb, ws, bst, *, n_chunks=8):
    T = z.shape[0]
    tm = GMLP_CHUNK * n_chunks
    W = GMLP_WIDTH
    return pl.pallas_call(
        functools.partial(_gmlp_kernel, n_chunks=n_chunks),
        grid=(T // tm,),
        in_specs=[
            pl.BlockSpec((tm, W), lambda i: (i, OFF_U // W)),
            pl.BlockSpec((tm, W), lambda i: (i, OFF_V // W)),
            pl.BlockSpec((1, W), lambda i: (0, 0)),
            pl.BlockSpec((1, W), lambda i: (0, 0)),
            pl.BlockSpec((GMLP_GROUPS, GMLP_CHUNK, GMLP_CHUNK), lambda i: (0, 0, 0)),
            pl.BlockSpec((GMLP_CHUNK, GMLP_GROUPS), lambda i: (0, 0)),
        ],
        out_specs=pl.BlockSpec((tm, W), lambda i: (i, 0)),
        out_shape=jax.ShapeDtypeStruct((T, W), BF16),
        compiler_params=_cparams(("parallel",)),
        name="gmlp",
    )(z, z, lng, lnb, ws, bst)


def _rglru_kernel(xr_ref, rg_ref, cw_ref, cb_ref, wab_ref, ba_ref, bx_ref, lam_ref, o_ref,
                  ext_scr, a_scr, b_scr, hc_scr, *, tt):
    W = RNN_WIDTH
    groups = tt // 8
    sub_row = lax.broadcasted_iota(jnp.int32, (groups, 8, RNN_BLOCK), 1)

    @pl.when(pl.program_id(1) == 0)
    def _():
        ext_scr[0:8, :] = jnp.zeros((8, W), F32)
        hc_scr[...] = jnp.zeros((8, W), F32)

    xr = xr_ref[...].astype(F32)
    ext_scr[8:8 + tt, :] = xr
    cw = cw_ref[...]
    xc = (cw[3:4] * xr + cw[2:3] * ext_scr[7:7 + tt, :] + cw[1:2] * ext_scr[6:6 + tt, :]
          + cw[0:1] * ext_scr[5:5 + tt, :] + cb_ref[...])
    ext_scr[0:8, :] = xr[tt - 8:tt, :]

    lam = lam_ref[...]
    neg = -lam
    softplus = jnp.maximum(neg, 0.0) + jnp.log1p(jnp.exp(-jnp.abs(neg)))
    for k in range(W // RNN_BLOCK):
        cols = slice(k * RNN_BLOCK, (k + 1) * RNN_BLOCK)
        xck = xc[:, cols]
        gates = jnp.dot(xck.astype(BF16), wab_ref[k], preferred_element_type=F32)
        r = _sigmoid(gates[:, :RNN_BLOCK] + ba_ref[:, cols])
        i = _sigmoid(gates[:, RNN_BLOCK:] + bx_ref[:, cols])
        log_a = -LRU_C * r * softplus[:, cols]
        a = jnp.exp(log_a)
        one_minus_a2 = -jnp.tanh(log_a) * (1.0 + a * a)
        b_in = jnp.sqrt(one_minus_a2) * (i * xck)
        a3 = a.reshape(groups, 8, RNN_BLOCK)
        b3 = b_in.reshape(groups, 8, RNN_BLOCK)
        for step in (1, 2, 4):
            keep = sub_row >= step
            a_prev = jnp.where(keep, pltpu.roll(a3, step, axis=1), 1.0)
            b_prev = jnp.where(keep, pltpu.roll(b3, step, axis=1), 0.0)
            b3 = a3 * b_prev + b3
            a3 = a3 * a_prev
        a_scr[:, cols] = a3.reshape(tt, RNN_BLOCK)
        b_scr[:, cols] = b3.reshape(tt, RNN_BLOCK)

    def carry_step(i, h_prev):
        base = pl.multiple_of(i * BF16_SUBLANES, BF16_SUBLANES)
        gate = _gelu(rg_ref[pl.ds(base, BF16_SUBLANES), :].astype(F32))
        ys = []
        for half in range(BF16_SUBLANES // 8):
            rows = pl.ds(base + half * 8, 8)
            h = b_scr[rows, :] + a_scr[rows, :] * h_prev
            ys.append(h * gate[half * 8:(half + 1) * 8, :])
            h_prev = h[7:8, :]
        o_ref[pl.ds(base, BF16_SUBLANES), :] = jnp.concatenate(ys, axis=0).astype(o_ref.dtype)
        return h_prev

    h_last = lax.fori_loop(0, tt // BF16_SUBLANES, carry_step, hc_scr[0:1, :])
    hc_scr[...] = jnp.broadcast_to(h_last, (8, W))


def _rglru(z, cw, cb, wab, ba, bx, lam, *, B, S, tt=512):
    T = z.shape[0]
    W = RNN_WIDTH
    nt = S // tt
    return pl.pallas_call(
        functools.partial(_rglru_kernel, tt=tt),
        grid=(B, nt),
        in_specs=[
            pl.BlockSpec((tt, W), lambda b, t: (b * nt + t, OFF_XR // W)),
            pl.BlockSpec((tt, W), lambda b, t: (b * nt + t, OFF_RG // W)),
            pl.BlockSpec((CONV_WIDTH, W), lambda b, t: (0, 0)),
            pl.BlockSpec((1, W), lambda b, t: (0, 0)),
            pl.BlockSpec((W // RNN_BLOCK, RNN_BLOCK, 2 * RNN_BLOCK), lambda b, t: (0, 0, 0)),
            pl.BlockSpec((1, W), lambda b, t: (0, 0)),
            pl.BlockSpec((1, W), lambda b, t: (0, 0)),
            pl.BlockSpec((1, W), lambda b, t: (0, 0)),
        ],
        out_specs=pl.BlockSpec((tt, W), lambda b, t: (b * nt + t, 0)),
        out_shape=jax.ShapeDtypeStruct((T, W), BF16),
        scratch_shapes=[
            pltpu.VMEM((tt + 8, W), F32),
            pltpu.VMEM((tt, W), F32),
            pltpu.VMEM((tt, W), F32),
            pltpu.VMEM((8, W), F32),
        ],
        compiler_params=_cparams(("parallel", "arbitrary")),
        name="rglru",
    )(z, z, cw, cb, wab, ba, bx, lam)


def _compress_hidden(x_ref, pe_ref, w1_ref, x_scr):
    DH = NSA_HEAD_DIM
    half = CMP_STRIDE * DH
    slots = x_ref.shape[0] // CMP_STRIDE
    x = x_ref[...].astype(F32)
    for c in range(NSA_KV_WIDTH // LANES):
        x_scr[c] = x[:, c * LANES:(c + 1) * LANES]
    phase = [[x_scr[c, pl.ds(l, slots, stride=CMP_STRIDE), :] for c in range(NSA_KV_WIDTH // LANES)]
             for l in range(CMP_STRIDE)]
    per_block = LANES // DH

    def group_rows(g):
        lanes = slice((g % per_block) * DH, (g % per_block + 1) * DH)
        return jnp.concatenate([phase[l][g // per_block][:, lanes] for l in range(CMP_STRIDE)], axis=1)

    hm = jnp.concatenate([group_rows(g) for g in range(NSA_KV_GROUPS)], axis=0).astype(BF16)
    rows = hm.shape[0]
    lo = jnp.dot(hm, w1_ref[0:half, :], preferred_element_type=F32)
    hi = jnp.dot(hm, w1_ref[half:2 * half, :], preferred_element_type=F32)
    pe = jnp.broadcast_to(pe_ref[...], (8, 2 * half)).astype(BF16)
    pe_term = jnp.dot(pe, w1_ref[...], preferred_element_type=F32)[0:1, :]
    return _gelu(lo + pltpu.roll(hi, rows - 1, axis=0) + pe_term).astype(BF16)


def _compress_kernel(xk_ref, xv_ref, pek_ref, pev_ref, wk1_ref, wk2_ref, wv1_ref, wv2t_ref, kc_ref, vct_ref, x_scr):
    kc_ref[...] = jnp.dot(_compress_hidden(xk_ref, pek_ref, wk1_ref, x_scr), wk2_ref[...],
                          preferred_element_type=F32)
    vct_ref[...] = lax.dot_general(wv2t_ref[...], _compress_hidden(xv_ref, pev_ref, wv1_ref, x_scr), _NT,
                                   preferred_element_type=F32)


def _compress(z, pek, pev, wk1, wk2, wv1, wv2t, *, B, S):
    G, DH = NSA_KV_GROUPS, NSA_HEAD_DIM
    slots = S // CMP_STRIDE
    KVW = NSA_KV_WIDTH
    full = lambda shape: pl.BlockSpec(shape, lambda b: (0,) * len(shape))
    return pl.pallas_call(
        _compress_kernel,
        grid=(B,),
        in_specs=[pl.BlockSpec((S, KVW), lambda b: (b, OFF_KC // KVW)),
                  pl.BlockSpec((S, KVW), lambda b: (b, OFF_KC // KVW + 1)),
                  full(pek.shape), full(pev.shape), full(wk1.shape), full(wk2.shape), full(wv1.shape),
                  full(wv2t.shape)],
        out_specs=[pl.BlockSpec((G * slots, DH), lambda b: (b, 0)), pl.BlockSpec((DH, G * slots), lambda b: (0, b))],
        out_shape=[jax.ShapeDtypeStruct((B * G * slots, DH), F32), jax.ShapeDtypeStruct((DH, B * G * slots), F32)],
        scratch_shapes=[pltpu.VMEM((KVW // LANES, S, LANES), F32)],
        compiler_params=_cparams(("parallel",)),
        name="nsa_compress",
    )(z, z, pek, pev, wk1, wk2, wv1, wv2t)


def _nsa_kernel(q_ref, kc_ref, vct_ref, ks_ref, vs_ref, kw_ref, vw_ref, gate_ref, o_ref,
                ks_scr, kw_scr, vs_scr, vw_scr, *tile_scratch, tq, S):
    DH = NSA_HEAD_DIM
    tk = tq
    v_rows = DH + BF16_SUBLANES
    g_idx = pl.program_id(1)

    for g in range(NSA_KV_GROUPS):
        @pl.when(g_idx == g)
        def _():
            lanes = slice(g * DH, (g + 1) * DH)
            ks_scr[...] = ks_ref[:, lanes]
            kw_scr[...] = kw_ref[:, lanes]
            ones = jnp.ones((BF16_SUBLANES, tk), BF16)
            for v_ref, v_scr in ((vs_ref, vs_scr), (vw_ref, vw_scr)):
                vt = v_ref[:, lanes].T
                for kt in range(S // tk):
                    v_scr[kt, 0:DH, :] = vt[:, kt * tk:(kt + 1) * tk]
                    v_scr[kt, DH:v_rows, :] = ones

    def q_tile(qi, carry):
        rows = pl.ds(pl.multiple_of(qi * tq, tq), tq)
        _nsa_q_tile(qi, g_idx, q_ref.at[rows, :], kc_ref, vct_ref, gate_ref.at[rows, :], o_ref.at[rows, :],
                    ks_scr, kw_scr, vs_scr, vw_scr, *tile_scratch, tq=tq, S=S)
        return carry

    lax.fori_loop(0, S // tq, q_tile, 0)


def _nsa_q_tile(qi, g_idx, q_ref, kc_ref, vct_ref, gate_ref, o_ref,
                ks_scr, kw_scr, vs_scr, vw_scr, gt_scr, ps_scr, sel_scr, sa_scr, sb_scr, m_scr, acc_scr, *, tq, S):
    HP, DH = NSA_HPG, NSA_HEAD_DIM
    M = HP * tq
    tk = tq
    n_sel = S // SEL_BLOCK
    slots = kc_ref.shape[0]
    v_rows = DH + BF16_SUBLANES
    t0 = qi * tq

    def per_head(x):
        return jnp.concatenate([x] * HP, axis=1)

    q_t = q_ref[...].T
    qt = jnp.concatenate([q_t[h * DH:(h + 1) * DH, :] for h in range(HP)], axis=1)

    def scores(k_scr, kt, dst):
        start = pl.multiple_of(kt * tk, tk)
        dst[...] = jnp.dot(k_scr[pl.ds(start, tk), :], qt, preferred_element_type=F32)

    sc = jnp.dot(kc_ref[...].astype(BF16), qt, preferred_element_type=F32)
    scores(ks_scr, 0, sa_scr)
    n_idx = lax.broadcasted_iota(jnp.int32, (slots, tq), 0)
    t_idx = t0 + lax.broadcasted_iota(jnp.int32, (slots, tq), 1)
    valid = per_head(jnp.where(n_idx * CMP_STRIDE + (CMP_BLOCK - 1) <= t_idx, 1.0, 0.0)) > 0.5
    sc = jnp.where(valid, sc, NEG_INF)
    mx = jnp.max(sc, axis=0, keepdims=True)
    p = jnp.where(valid, jnp.exp2(sc - mx), 0.0)
    den = jnp.sum(p, axis=0, keepdims=True)
    p_c = p * (1.0 / jnp.where(den > 0.0, den, 1.0))
    o_cmp = jnp.dot(vct_ref[...].astype(BF16), p_c.astype(BF16), preferred_element_type=F32)

    p_sum = p_c[:, 0:tq]
    for h in range(1, HP):
        p_sum = p_sum + p_c[:, h * tq:(h + 1) * tq]
    per_sel = SEL_BLOCK // CMP_STRIDE
    for c in range(tq // LANES):
        ps_scr[c] = p_sum[:, c * LANES:(c + 1) * LANES]
    every = [jnp.concatenate([ps_scr[c, pl.ds(r, n_sel, stride=per_sel), :] for c in range(tq // LANES)], axis=1)
             for r in range(per_sel)]
    j_idx = lax.broadcasted_iota(jnp.int32, (n_sel, tq), 0)
    before = jnp.where(j_idx >= 1, pltpu.roll(every[per_sel - 1], 1, axis=0), 0.0)
    imp = every[0]
    for r in range(1, per_sel - 1):
        imp = imp + every[r]
    imp = imp + 0.5 * every[per_sel - 1] + 0.5 * before
    cur = (t0 + lax.broadcasted_iota(jnp.int32, (n_sel, tq), 1)) // SEL_BLOCK
    forced = (j_idx == 0) | (j_idx == cur) | (j_idx == cur - 1)
    imp = jnp.where(forced, FORCE_SCORE, jnp.where(j_idx > cur, NEG_INF, imp))
    rank = jnp.zeros((n_sel, tq), F32)
    for i in range(n_sel):
        ci = imp[i:i + 1, :]
        ge = jnp.where(ci >= imp, 1.0, 0.0)
        gt = jnp.where(ci > imp, 1.0, 0.0)
        rank = rank + jnp.where(j_idx > i, ge, gt)
    sel_scr[...] = jnp.where(rank < float(min(SEL_TOP_N, n_sel)), 1.0, 0.0)

    per_tile = tk // SEL_BLOCK
    kloc = lax.broadcasted_iota(jnp.int32, (tk, tq), 0)
    tloc = lax.broadcasted_iota(jnp.int32, (tk, tq), 1)
    key_ahead = kloc - tloc

    def slc_bias(kt):
        hit = jnp.concatenate(
            [jnp.broadcast_to(sel_scr[pl.ds(kt * per_tile + jj, 1), :], (SEL_BLOCK, tq)) for jj in range(per_tile)],
            axis=0)
        return jnp.where((hit > 0.5) & (key_ahead <= (qi - kt) * tk), 0.0, NEG_INF)

    def absorb(v_ext, src, bias):
        s = src[...] if bias is None else src[...] + per_head(bias)
        m_old = m_scr[...]
        m_new = jnp.maximum(m_old, jnp.max(s, axis=0, keepdims=True))
        alpha = jnp.exp2(m_old - m_new)
        pr = jnp.exp2(s - m_new).astype(BF16)
        m_scr[...] = m_new
        acc_scr[...] = alpha * acc_scr[...] + jnp.dot(v_ext, pr, preferred_element_type=F32)

    def reset():
        m_scr[...] = jnp.full((1, M), NEG_INF, F32)
        acc_scr[...] = jnp.zeros((v_rows, M), F32)

    def result():
        acc = acc_scr[...]
        return acc[0:DH, :] * (1.0 / acc[DH:DH + 1, :])

    reset()
    n_tiles = qi + 1

    def slc_scores(kt, dst):
        scores(ks_scr, jnp.minimum(kt, qi), dst)

    def slc_pair(j, carry):
        slc_scores(2 * j + 1, sb_scr)
        absorb(vs_scr[2 * j], sa_scr, slc_bias(2 * j))
        slc_scores(2 * j + 2, sa_scr)
        absorb(vs_scr[2 * j + 1], sb_scr, slc_bias(2 * j + 1))
        return carry

    lax.fori_loop(0, n_tiles // 2, slc_pair, 0)

    @pl.when(n_tiles % 2 == 1)
    def _():
        absorb(vs_scr[qi], sa_scr, slc_bias(qi))

    o_slc = result()

    reset()
    n_back = WINDOW // tk
    diag_bias = jnp.where(kloc <= tloc, 0.0, NEG_INF)
    far_bias = jnp.where(kloc > tloc, 0.0, NEG_INF)

    def win_bias(d):
        return diag_bias if d == 0 else (far_bias if d == n_back else None)

    def window(n_win):
        bufs = (sa_scr, sb_scr)
        scores(kw_scr, qi, bufs[0])
        for d in range(n_win):
            if d + 1 < n_win:
                scores(kw_scr, qi - (d + 1), bufs[(d + 1) % 2])
            absorb(vw_scr[qi - d], bufs[d % 2], win_bias(d))

    for n_win in range(1, n_back + 2):
        @pl.when((qi == n_win - 1) if n_win <= n_back else (qi >= n_back))
        def _():
            window(n_win)

    o_win = result()

    gt_scr[...] = gate_ref[...].astype(F32).T

    def gate(branch):
        rows = [gt_scr[pl.ds(g_idx * (HP * N_BRANCH) + h * N_BRANCH + branch, 1), :] for h in range(HP)]
        return _sigmoid(jnp.concatenate(rows, axis=1))

    o = gate(0) * o_cmp + gate(1) * o_slc + gate(2) * o_win
    o_ref[...] = jnp.concatenate([o[:, h * tq:(h + 1) * tq].T for h in range(HP)], axis=1).astype(o_ref.dtype)


def _nsa(z, kc, vct, *, B, S, tq=256):
    G, HP, DH = NSA_KV_GROUPS, NSA_HPG, NSA_HEAD_DIM
    T = B * S
    slots = S // CMP_STRIDE
    n_sel = S // SEL_BLOCK
    n_kt = S // tq
    v_rows = DH + BF16_SUBLANES
    KVW = NSA_KV_WIDTH

    def kv_spec(which):
        return pl.BlockSpec((S, KVW), lambda b, g: (b, OFF_KV // KVW + which))

    return pl.pallas_call(
        functools.partial(_nsa_kernel, tq=tq, S=S),
        grid=(B, G),
        in_specs=[
            pl.BlockSpec((S, HP * DH), lambda b, g: (b, OFF_Q // (HP * DH) + g)),
            pl.BlockSpec((slots, DH), lambda b, g: (b * G + g, 0)),
            pl.BlockSpec((DH, slots), lambda b, g: (0, b * G + g)),
            kv_spec(0), kv_spec(1), kv_spec(2), kv_spec(3),
            pl.BlockSpec((S, LANES), lambda b, g: (b, OFF_NG // LANES)),
        ],
        out_specs=pl.BlockSpec((S, HP * DH), lambda b, g: (b, g)),
        out_shape=jax.ShapeDtypeStruct((T, NSA_Q_WIDTH), BF16),
        scratch_shapes=[
            pltpu.VMEM((S, DH), BF16),
            pltpu.VMEM((S, DH), BF16),
            pltpu.VMEM((n_kt, v_rows, tq), BF16),
            pltpu.VMEM((n_kt, v_rows, tq), BF16),
            pltpu.VMEM((LANES, tq), F32),
            pltpu.VMEM((tq // LANES, slots, LANES), F32),
            pltpu.VMEM((n_sel, tq), F32),
            pltpu.VMEM((tq, HP * tq), F32),
            pltpu.VMEM((tq, HP * tq), F32),
            pltpu.VMEM((1, HP * tq), F32),
            pltpu.VMEM((v_rows, HP * tq), F32),
        ],
        compiler_params=_cparams(("parallel", "parallel")),
        name="nsa_attention",
    )(z, kc, vct, z, z, z, z, z)


def _merge_kernel(ya_ref, yb_ref, yc_ref, mga_ref, mgb_ref, mgc_ref, x_ref, wa_ref, wb_ref, wc_ref, wo_ref,
                  gpost_ref, gpre_ref, x1_ref, hf_ref):
    gate = lambda ref: _sigmoid(ref[...].astype(F32))
    merged = gate(mga_ref) * jnp.dot(ya_ref[...], wa_ref[...], preferred_element_type=F32)
    merged = merged + gate(mgb_ref) * jnp.dot(yb_ref[...], wb_ref[...], preferred_element_type=F32)
    merged = merged + gate(mgc_ref) * jnp.dot(yc_ref[...], wc_ref[...], preferred_element_type=F32)
    y = jnp.dot(merged.astype(BF16), wo_ref[...], preferred_element_type=F32)
    x1 = x_ref[...] + _rms(y, gpost_ref[...])
    x1_ref[...] = x1
    hf_ref[...] = _rms(x1, gpre_ref[...]).astype(BF16)


def _merge(ya, yb, yc, z, x2d, wa, wb, wc, wo, gpost, gpre, *, tm=512):
    T, D = x2d.shape
    row = lambda c: pl.BlockSpec((tm, D), lambda i: (i, c))
    wfull = pl.BlockSpec((D, D), lambda i: (0, 0))
    vec = pl.BlockSpec((1, D), lambda i: (0, 0))
    mg0 = OFF_MG // D
    return pl.pallas_call(
        _merge_kernel,
        grid=(T // tm,),
        in_specs=[row(0), row(0), row(0), row(mg0), row(mg0 + 1), row(mg0 + 2), row(0),
                  wfull, wfull, wfull, wfull, vec, vec],
        out_specs=[row(0), row(0)],
        out_shape=[jax.ShapeDtypeStruct((T, D), F32), jax.ShapeDtypeStruct((T, D), BF16)],
        compiler_params=_cparams(("parallel",)),
        name="merge",
    )(ya, yb, yc, z, z, z, x2d, wa, wb, wc, wo, gpost, gpre)


def _ffn_kernel(hf_ref, x1_ref, win_ref, wout_ref, gpost_ref, o_ref, acc_scr, *, bounds):
    hf = hf_ref[...]
    for c, (lo, hi) in enumerate(zip(bounds[:-1], bounds[1:])):
        gate = jnp.dot(hf, win_ref[:, lo:hi], preferred_element_type=F32)
        up = jnp.dot(hf, win_ref[:, D_FF + lo:D_FF + hi], preferred_element_type=F32)
        act = (gate * _sigmoid(gate) * up).astype(BF16)
        part = jnp.dot(act, wout_ref[lo:hi, :], preferred_element_type=F32)
        if c == 0:
            acc_scr[...] = part
        else:
            acc_scr[...] += part
    o_ref[...] = x1_ref[...] + _rms(acc_scr[...], gpost_ref[...])


def _ffn(hf, x1, win, wout, gpost, *, tm=512):
    T, D = x1.shape
    blocks = D_FF // V7X_MXU_WIDTH
    bounds = (0, (blocks + 1) // 2 * V7X_MXU_WIDTH, D_FF)
    row = pl.BlockSpec((tm, D), lambda i: (i, 0))
    return pl.pallas_call(
        functools.partial(_ffn_kernel, bounds=bounds),
        grid=(T // tm,),
        in_specs=[row, row,
                  pl.BlockSpec(win.shape, lambda i: (0, 0), pipeline_mode=pl.Buffered(1)),
                  pl.BlockSpec(wout.shape, lambda i: (0, 0), pipeline_mode=pl.Buffered(1)),
                  pl.BlockSpec((1, D), lambda i: (0, 0))],
        out_specs=row,
        out_shape=jax.ShapeDtypeStruct((T, D), F32),
        scratch_shapes=[pltpu.VMEM((tm, D), F32)],
        compiler_params=_cparams(("parallel",)),
        name="ffn",
    )(hf, x1, win, wout, gpost)


_SRC = [int(c) for c in np.cumsum([0, GMLP_WIDTH, GMLP_WIDTH, NSA_Q_WIDTH, 6 * NSA_KV_WIDTH, N_BRANCH * NSA_HEADS,
                                   RNN_WIDTH, RNN_WIDTH, N_BRANCH * D_MODEL])]


def _reorder_w_in_kernel(w_ref, o_ref):
    u0, _, q0, kv0, ng0, xr0, _, _, end = _SRC
    rest0 = kv0 + 2 * NSA_KV_WIDTH

    def put(dst, lo, hi, scale=None):
        x = w_ref[:, lo:hi]
        if scale is not None:
            x = x * scale
        o_ref[:, dst:dst + hi - lo] = x.astype(o_ref.dtype)

    put(OFF_U, u0, q0)
    put(OFF_XR, xr0, end)
    put(OFF_KC, kv0, rest0)
    put(OFF_NG, ng0, xr0)
    pad0 = OFF_NG + xr0 - ng0
    o_ref[:, pad0:OFF_Q] = jnp.zeros((o_ref.shape[0], OFF_Q - pad0), o_ref.dtype)
    put(OFF_Q, q0, kv0, Q_SCALE)
    put(OFF_KV, rest0, ng0)


def _reorder_w_in(w_in, *, tr=128):
    L, D, d_in = w_in.shape
    return pl.pallas_call(
        _reorder_w_in_kernel,
        grid=(L, D // tr),
        in_specs=[pl.BlockSpec((None, tr, d_in), lambda l, i: (l, i, 0))],
        out_specs=pl.BlockSpec((None, tr, D_IN_PAD), lambda l, i: (l, i, 0)),
        out_shape=jax.ShapeDtypeStruct((L, D, D_IN_PAD), BF16),
        compiler_params=_cparams(("parallel", "parallel")),
        name="reorder_w_in",
    )(w_in)


def _block_diag_gates(wa, wx):
    per = RNN_BLOCK // RNN_HEAD_DIM
    nblk = RNN_HEADS // per
    eye = jnp.eye(per, dtype=wa.dtype)

    def bd(w):
        w = w.reshape(nblk, per, RNN_HEAD_DIM, RNN_HEAD_DIM)
        return jnp.einsum('kpio,pq->kpiqo', w, eye).reshape(nblk, RNN_BLOCK, RNN_BLOCK)

    return jnp.concatenate([bd(wa), bd(wx)], axis=-1).astype(BF16)


def _layer(x2d, B, S, g_pre_mix, g_post_mix, g_pre_ffn, g_post_ffn, w_in,
           gmlp_ln_g, gmlp_ln_b, gmlp_ws, gmlp_bs,
           nsa_pe_k, nsa_pe_v, nsa_wk1, nsa_wk2, nsa_wv1, nsa_wv2,
           rnn_conv_w, rnn_conv_b, rnn_wa, rnn_ba, rnn_wx, rnn_bx, rnn_lam,
           w_br_a, w_br_b, w_br_c, w_o, w_ffn_in, w_ffn_out):
    row = lambda a: a.reshape(1, -1)
    z = _in_proj(x2d, row(g_pre_mix), w_in)

    y_a = _gmlp(z, row(gmlp_ln_g), row(gmlp_ln_b), gmlp_ws, gmlp_bs.T)
    y_c = _rglru(z, rnn_conv_w, row(rnn_conv_b), _block_diag_gates(rnn_wa, rnn_wx),
                 row(rnn_ba), row(rnn_bx), row(rnn_lam), B=B, S=S)

    kc, vct = _compress(z, nsa_pe_k.reshape(1, -1), nsa_pe_v.reshape(1, -1), nsa_wk1.astype(BF16),
                        nsa_wk2.astype(BF16), nsa_wv1.astype(BF16), nsa_wv2.T.astype(BF16), B=B, S=S)
    y_b = _nsa(z, kc, vct, B=B, S=S)

    x1, hf = _merge(y_a, y_b, y_c, z, x2d, w_br_a.astype(BF16), w_br_b.astype(BF16), w_br_c.astype(BF16),
                    w_o.astype(BF16), row(g_post_mix), row(g_pre_ffn))
    return _ffn(hf, x1, w_ffn_in.astype(BF16), w_ffn_out.astype(BF16), row(g_post_ffn))


def kernel(x, g_pre_mix, g_post_mix, g_pre_ffn, g_post_ffn, w_in, gmlp_ln_g, gmlp_ln_b, gmlp_ws, gmlp_bs, nsa_pe_k, nsa_pe_v, nsa_wk1, nsa_wk2, nsa_wv1, nsa_wv2, rnn_conv_w, rnn_conv_b, rnn_wa, rnn_ba, rnn_wx, rnn_bx, rnn_lam, w_br_a, w_br_b, w_br_c, w_o, w_ffn_in, w_ffn_out):
    B, S, D = x.shape
    params = (g_pre_mix, g_post_mix, g_pre_ffn, g_post_ffn, _reorder_w_in(w_in), gmlp_ln_g, gmlp_ln_b, gmlp_ws,
              gmlp_bs,
              nsa_pe_k, nsa_pe_v, nsa_wk1, nsa_wk2, nsa_wv1, nsa_wv2,
              rnn_conv_w, rnn_conv_b, rnn_wa, rnn_ba, rnn_wx, rnn_bx, rnn_lam,
              w_br_a, w_br_b, w_br_c, w_o, w_ffn_in, w_ffn_out)
    x2d = x.reshape(B * S, D)
    for l in range(w_in.shape[0]):
        x2d = _layer(x2d, B, S, *(p[l] for p in params))
    return x2d.reshape(B, S, D)
```

```python
import functools

import jax
import jax.numpy as jnp
import numpy as np
from jax import lax
from jax.experimental import pallas as pl
from jax.experimental.pallas import tpu as pltpu

F32 = jnp.float32
BF16 = jnp.bfloat16

EPS = 1e-6
NEG_INF = -1e30
FORCE_SCORE = 1e4

D_MODEL = 1024
GMLP_WIDTH = 1024
GMLP_GROUPS = 4
GMLP_GROUP_DIM = GMLP_WIDTH // GMLP_GROUPS
GMLP_CHUNK = 128

NSA_HEADS = 16
NSA_KV_GROUPS = 4
NSA_HEAD_DIM = 64
NSA_HPG = NSA_HEADS // NSA_KV_GROUPS
NSA_Q_WIDTH = NSA_HEADS * NSA_HEAD_DIM
NSA_KV_WIDTH = NSA_KV_GROUPS * NSA_HEAD_DIM
N_BRANCH = 3
CMP_BLOCK = 32
CMP_STRIDE = 16
CMP_HIDDEN = 256
SEL_BLOCK = 64
SEL_TOP_N = 16
WINDOW = 512
Q_SCALE = NSA_HEAD_DIM ** -0.5 * float(np.log2(np.e))

RNN_WIDTH = 1024
RNN_HEADS = 16
RNN_HEAD_DIM = RNN_WIDTH // RNN_HEADS
CONV_WIDTH = 4
LRU_C = 8.0
RNN_BLOCK = 256

D_FF = 2816

OFF_U = 0
OFF_V = 1024
OFF_XR = 2048
OFF_RG = 3072
OFF_MG = 4096
OFF_KC = 7168
OFF_NG = OFF_KC + 2 * NSA_KV_WIDTH
NG_PAD = 512
OFF_Q = OFF_NG + NG_PAD
OFF_KV = OFF_Q + NSA_Q_WIDTH
D_IN_PAD = OFF_KV + 4 * NSA_KV_WIDTH

LANES = 128
V7X_MXU_WIDTH = 256
BF16_SUBLANES = 16
V7X_VMEM_LIMIT = 56 * 1024 * 1024


def _cparams(sem, vmem=V7X_VMEM_LIMIT):
    return pltpu.CompilerParams(dimension_semantics=sem, vmem_limit_bytes=vmem)


def _rms(x, g):
    ms = jnp.mean(x * x, axis=-1, keepdims=True)
    return x * lax.rsqrt(ms + EPS) * g


def _gelu(x):
    return jax.nn.gelu(x)


def _sigmoid(x):
    return jax.nn.sigmoid(x)


_NT = (((1,), (1,)), ((), ()))


def _in_proj_kernel(x_ref, g_ref, w_ref, o_ref, h_scr):
    @pl.when(pl.program_id(1) == 0)
    def _():
        h_scr[...] = _rms(x_ref[...], g_ref[...]).astype(BF16)

    o_ref[...] = lax.dot_general(h_scr[...], w_ref[...], _NT, preferred_element_type=F32).astype(o_ref.dtype)


def _in_proj(x2d, g, wt_all, layer, *, tm=1024, tn=2048):
    T, D = x2d.shape
    N = wt_all.shape[1]
    return pl.pallas_call(
        _in_proj_kernel,
        grid=(T // tm, N // tn),
        in_specs=[
            pl.BlockSpec((tm, D), lambda i, j: (i, 0)),
            pl.BlockSpec((1, D), lambda i, j: (0, 0)),
            pl.BlockSpec((None, tn, D), lambda i, j: (layer, j, 0)),
        ],
        out_specs=pl.BlockSpec((tm, tn), lambda i, j: (i, j)),
        out_shape=jax.ShapeDtypeStruct((T, N), BF16),
        scratch_shapes=[pltpu.VMEM((tm, D), BF16)],
        compiler_params=_cparams(("parallel", "arbitrary")),
        name="in_proj",
    )(x2d, g, wt_all)


def _gmlp_kernel(u_ref, v_ref, lng_ref, lnb_ref, ws_ref, bst_ref, o_ref, *, n_chunks):
    C = GMLP_CHUNK
    row = lax.broadcasted_iota(jnp.int32, (C, C), 0)
    col = lax.broadcasted_iota(jnp.int32, (C, C), 1)
    causal = col <= row
    ws = [jnp.where(causal, ws_ref[g], 0.0).astype(BF16) for g in range(GMLP_GROUPS)]
    bst = bst_ref[...]
    for c in range(n_chunks):
        rows = slice(c * C, (c + 1) * C)
        gv = _gelu(v_ref[rows, :].astype(F32))
        mu = jnp.mean(gv, axis=-1, keepdims=True)
        d = gv - mu
        var = jnp.mean(d * d, axis=-1, keepdims=True)
        vn = (d * lax.rsqrt(var + EPS) * lng_ref[...] + lnb_ref[...]).astype(BF16)
        for g in range(GMLP_GROUPS):
            cols = slice(g * GMLP_GROUP_DIM, (g + 1) * GMLP_GROUP_DIM)
            mixed = jnp.dot(ws[g], vn[:, cols], preferred_element_type=F32) + bst[:, g:g + 1]
            o_ref[rows, cols] = (_gelu(u_ref[rows, cols].astype(F32)) * mixed).astype(o_ref.dtype)


def _gmlp(z, lng, lnb, ws, bst, *, n_chunks=8):
    T = z.shape[0]
    tm = GMLP_CHUNK * n_chunks
    W = GMLP_WIDTH
    return pl.pallas_call(
        functools.partial(_gmlp_kernel, n_chunks=n_chunks),
        grid=(T // tm,),
        in_specs=[
            pl.BlockSpec((tm, W), lambda i: (i, OFF_U // W)),
            pl.BlockSpec((tm, W), lambda i: (i, OFF_V // W)),
            pl.BlockSpec((1, W), lambda i: (0, 0)),
            pl.BlockSpec((1, W), lambda i: (0, 0)),
            pl.BlockSpec((GMLP_GROUPS, GMLP_CHUNK, GMLP_CHUNK), lambda i: (0, 0, 0)),
            pl.BlockSpec((GMLP_CHUNK, GMLP_GROUPS), lambda i: (0, 0)),
        ],
        out_specs=pl.BlockSpec((tm, W), lambda i: (i, 0)),
        out_shape=jax.ShapeDtypeStruct((T, W), BF16),
        compiler_params=_cparams(("parallel",)),
        name="gmlp",
    )(z, z, lng, lnb, ws, bst)


def _rglru_kernel(xr_ref, rg_ref, cw_ref, cb_ref, wab_ref, ba_ref, bx_ref, lam_ref, o_ref,
                  ext_scr, a_scr, b_scr, hc_scr, *, tt):
    W = RNN_WIDTH
    groups = tt // 8
    sub_row = lax.broadcasted_iota(jnp.int32, (groups, 8, RNN_BLOCK), 1)

    @pl.when(pl.program_id(1) == 0)
    def _():
        ext_scr[0:8, :] = jnp.zeros((8, W), F32)
        hc_scr[...] = jnp.zeros((8, W), F32)

    xr = xr_ref[...].astype(F32)
    ext_scr[8:8 + tt, :] = xr
    cw = cw_ref[...]
    xc = (cw[3:4] * xr + cw[2:3] * ext_scr[7:7 + tt, :] + cw[1:2] * ext_scr[6:6 + tt, :]
          + cw[0:1] * ext_scr[5:5 + tt, :] + cb_ref[...])
    ext_scr[0:8, :] = xr[tt - 8:tt, :]

    lam = lam_ref[...]
    neg = -lam
    softplus = jnp.maximum(neg, 0.0) + jnp.log1p(jnp.exp(-jnp.abs(neg)))
    for k in range(W // RNN_BLOCK):
        cols = slice(k * RNN_BLOCK, (k + 1) * RNN_BLOCK)
        xck = xc[:, cols]
        gates = jnp.dot(xck.astype(BF16), wab_ref[k], preferred_element_type=F32)
        r = _sigmoid(gates[:, :RNN_BLOCK] + ba_ref[:, cols])
        i = _sigmoid(gates[:, RNN_BLOCK:] + bx_ref[:, cols])
        log_a = -LRU_C * r * softplus[:, cols]
        a = jnp.exp(log_a)
        one_minus_a2 = -jnp.tanh(log_a) * (1.0 + a * a)
        b_in = jnp.sqrt(one_minus_a2) * (i * xck)
        a3 = a.reshape(groups, 8, RNN_BLOCK)
        b3 = b_in.reshape(groups, 8, RNN_BLOCK)
        for step in (1, 2, 4):
            keep = sub_row >= step
            a_prev = jnp.where(keep, pltpu.roll(a3, step, axis=1), 1.0)
            b_prev = jnp.where(keep, pltpu.roll(b3, step, axis=1), 0.0)
            b3 = a3 * b_prev + b3
            a3 = a3 * a_prev
        a_scr[:, cols] = a3.reshape(tt, RNN_BLOCK)
        b_scr[:, cols] = b3.reshape(tt, RNN_BLOCK)

    def carry_step(i, h_prev):
        base = pl.multiple_of(i * BF16_SUBLANES, BF16_SUBLANES)
        gate = _gelu(rg_ref[pl.ds(base, BF16_SUBLANES), :].astype(F32))
        ys = []
        for half in range(BF16_SUBLANES // 8):
            rows = pl.ds(base + half * 8, 8)
            h = b_scr[rows, :] + a_scr[rows, :] * h_prev
            ys.append(h * gate[half * 8:(half + 1) * 8, :])
            h_prev = h[7:8, :]
        o_ref[pl.ds(base, BF16_SUBLANES), :] = jnp.concatenate(ys, axis=0).astype(o_ref.dtype)
        return h_prev

    h_last = lax.fori_loop(0, tt // BF16_SUBLANES, carry_step, hc_scr[0:1, :])
    hc_scr[...] = jnp.broadcast_to(h_last, (8, W))


def _rglru(z, cw, cb, wab, ba, bx, lam, *, B, S, tt=512):
    T = z.shape[0]
    W = RNN_WIDTH
    nt = S // tt
    return pl.pallas_call(
        functools.partial(_rglru_kernel, tt=tt),
        grid=(B, nt),
        in_specs=[
            pl.BlockSpec((tt, W), lambda b, t: (b * nt + t, OFF_XR // W)),
            pl.BlockSpec((tt, W), lambda b, t: (b * nt + t, OFF_RG // W)),
            pl.BlockSpec((CONV_WIDTH, W), lambda b, t: (0, 0)),
            pl.BlockSpec((1, W), lambda b, t: (0, 0)),
            pl.BlockSpec((W // RNN_BLOCK, RNN_BLOCK, 2 * RNN_BLOCK), lambda b, t: (0, 0, 0)),
            pl.BlockSpec((1, W), lambda b, t: (0, 0)),
            pl.BlockSpec((1, W), lambda b, t: (0, 0)),
            pl.BlockSpec((1, W), lambda b, t: (0, 0)),
        ],
        out_specs=pl.BlockSpec((tt, W), lambda b, t: (b * nt + t, 0)),
        out_shape=jax.ShapeDtypeStruct((T, W), BF16),
        scratch_shapes=[
            pltpu.VMEM((tt + 8, W), F32),
            pltpu.VMEM((tt, W), F32),
            pltpu.VMEM((tt, W), F32),
            pltpu.VMEM((8, W), F32),
        ],
        compiler_params=_cparams(("parallel", "arbitrary")),
        name="rglru",
    )(z, z, cw, cb, wab, ba, bx, lam)


def _compress_hidden(x_ref, pe_ref, w1_ref, x_scr):
    DH = NSA_HEAD_DIM
    half = CMP_STRIDE * DH
    slots = x_ref.shape[0] // CMP_STRIDE
    x = x_ref[...].astype(F32)
    for c in range(NSA_KV_WIDTH // LANES):
        x_scr[c] = x[:, c * LANES:(c + 1) * LANES]
    phase = [[x_scr[c, pl.ds(l, slots, stride=CMP_STRIDE), :] for c in range(NSA_KV_WIDTH // LANES)]
             for l in range(CMP_STRIDE)]
    per_block = LANES // DH

    def group_rows(g):
        lanes = slice((g % per_block) * DH, (g % per_block + 1) * DH)
        return jnp.concatenate([phase[l][g // per_block][:, lanes] for l in range(CMP_STRIDE)], axis=1)

    hm = jnp.concatenate([group_rows(g) for g in range(NSA_KV_GROUPS)], axis=0).astype(BF16)
    rows = hm.shape[0]
    lo = jnp.dot(hm, w1_ref[0:half, :], preferred_element_type=F32)
    hi = jnp.dot(hm, w1_ref[half:2 * half, :], preferred_element_type=F32)
    pe = jnp.broadcast_to(pe_ref[...], (8, 2 * half)).astype(BF16)
    pe_term = jnp.dot(pe, w1_ref[...], preferred_element_type=F32)[0:1, :]
    return _gelu(lo + pltpu.roll(hi, rows - 1, axis=0) + pe_term).astype(BF16)


def _compress_kernel(xk_ref, xv_ref, pek_ref, pev_ref, wk1_ref, wk2_ref, wv1_ref, wv2t_ref, kc_ref, vct_ref, x_scr):
    kc_ref[...] = jnp.dot(_compress_hidden(xk_ref, pek_ref, wk1_ref, x_scr), wk2_ref[...],
                          preferred_element_type=F32)
    vct_ref[...] = lax.dot_general(wv2t_ref[...], _compress_hidden(xv_ref, pev_ref, wv1_ref, x_scr), _NT,
                                   preferred_element_type=F32)


def _compress(z, pek, pev, wk1, wk2, wv1, wv2t, *, B, S):
    G, DH = NSA_KV_GROUPS, NSA_HEAD_DIM
    slots = S // CMP_STRIDE
    KVW = NSA_KV_WIDTH
    full = lambda shape: pl.BlockSpec(shape, lambda b: (0,) * len(shape))
    return pl.pallas_call(
        _compress_kernel,
        grid=(B,),
        in_specs=[pl.BlockSpec((S, KVW), lambda b: (b, OFF_KC // KVW)),
                  pl.BlockSpec((S, KVW), lambda b: (b, OFF_KC // KVW + 1)),
                  full(pek.shape), full(pev.shape), full(wk1.shape), full(wk2.shape), full(wv1.shape),
                  full(wv2t.shape)],
        out_specs=[pl.BlockSpec((G * slots, DH), lambda b: (b, 0)), pl.BlockSpec((DH, G * slots), lambda b: (0, b))],
        out_shape=[jax.ShapeDtypeStruct((B * G * slots, DH), F32), jax.ShapeDtypeStruct((DH, B * G * slots), F32)],
        scratch_shapes=[pltpu.VMEM((KVW // LANES, S, LANES), F32)],
        compiler_params=_cparams(("parallel",)),
        name="nsa_compress",
    )(z, z, pek, pev, wk1, wk2, wv1, wv2t)


def _nsa_kernel(q_ref, kc_ref, vct_ref, ks_ref, vs_ref, kw_ref, vw_ref, gate_ref, o_ref,
                ks_scr, kw_scr, vs_scr, vw_scr, *tile_scratch, tq, S):
    DH = NSA_HEAD_DIM
    tk = tq
    v_rows = DH + BF16_SUBLANES
    g_idx = pl.program_id(1)

    for g in range(NSA_KV_GROUPS):
        @pl.when(g_idx == g)
        def _():
            lanes = slice(g * DH, (g + 1) * DH)
            ks_scr[...] = ks_ref[:, lanes]
            kw_scr[...] = kw_ref[:, lanes]
            ones = jnp.ones((BF16_SUBLANES, tk), BF16)
            for v_ref, v_scr in ((vs_ref, vs_scr), (vw_ref, vw_scr)):
                vt = v_ref[:, lanes].T
                for kt in range(S // tk):
                    v_scr[kt, 0:DH, :] = vt[:, kt * tk:(kt + 1) * tk]
                    v_scr[kt, DH:v_rows, :] = ones

    def q_tile(qi, carry):
        rows = pl.ds(pl.multiple_of(qi * tq, tq), tq)
        _nsa_q_tile(qi, g_idx, q_ref.at[rows, :], kc_ref, vct_ref, gate_ref.at[rows, :], o_ref.at[rows, :],
                    ks_scr, kw_scr, vs_scr, vw_scr, *tile_scratch, tq=tq, S=S)
        return carry

    lax.fori_loop(0, S // tq, q_tile, 0)


def _nsa_q_tile(qi, g_idx, q_ref, kc_ref, vct_ref, gate_ref, o_ref,
                ks_scr, kw_scr, vs_scr, vw_scr, gt_scr, ps_scr, sel_scr, sa_scr, sb_scr, m_scr, acc_scr, *, tq, S):
    HP, DH = NSA_HPG, NSA_HEAD_DIM
    M = HP * tq
    tk = tq
    n_sel = S // SEL_BLOCK
    slots = kc_ref.shape[0]
    v_rows = DH + BF16_SUBLANES
    t0 = qi * tq

    def per_head(x):
        return jnp.concatenate([x] * HP, axis=1)

    q_t = q_ref[...].T
    qt = jnp.concatenate([q_t[h * DH:(h + 1) * DH, :] for h in range(HP)], axis=1)

    def scores(k_scr, kt, dst):
        start = pl.multiple_of(kt * tk, tk)
        dst[...] = jnp.dot(k_scr[pl.ds(start, tk), :], qt, preferred_element_type=F32)

    sc = jnp.dot(kc_ref[...].astype(BF16), qt, preferred_element_type=F32)
    scores(ks_scr, 0, sa_scr)
    n_idx = lax.broadcasted_iota(jnp.int32, (slots, tq), 0)
    t_idx = t0 + lax.broadcasted_iota(jnp.int32, (slots, tq), 1)
    valid = per_head(jnp.where(n_idx * CMP_STRIDE + (CMP_BLOCK - 1) <= t_idx, 1.0, 0.0)) > 0.5
    sc = jnp.where(valid, sc, NEG_INF)
    mx = jnp.max(sc, axis=0, keepdims=True)
    p = jnp.where(valid, jnp.exp2(sc - mx), 0.0)
    den = jnp.sum(p, axis=0, keepdims=True)
    p_c = p * (1.0 / jnp.where(den > 0.0, den, 1.0))
    o_cmp = jnp.dot(vct_ref[...].astype(BF16), p_c.astype(BF16), preferred_element_type=F32)

    p_sum = p_c[:, 0:tq]
    for h in range(1, HP):
        p_sum = p_sum + p_c[:, h * tq:(h + 1) * tq]
    per_sel = SEL_BLOCK // CMP_STRIDE
    for c in range(tq // LANES):
        ps_scr[c] = p_sum[:, c * LANES:(c + 1) * LANES]
    every = [jnp.concatenate([ps_scr[c, pl.ds(r, n_sel, stride=per_sel), :] for c in range(tq // LANES)], axis=1)
             for r in range(per_sel)]
    j_idx = lax.broadcasted_iota(jnp.int32, (n_sel, tq), 0)
    before = jnp.where(j_idx >= 1, pltpu.roll(every[per_sel - 1], 1, axis=0), 0.0)
    imp = every[0]
    for r in range(1, per_sel - 1):
        imp = imp + every[r]
    imp = imp + 0.5 * every[per_sel - 1] + 0.5 * before
    cur = (t0 + lax.broadcasted_iota(jnp.int32, (n_sel, tq), 1)) // SEL_BLOCK
    forced = (j_idx == 0) | (j_idx == cur) | (j_idx == cur - 1)
    imp = jnp.where(forced, FORCE_SCORE, jnp.where(j_idx > cur, NEG_INF, imp))
    rank = jnp.zeros((n_sel, tq), F32)
    for i in range(n_sel):
        ci = imp[i:i + 1, :]
        ge = jnp.where(ci >= imp, 1.0, 0.0)
        gt = jnp.where(ci > imp, 1.0, 0.0)
        rank = rank + jnp.where(j_idx > i, ge, gt)
    sel_scr[...] = jnp.where(rank < float(min(SEL_TOP_N, n_sel)), 1.0, 0.0)

    per_tile = tk // SEL_BLOCK
    kloc = lax.broadcasted_iota(jnp.int32, (tk, tq), 0)
    tloc = lax.broadcasted_iota(jnp.int32, (tk, tq), 1)
    key_ahead = kloc - tloc

    def slc_bias(kt):
        hit = jnp.concatenate(
            [jnp.broadcast_to(sel_scr[pl.ds(kt * per_tile + jj, 1), :], (SEL_BLOCK, tq)) for jj in range(per_tile)],
            axis=0)
        return jnp.where((hit > 0.5) & (key_ahead <= (qi - kt) * tk), 0.0, NEG_INF)

    def absorb(v_ext, src, bias):
        s = src[...] if bias is None else src[...] + per_head(bias)
        m_old = m_scr[...]
        m_new = jnp.maximum(m_old, jnp.max(s, axis=0, keepdims=True))
        alpha = jnp.exp2(m_old - m_new)
        pr = jnp.exp2(s - m_new).astype(BF16)
        m_scr[...] = m_new
        acc_scr[...] = alpha * acc_scr[...] + jnp.dot(v_ext, pr, preferred_element_type=F32)

    def reset():
        m_scr[...] = jnp.full((1, M), NEG_INF, F32)
        acc_scr[...] = jnp.zeros((v_rows, M), F32)

    def result():
        acc = acc_scr[...]
        return acc[0:DH, :] * (1.0 / acc[DH:DH + 1, :])

    reset()
    n_tiles = qi + 1

    def slc_scores(kt, dst):
        scores(ks_scr, jnp.minimum(kt, qi), dst)

    def slc_pair(j, carry):
        slc_scores(2 * j + 1, sb_scr)
        absorb(vs_scr[2 * j], sa_scr, slc_bias(2 * j))
        slc_scores(2 * j + 2, sa_scr)
        absorb(vs_scr[2 * j + 1], sb_scr, slc_bias(2 * j + 1))
        return carry

    lax.fori_loop(0, n_tiles // 2, slc_pair, 0)

    @pl.when(n_tiles % 2 == 1)
    def _():
        absorb(vs_scr[qi], sa_scr, slc_bias(qi))

    o_slc = result()

    reset()
    n_back = WINDOW // tk
    diag_bias = jnp.where(kloc <= tloc, 0.0, NEG_INF)
    far_bias = jnp.where(kloc > tloc, 0.0, NEG_INF)

    def win_bias(d):
        return diag_bias if d == 0 else (far_bias if d == n_back else None)

    def window(n_win):
        bufs = (sa_scr, sb_scr)
        scores(kw_scr, qi, bufs[0])
        for d in range(n_win):
            if d + 1 < n_win:
                scores(kw_scr, qi - (d + 1), bufs[(d + 1) % 2])
            absorb(vw_scr[qi - d], bufs[d % 2], win_bias(d))

    for n_win in range(1, n_back + 2):
        @pl.when((qi == n_win - 1) if n_win <= n_back else (qi >= n_back))
        def _():
            window(n_win)

    o_win = result()

    gt_scr[...] = gate_ref[...].astype(F32).T

    def gate(branch):
        rows = [gt_scr[pl.ds(g_idx * (HP * N_BRANCH) + h * N_BRANCH + branch, 1), :] for h in range(HP)]
        return _sigmoid(jnp.concatenate(rows, axis=1))

    o = gate(0) * o_cmp + gate(1) * o_slc + gate(2) * o_win
    o_ref[...] = jnp.concatenate([o[:, h * tq:(h + 1) * tq].T for h in range(HP)], axis=1).astype(o_ref.dtype)


def _nsa(z, kc, vct, *, B, S, tq=256):
    G, HP, DH = NSA_KV_GROUPS, NSA_HPG, NSA_HEAD_DIM
    T = B * S
    slots = S // CMP_STRIDE
    n_sel = S // SEL_BLOCK
    n_kt = S // tq
    v_rows = DH + BF16_SUBLANES
    KVW = NSA_KV_WIDTH

    def kv_spec(which):
        return pl.BlockSpec((S, KVW), lambda b, g: (b, OFF_KV // KVW + which))

    return pl.pallas_call(
        functools.partial(_nsa_kernel, tq=tq, S=S),
        grid=(B, G),
        in_specs=[
            pl.BlockSpec((S, HP * DH), lambda b, g: (b, OFF_Q // (HP * DH) + g)),
            pl.BlockSpec((slots, DH), lambda b, g: (b * G + g, 0)),
            pl.BlockSpec((DH, slots), lambda b, g: (0, b * G + g)),
            kv_spec(0), kv_spec(1), kv_spec(2), kv_spec(3),
            pl.BlockSpec((S, LANES), lambda b, g: (b, OFF_NG // LANES)),
        ],
        out_specs=pl.BlockSpec((S, HP * DH), lambda b, g: (b, g)),
        out_shape=jax.ShapeDtypeStruct((T, NSA_Q_WIDTH), BF16),
        scratch_shapes=[
            pltpu.VMEM((S, DH), BF16),
            pltpu.VMEM((S, DH), BF16),
            pltpu.VMEM((n_kt, v_rows, tq), BF16),
            pltpu.VMEM((n_kt, v_rows, tq), BF16),
            pltpu.VMEM((LANES, tq), F32),
            pltpu.VMEM((tq // LANES, slots, LANES), F32),
            pltpu.VMEM((n_sel, tq), F32),
            pltpu.VMEM((tq, HP * tq), F32),
            pltpu.VMEM((tq, HP * tq), F32),
            pltpu.VMEM((1, HP * tq), F32),
            pltpu.VMEM((v_rows, HP * tq), F32),
        ],
        compiler_params=_cparams(("parallel", "parallel")),
        name="nsa_attention",
    )(z, kc, vct, z, z, z, z, z)


def _merge_kernel(ya_ref, yb_ref, yc_ref, mga_ref, mgb_ref, mgc_ref, x_ref, wa_ref, wb_ref, wc_ref, wo_ref,
                  gpost_ref, gpre_ref, x1_ref, hf_ref):
    gate = lambda ref: _sigmoid(ref[...].astype(F32))
    merged = gate(mga_ref) * jnp.dot(ya_ref[...], wa_ref[...], preferred_element_type=F32)
    merged = merged + gate(mgb_ref) * jnp.dot(yb_ref[...], wb_ref[...], preferred_element_type=F32)
    merged = merged + gate(mgc_ref) * jnp.dot(yc_ref[...], wc_ref[...], preferred_element_type=F32)
    y = jnp.dot(merged.astype(BF16), wo_ref[...], preferred_element_type=F32)
    x1 = x_ref[...] + _rms(y, gpost_ref[...])
    x1_ref[...] = x1
    hf_ref[...] = _rms(x1, gpre_ref[...]).astype(BF16)


def _merge(ya, yb, yc, z, x2d, wa, wb, wc, wo, layer, gpost, gpre, *, tm=512):
    T, D = x2d.shape
    row = lambda c: pl.BlockSpec((tm, D), lambda i: (i, c))
    wfull = pl.BlockSpec((None, D, D), lambda i: (layer, 0, 0))
    vec = pl.BlockSpec((1, D), lambda i: (0, 0))
    mg0 = OFF_MG // D
    return pl.pallas_call(
        _merge_kernel,
        grid=(T // tm,),
        in_specs=[row(0), row(0), row(0), row(mg0), row(mg0 + 1), row(mg0 + 2), row(0),
                  wfull, wfull, wfull, wfull, vec, vec],
        out_specs=[row(0), row(0)],
        out_shape=[jax.ShapeDtypeStruct((T, D), F32), jax.ShapeDtypeStruct((T, D), BF16)],
        compiler_params=_cparams(("parallel",)),
        name="merge",
    )(ya, yb, yc, z, z, z, x2d, wa, wb, wc, wo, gpost, gpre)


def _ffn_kernel(hf_ref, x1_ref, win_ref, wout_ref, gpost_ref, o_ref, acc_scr, *, bounds):
    hf = hf_ref[...]
    for c, (lo, hi) in enumerate(zip(bounds[:-1], bounds[1:])):
        gate = jnp.dot(hf, win_ref[:, lo:hi], preferred_element_type=F32)
        up = jnp.dot(hf, win_ref[:, D_FF + lo:D_FF + hi], preferred_element_type=F32)
        act = (gate * _sigmoid(gate) * up).astype(BF16)
        part = jnp.dot(act, wout_ref[lo:hi, :], preferred_element_type=F32)
        if c == 0:
            acc_scr[...] = part
        else:
            acc_scr[...] += part
    o_ref[...] = x1_ref[...] + _rms(acc_scr[...], gpost_ref[...])


def _ffn(hf, x1, win, wout, layer, gpost, *, tm=512):
    T, D = x1.shape
    blocks = D_FF // V7X_MXU_WIDTH
    bounds = (0, (blocks + 1) // 2 * V7X_MXU_WIDTH, D_FF)
    row = pl.BlockSpec((tm, D), lambda i: (i, 0))
    return pl.pallas_call(
        functools.partial(_ffn_kernel, bounds=bounds),
        grid=(T // tm,),
        in_specs=[row, row,
                  pl.BlockSpec((None,) + win.shape[1:], lambda i: (layer, 0, 0), pipeline_mode=pl.Buffered(1)),
                  pl.BlockSpec((None,) + wout.shape[1:], lambda i: (layer, 0, 0), pipeline_mode=pl.Buffered(1)),
                  pl.BlockSpec((1, D), lambda i: (0, 0))],
        out_specs=row,
        out_shape=jax.ShapeDtypeStruct((T, D), F32),
        scratch_shapes=[pltpu.VMEM((tm, D), F32)],
        compiler_params=_cparams(("parallel",)),
        name="ffn",
    )(hf, x1, win, wout, gpost)


_SRC = [int(c) for c in np.cumsum([0, GMLP_WIDTH, GMLP_WIDTH, NSA_Q_WIDTH, 6 * NSA_KV_WIDTH, N_BRANCH * NSA_HEADS,
                                   RNN_WIDTH, RNN_WIDTH, N_BRANCH * D_MODEL])]


def _reorder_w_in_kernel(w_ref, o_ref):
    u0, _, q0, kv0, ng0, xr0, _, _, end = _SRC
    rest0 = kv0 + 2 * NSA_KV_WIDTH

    def put(dst, lo, hi, scale=None):
        x = w_ref[lo:hi, :]
        if scale is not None:
            x = x * scale
        o_ref[dst:dst + hi - lo, :] = x.astype(o_ref.dtype)

    put(OFF_U, u0, q0)
    put(OFF_XR, xr0, end)
    put(OFF_KC, kv0, rest0)
    put(OFF_NG, ng0, xr0)
    pad0 = OFF_NG + xr0 - ng0
    o_ref[pad0:OFF_Q, :] = jnp.zeros((OFF_Q - pad0, o_ref.shape[1]), o_ref.dtype)
    put(OFF_Q, q0, kv0, Q_SCALE)
    put(OFF_KV, rest0, ng0)


def _reorder_w_in(w_in, *, td=256):
    L, D, d_in = w_in.shape
    return pl.pallas_call(
        _reorder_w_in_kernel,
        grid=(L, D // td),
        in_specs=[pl.BlockSpec((None, d_in, td), lambda l, i: (l, 0, i))],
        out_specs=pl.BlockSpec((None, D_IN_PAD, td), lambda l, i: (l, 0, i)),
        out_shape=jax.ShapeDtypeStruct((L, D_IN_PAD, D), BF16),
        compiler_params=_cparams(("parallel", "parallel")),
        name="reorder_w_in",
    )(jnp.swapaxes(w_in, 1, 2))


def _block_diag_gates(wa, wx):
    per = RNN_BLOCK // RNN_HEAD_DIM
    nblk = RNN_HEADS // per
    eye = jnp.eye(per, dtype=wa.dtype)

    def bd(w):
        w = w.reshape(nblk, per, RNN_HEAD_DIM, RNN_HEAD_DIM)
        return jnp.einsum('kpio,pq->kpiqo', w, eye).reshape(nblk, RNN_BLOCK, RNN_BLOCK)

    return jnp.concatenate([bd(wa), bd(wx)], axis=-1).astype(BF16)


def _layer(x2d, B, S, wt_in_all, dense_all, layer, g_pre_mix, g_post_mix, g_pre_ffn, g_post_ffn,
           gmlp_ln_g, gmlp_ln_b, gmlp_ws, gmlp_bs,
           nsa_pe_k, nsa_pe_v, nsa_wk1, nsa_wk2, nsa_wv1, nsa_wv2,
           rnn_conv_w, rnn_conv_b, rnn_wa, rnn_ba, rnn_wx, rnn_bx, rnn_lam):
    row = lambda a: a.reshape(1, -1)
    z = _in_proj(x2d, row(g_pre_mix), wt_in_all, layer)

    y_a = _gmlp(z, row(gmlp_ln_g), row(gmlp_ln_b), gmlp_ws, gmlp_bs.T)
    y_c = _rglru(z, rnn_conv_w, row(rnn_conv_b), _block_diag_gates(rnn_wa, rnn_wx),
                 row(rnn_ba), row(rnn_bx), row(rnn_lam), B=B, S=S)

    kc, vct = _compress(z, nsa_pe_k.reshape(1, -1), nsa_pe_v.reshape(1, -1), nsa_wk1.astype(BF16),
                        nsa_wk2.astype(BF16), nsa_wv1.astype(BF16), nsa_wv2.T.astype(BF16), B=B, S=S)
    y_b = _nsa(z, kc, vct, B=B, S=S)

    x1, hf = _merge(y_a, y_b, y_c, z, x2d, *dense_all[:4], layer, row(g_post_mix), row(g_pre_ffn))
    return _ffn(hf, x1, *dense_all[4:], layer, row(g_post_ffn))


def kernel(x, g_pre_mix, g_post_mix, g_pre_ffn, g_post_ffn, w_in, gmlp_ln_g, gmlp_ln_b, gmlp_ws, gmlp_bs, nsa_pe_k, nsa_pe_v, nsa_wk1, nsa_wk2, nsa_wv1, nsa_wv2, rnn_conv_w, rnn_conv_b, rnn_wa, rnn_ba, rnn_wx, rnn_bx, rnn_lam, w_br_a, w_br_b, w_br_c, w_o, w_ffn_in, w_ffn_out):
    B, S, D = x.shape
    params = (g_pre_mix, g_post_mix, g_pre_ffn, g_post_ffn, gmlp_ln_g, gmlp_ln_b, gmlp_ws, gmlp_bs,
              nsa_pe_k, nsa_pe_v, nsa_wk1, nsa_wk2, nsa_wv1, nsa_wv2,
              rnn_conv_w, rnn_conv_b, rnn_wa, rnn_ba, rnn_wx, rnn_bx, rnn_lam)
    wt_in_all = _reorder_w_in(w_in)
    dense_all = tuple(w.astype(BF16) for w in (w_br_a, w_br_b, w_br_c, w_o, w_ffn_in, w_ffn_out))
    x2d = x.reshape(B * S, D)
    for l in range(w_in.shape[0]):
        x2d = _layer(x2d, B, S, wt_in_all, dense_all, l, *(p[l] for p in params))
    return x2d.reshape(B, S, D)
```

```python
import functools

import jax
import jax.numpy as jnp
import numpy as np
from jax import lax
from jax.experimental import pallas as pl
from jax.experimental.pallas import tpu as pltpu

F32 = jnp.float32
BF16 = jnp.bfloat16

EPS = 1e-6
NEG_INF = -1e30
FORCE_SCORE = 1e4

D_MODEL = 1024
GMLP_WIDTH = 1024
GMLP_GROUPS = 4
GMLP_GROUP_DIM = GMLP_WIDTH // GMLP_GROUPS
GMLP_CHUNK = 128

NSA_HEADS = 16
NSA_KV_GROUPS = 4
NSA_HEAD_DIM = 64
NSA_HPG = NSA_HEADS // NSA_KV_GROUPS
NSA_Q_WIDTH = NSA_HEADS * NSA_HEAD_DIM
NSA_KV_WIDTH = NSA_KV_GROUPS * NSA_HEAD_DIM
N_BRANCH = 3
CMP_BLOCK = 32
CMP_STRIDE = 16
CMP_HIDDEN = 256
SEL_BLOCK = 64
SEL_TOP_N = 16
WINDOW = 512
Q_SCALE = NSA_HEAD_DIM ** -0.5 * float(np.log2(np.e))

RNN_WIDTH = 1024
RNN_HEADS = 16
RNN_HEAD_DIM = RNN_WIDTH // RNN_HEADS
CONV_WIDTH = 4
LRU_C = 8.0
RNN_BLOCK = 256

D_FF = 2816

OFF_U = 0
OFF_V = 1024
OFF_XR = 2048
OFF_RG = 3072
OFF_MG = 4096
OFF_KC = 7168
OFF_NG = OFF_KC + 2 * NSA_KV_WIDTH
NG_PAD = 512
OFF_Q = OFF_NG + NG_PAD
OFF_KV = OFF_Q + NSA_Q_WIDTH
D_IN_PAD = OFF_KV + 4 * NSA_KV_WIDTH

LANES = 128
V7X_MXU_WIDTH = 256
BF16_SUBLANES = 16
V7X_VMEM_LIMIT = 56 * 1024 * 1024


def _cparams(sem, vmem=V7X_VMEM_LIMIT):
    return pltpu.CompilerParams(dimension_semantics=sem, vmem_limit_bytes=vmem)


def _rms(x, g):
    ms = jnp.mean(x * x, axis=-1, keepdims=True)
    return x * lax.rsqrt(ms + EPS) * g


def _gelu(x):
    return jax.nn.gelu(x)


def _sigmoid(x):
    return 0.5 * jnp.tanh(0.5 * x) + 0.5


_NT = (((1,), (1,)), ((), ()))


def _in_proj_kernel(x_ref, g_ref, w_ref, o_ref, h_scr):
    @pl.when(pl.program_id(1) == 0)
    def _():
        h_scr[...] = _rms(x_ref[...], g_ref[...]).astype(BF16)

    o_ref[...] = lax.dot_general(h_scr[...], w_ref[...], _NT, preferred_element_type=F32).astype(o_ref.dtype)


def _in_proj(x2d, g, wt_all, layer, *, tm=1024, tn=2048):
    T, D = x2d.shape
    N = wt_all.shape[1]
    return pl.pallas_call(
        _in_proj_kernel,
        grid=(T // tm, N // tn),
        in_specs=[
            pl.BlockSpec((tm, D), lambda i, j: (i, 0)),
            pl.BlockSpec((1, D), lambda i, j: (0, 0)),
            pl.BlockSpec((None, tn, D), lambda i, j: (layer, j, 0)),
        ],
        out_specs=pl.BlockSpec((tm, tn), lambda i, j: (i, j)),
        out_shape=jax.ShapeDtypeStruct((T, N), BF16),
        scratch_shapes=[pltpu.VMEM((tm, D), BF16)],
        compiler_params=_cparams(("parallel", "arbitrary")),
        name="in_proj",
    )(x2d, g, wt_all)


def _gmlp_kernel(u_ref, v_ref, lng_ref, lnb_ref, ws_ref, bst_ref, o_ref, *, n_chunks):
    C = GMLP_CHUNK
    row = lax.broadcasted_iota(jnp.int32, (C, C), 0)
    col = lax.broadcasted_iota(jnp.int32, (C, C), 1)
    causal = col <= row
    ws = [jnp.where(causal, ws_ref[g], 0.0).astype(BF16) for g in range(GMLP_GROUPS)]
    bst = bst_ref[...]
    for c in range(n_chunks):
        rows = slice(c * C, (c + 1) * C)
        gv = _gelu(v_ref[rows, :].astype(F32))
        mu = jnp.mean(gv, axis=-1, keepdims=True)
        d = gv - mu
        var = jnp.mean(d * d, axis=-1, keepdims=True)
        vn = (d * lax.rsqrt(var + EPS) * lng_ref[...] + lnb_ref[...]).astype(BF16)
        for g in range(GMLP_GROUPS):
            cols = slice(g * GMLP_GROUP_DIM, (g + 1) * GMLP_GROUP_DIM)
            mixed = jnp.dot(ws[g], vn[:, cols], preferred_element_type=F32) + bst[:, g:g + 1]
            o_ref[rows, cols] = (_gelu(u_ref[rows, cols].astype(F32)) * mixed).astype(o_ref.dtype)


def _gmlp(z, lng, lnb, ws, bst, *, n_chunks=8):
    T = z.shape[0]
    tm = GMLP_CHUNK * n_chunks
    W = GMLP_WIDTH
    return pl.pallas_call(
        functools.partial(_gmlp_kernel, n_chunks=n_chunks),
        grid=(T // tm,),
        in_specs=[
            pl.BlockSpec((tm, W), lambda i: (i, OFF_U // W)),
            pl.BlockSpec((tm, W), lambda i: (i, OFF_V // W)),
            pl.BlockSpec((1, W), lambda i: (0, 0)),
            pl.BlockSpec((1, W), lambda i: (0, 0)),
            pl.BlockSpec((GMLP_GROUPS, GMLP_CHUNK, GMLP_CHUNK), lambda i: (0, 0, 0)),
            pl.BlockSpec((GMLP_CHUNK, GMLP_GROUPS), lambda i: (0, 0)),
        ],
        out_specs=pl.BlockSpec((tm, W), lambda i: (i, 0)),
        out_shape=jax.ShapeDtypeStruct((T, W), BF16),
        compiler_params=_cparams(("parallel",)),
        name="gmlp",
    )(z, z, lng, lnb, ws, bst)


def _rglru_kernel(xr_ref, rg_ref, cw_ref, cb_ref, wab_ref, ba_ref, bx_ref, lam_ref, o_ref,
                  ext_scr, a_scr, b_scr, hc_scr, *, tt):
    W = RNN_WIDTH
    groups = tt // 8
    sub_row = lax.broadcasted_iota(jnp.int32, (groups, 8, RNN_BLOCK), 1)

    @pl.when(pl.program_id(1) == 0)
    def _():
        ext_scr[0:8, :] = jnp.zeros((8, W), F32)
        hc_scr[...] = jnp.zeros((8, W), F32)

    xr = xr_ref[...].astype(F32)
    ext_scr[8:8 + tt, :] = xr
    cw = cw_ref[...]
    xc = (cw[3:4] * xr + cw[2:3] * ext_scr[7:7 + tt, :] + cw[1:2] * ext_scr[6:6 + tt, :]
          + cw[0:1] * ext_scr[5:5 + tt, :] + cb_ref[...])
    ext_scr[0:8, :] = xr[tt - 8:tt, :]

    lam = lam_ref[...]
    neg = -lam
    softplus = jnp.maximum(neg, 0.0) + jnp.log1p(jnp.exp(-jnp.abs(neg)))
    for k in range(W // RNN_BLOCK):
        cols = slice(k * RNN_BLOCK, (k + 1) * RNN_BLOCK)
        xck = xc[:, cols]
        gates = jnp.dot(xck.astype(BF16), wab_ref[k], preferred_element_type=F32)
        r = _sigmoid(gates[:, :RNN_BLOCK] + ba_ref[:, cols])
        i = _sigmoid(gates[:, RNN_BLOCK:] + bx_ref[:, cols])
        log_a = -LRU_C * r * softplus[:, cols]
        a = jnp.exp(log_a)
        one_minus_a2 = -jnp.tanh(log_a) * (1.0 + a * a)
        b_in = jnp.sqrt(one_minus_a2) * (i * xck)
        a3 = a.reshape(groups, 8, RNN_BLOCK)
        b3 = b_in.reshape(groups, 8, RNN_BLOCK)
        for step in (1, 2, 4):
            keep = sub_row >= step
            a_prev = jnp.where(keep, pltpu.roll(a3, step, axis=1), 1.0)
            b_prev = jnp.where(keep, pltpu.roll(b3, step, axis=1), 0.0)
            b3 = a3 * b_prev + b3
            a3 = a3 * a_prev
        a_scr[:, cols] = a3.reshape(tt, RNN_BLOCK)
        b_scr[:, cols] = b3.reshape(tt, RNN_BLOCK)

    def carry_step(i, h_prev):
        base = pl.multiple_of(i * BF16_SUBLANES, BF16_SUBLANES)
        gate = _gelu(rg_ref[pl.ds(base, BF16_SUBLANES), :].astype(F32))
        ys = []
        for half in range(BF16_SUBLANES // 8):
            rows = pl.ds(base + half * 8, 8)
            h = b_scr[rows, :] + a_scr[rows, :] * h_prev
            ys.append(h * gate[half * 8:(half + 1) * 8, :])
            h_prev = h[7:8, :]
        o_ref[pl.ds(base, BF16_SUBLANES), :] = jnp.concatenate(ys, axis=0).astype(o_ref.dtype)
        return h_prev

    h_last = lax.fori_loop(0, tt // BF16_SUBLANES, carry_step, hc_scr[0:1, :])
    hc_scr[...] = jnp.broadcast_to(h_last, (8, W))


def _rglru(z, cw, cb, wab, ba, bx, lam, *, B, S, tt=512):
    T = z.shape[0]
    W = RNN_WIDTH
    nt = S // tt
    return pl.pallas_call(
        functools.partial(_rglru_kernel, tt=tt),
        grid=(B, nt),
        in_specs=[
            pl.BlockSpec((tt, W), lambda b, t: (b * nt + t, OFF_XR // W)),
            pl.BlockSpec((tt, W), lambda b, t: (b * nt + t, OFF_RG // W)),
            pl.BlockSpec((CONV_WIDTH, W), lambda b, t: (0, 0)),
            pl.BlockSpec((1, W), lambda b, t: (0, 0)),
            pl.BlockSpec((W // RNN_BLOCK, RNN_BLOCK, 2 * RNN_BLOCK), lambda b, t: (0, 0, 0)),
            pl.BlockSpec((1, W), lambda b, t: (0, 0)),
            pl.BlockSpec((1, W), lambda b, t: (0, 0)),
            pl.BlockSpec((1, W), lambda b, t: (0, 0)),
        ],
        out_specs=pl.BlockSpec((tt, W), lambda b, t: (b * nt + t, 0)),
        out_shape=jax.ShapeDtypeStruct((T, W), BF16),
        scratch_shapes=[
            pltpu.VMEM((tt + 8, W), F32),
            pltpu.VMEM((tt, W), F32),
            pltpu.VMEM((tt, W), F32),
            pltpu.VMEM((8, W), F32),
        ],
        compiler_params=_cparams(("parallel", "arbitrary")),
        name="rglru",
    )(z, z, cw, cb, wab, ba, bx, lam)


def _compress_hidden(x_ref, pe_ref, w1_ref, x_scr):
    DH = NSA_HEAD_DIM
    half = CMP_STRIDE * DH
    slots = x_ref.shape[0] // CMP_STRIDE
    x = x_ref[...].astype(F32)
    for c in range(NSA_KV_WIDTH // LANES):
        x_scr[c] = x[:, c * LANES:(c + 1) * LANES]
    phase = [[x_scr[c, pl.ds(l, slots, stride=CMP_STRIDE), :] for c in range(NSA_KV_WIDTH // LANES)]
             for l in range(CMP_STRIDE)]
    per_block = LANES // DH

    def group_rows(g):
        lanes = slice((g % per_block) * DH, (g % per_block + 1) * DH)
        return jnp.concatenate([phase[l][g // per_block][:, lanes] for l in range(CMP_STRIDE)], axis=1)

    hm = jnp.concatenate([group_rows(g) for g in range(NSA_KV_GROUPS)], axis=0).astype(BF16)
    rows = hm.shape[0]
    lo = jnp.dot(hm, w1_ref[0:half, :], preferred_element_type=F32)
    hi = jnp.dot(hm, w1_ref[half:2 * half, :], preferred_element_type=F32)
    pe = jnp.broadcast_to(pe_ref[...], (8, 2 * half)).astype(BF16)
    pe_term = jnp.dot(pe, w1_ref[...], preferred_element_type=F32)[0:1, :]
    return _gelu(lo + pltpu.roll(hi, rows - 1, axis=0) + pe_term).astype(BF16)


def _compress_kernel(xk_ref, xv_ref, pek_ref, pev_ref, wk1_ref, wk2_ref, wv1_ref, wv2t_ref, kc_ref, vct_ref, x_scr):
    kc_ref[...] = jnp.dot(_compress_hidden(xk_ref, pek_ref, wk1_ref, x_scr), wk2_ref[...],
                          preferred_element_type=F32)
    vct_ref[...] = lax.dot_general(wv2t_ref[...], _compress_hidden(xv_ref, pev_ref, wv1_ref, x_scr), _NT,
                                   preferred_element_type=F32)


def _compress(z, pek, pev, wk1, wk2, wv1, wv2t, *, B, S):
    G, DH = NSA_KV_GROUPS, NSA_HEAD_DIM
    slots = S // CMP_STRIDE
    KVW = NSA_KV_WIDTH
    full = lambda shape: pl.BlockSpec(shape, lambda b: (0,) * len(shape))
    return pl.pallas_call(
        _compress_kernel,
        grid=(B,),
        in_specs=[pl.BlockSpec((S, KVW), lambda b: (b, OFF_KC // KVW)),
                  pl.BlockSpec((S, KVW), lambda b: (b, OFF_KC // KVW + 1)),
                  full(pek.shape), full(pev.shape), full(wk1.shape), full(wk2.shape), full(wv1.shape),
                  full(wv2t.shape)],
        out_specs=[pl.BlockSpec((G * slots, DH), lambda b: (b, 0)), pl.BlockSpec((DH, G * slots), lambda b: (0, b))],
        out_shape=[jax.ShapeDtypeStruct((B * G * slots, DH), F32), jax.ShapeDtypeStruct((DH, B * G * slots), F32)],
        scratch_shapes=[pltpu.VMEM((KVW // LANES, S, LANES), F32)],
        compiler_params=_cparams(("parallel",)),
        name="nsa_compress",
    )(z, z, pek, pev, wk1, wk2, wv1, wv2t)


def _nsa_kernel(q_ref, kc_ref, vct_ref, ks_ref, vs_ref, kw_ref, vw_ref, gate_ref, o_ref,
                ks_scr, kw_scr, vs_scr, vw_scr, *tile_scratch, tq, S):
    DH = NSA_HEAD_DIM
    tk = tq
    v_rows = DH + BF16_SUBLANES
    g_idx = pl.program_id(1)

    for g in range(NSA_KV_GROUPS):
        @pl.when(g_idx == g)
        def _():
            lanes = slice(g * DH, (g + 1) * DH)
            ks_scr[...] = ks_ref[:, lanes]
            kw_scr[...] = kw_ref[:, lanes]
            ones = jnp.ones((BF16_SUBLANES, tk), BF16)
            for v_ref, v_scr in ((vs_ref, vs_scr), (vw_ref, vw_scr)):
                vt = v_ref[:, lanes].T
                for kt in range(S // tk):
                    v_scr[kt, 0:DH, :] = vt[:, kt * tk:(kt + 1) * tk]
                    v_scr[kt, DH:v_rows, :] = ones

    def q_tile(qi, carry):
        rows = pl.ds(pl.multiple_of(qi * tq, tq), tq)
        _nsa_q_tile(qi, g_idx, q_ref.at[rows, :], kc_ref, vct_ref, gate_ref.at[rows, :], o_ref.at[rows, :],
                    ks_scr, kw_scr, vs_scr, vw_scr, *tile_scratch, tq=tq, S=S)
        return carry

    lax.fori_loop(0, S // tq, q_tile, 0)


def _nsa_q_tile(qi, g_idx, q_ref, kc_ref, vct_ref, gate_ref, o_ref,
                ks_scr, kw_scr, vs_scr, vw_scr, gt_scr, ps_scr, sel_scr, sa_scr, sb_scr, m_scr, acc_scr, *, tq, S):
    HP, DH = NSA_HPG, NSA_HEAD_DIM
    M = HP * tq
    tk = tq
    n_sel = S // SEL_BLOCK
    slots = kc_ref.shape[0]
    v_rows = DH + BF16_SUBLANES
    t0 = qi * tq

    def per_head(x):
        return jnp.concatenate([x] * HP, axis=1)

    q_t = q_ref[...].T
    qt = jnp.concatenate([q_t[h * DH:(h + 1) * DH, :] for h in range(HP)], axis=1)

    def scores(k_scr, kt, dst):
        start = pl.multiple_of(kt * tk, tk)
        dst[:, 0:M] = jnp.dot(k_scr[pl.ds(start, tk), :], qt, preferred_element_type=F32)

    sc = jnp.dot(kc_ref[...].astype(BF16), qt, preferred_element_type=F32)
    scores(ks_scr, 0, sa_scr)
    n_idx = lax.broadcasted_iota(jnp.int32, (slots, tq), 0)
    t_idx = t0 + lax.broadcasted_iota(jnp.int32, (slots, tq), 1)
    valid = per_head(jnp.where(n_idx * CMP_STRIDE + (CMP_BLOCK - 1) <= t_idx, 1.0, 0.0)) > 0.5
    sc = jnp.where(valid, sc, NEG_INF)
    mx = jnp.max(sc, axis=0, keepdims=True)
    p = jnp.where(valid, jnp.exp2(sc - mx), 0.0)
    den = jnp.sum(p, axis=0, keepdims=True)
    p_c = p * (1.0 / jnp.where(den > 0.0, den, 1.0))
    o_cmp = jnp.dot(vct_ref[...].astype(BF16), p_c.astype(BF16), preferred_element_type=F32)

    p_sum = p_c[:, 0:tq]
    for h in range(1, HP):
        p_sum = p_sum + p_c[:, h * tq:(h + 1) * tq]
    per_sel = SEL_BLOCK // CMP_STRIDE
    for c in range(tq // LANES):
        ps_scr[c] = p_sum[:, c * LANES:(c + 1) * LANES]
    every = [jnp.concatenate([ps_scr[c, pl.ds(r, n_sel, stride=per_sel), :] for c in range(tq // LANES)], axis=1)
             for r in range(per_sel)]
    j_idx = lax.broadcasted_iota(jnp.int32, (n_sel, tq), 0)
    before = jnp.where(j_idx >= 1, pltpu.roll(every[per_sel - 1], 1, axis=0), 0.0)
    imp = every[0]
    for r in range(1, per_sel - 1):
        imp = imp + every[r]
    imp = imp + 0.5 * every[per_sel - 1] + 0.5 * before
    cur = (t0 + lax.broadcasted_iota(jnp.int32, (n_sel, tq), 1)) // SEL_BLOCK
    forced = (j_idx == 0) | (j_idx == cur) | (j_idx == cur - 1)
    imp = jnp.where(forced, FORCE_SCORE, jnp.where(j_idx > cur, NEG_INF, imp))
    rank = jnp.zeros((n_sel, tq), F32)
    for i in range(n_sel):
        ci = imp[i:i + 1, :]
        ge = jnp.where(ci >= imp, 1.0, 0.0)
        gt = jnp.where(ci > imp, 1.0, 0.0)
        rank = rank + jnp.where(j_idx > i, ge, gt)
    sel_scr[...] = jnp.where(rank < float(min(SEL_TOP_N, n_sel)), 1.0, 0.0)

    per_tile = tk // SEL_BLOCK
    kloc = lax.broadcasted_iota(jnp.int32, (tk, tq), 0)
    tloc = lax.broadcasted_iota(jnp.int32, (tk, tq), 1)
    key_ahead = kloc - tloc

    def slc_bias(kt):
        hit = jnp.concatenate(
            [jnp.broadcast_to(sel_scr[pl.ds(kt * per_tile + jj, 1), :], (SEL_BLOCK, tq)) for jj in range(per_tile)],
            axis=0)
        return jnp.where((hit > 0.5) & (key_ahead <= (qi - kt) * tk), 0.0, NEG_INF)

    def absorb(v_ext, src, bias):
        s = src[:, 0:M] if bias is None else src[:, 0:M] + per_head(bias)
        m_old = m_scr[...]
        m_new = jnp.maximum(m_old, jnp.max(s, axis=0, keepdims=True))
        alpha = jnp.exp2(m_old - m_new)
        pr = jnp.exp2(s - m_new).astype(BF16)
        m_scr[...] = m_new
        acc_scr[...] = alpha * acc_scr[...] + jnp.dot(v_ext, pr, preferred_element_type=F32)

    def reset():
        m_scr[...] = jnp.full((1, M), NEG_INF, F32)
        acc_scr[...] = jnp.zeros((v_rows, M), F32)

    def result():
        acc = acc_scr[...]
        return acc[0:DH, :] * (1.0 / acc[DH:DH + 1, :])

    reset()
    n_tiles = qi + 1

    def slc_scores(kt, dst):
        scores(ks_scr, jnp.minimum(kt, qi), dst)

    def slc_pair(j, carry):
        slc_scores(2 * j + 1, sb_scr)
        absorb(vs_scr[2 * j], sa_scr, slc_bias(2 * j))
        slc_scores(2 * j + 2, sa_scr)
        absorb(vs_scr[2 * j + 1], sb_scr, slc_bias(2 * j + 1))
        return carry

    lax.fori_loop(0, n_tiles // 2, slc_pair, 0)

    @pl.when(n_tiles % 2 == 1)
    def _():
        absorb(vs_scr[qi], sa_scr, slc_bias(qi))

    o_slc = result()

    reset()
    n_back = WINDOW // tk
    diag_bias = jnp.where(kloc <= tloc, 0.0, NEG_INF)
    far_bias = jnp.where(kloc > tloc, 0.0, NEG_INF)

    def win_bias(d):
        return diag_bias if d == 0 else (far_bias if d == n_back else None)

    def window(n_win):
        bufs = (sa_scr, sb_scr)
        scores(kw_scr, qi, bufs[0])
        for d in range(n_win):
            if d + 1 < n_win:
                scores(kw_scr, qi - (d + 1), bufs[(d + 1) % 2])
            absorb(vw_scr[qi - d], bufs[d % 2], win_bias(d))

    for n_win in range(1, n_back + 2):
        @pl.when((qi == n_win - 1) if n_win <= n_back else (qi >= n_back))
        def _():
            window(n_win)

    o_win = result()

    gt_scr[...] = gate_ref[...].astype(F32).T

    def gate(branch):
        rows = [gt_scr[pl.ds(g_idx * (HP * N_BRANCH) + h * N_BRANCH + branch, 1), :] for h in range(HP)]
        return _sigmoid(jnp.concatenate(rows, axis=1))

    o = gate(0) * o_cmp + gate(1) * o_slc + gate(2) * o_win
    o_ref[...] = jnp.concatenate([o[:, h * tq:(h + 1) * tq].T for h in range(HP)], axis=1).astype(o_ref.dtype)


def _nsa(z, kc, vct, *, B, S, tq=256):
    G, HP, DH = NSA_KV_GROUPS, NSA_HPG, NSA_HEAD_DIM
    T = B * S
    slots = S // CMP_STRIDE
    n_sel = S // SEL_BLOCK
    n_kt = S // tq
    v_rows = DH + BF16_SUBLANES
    KVW = NSA_KV_WIDTH

    def kv_spec(which):
        return pl.BlockSpec((S, KVW), lambda b, g: (b, OFF_KV // KVW + which))

    return pl.pallas_call(
        functools.partial(_nsa_kernel, tq=tq, S=S),
        grid=(B, G),
        in_specs=[
            pl.BlockSpec((S, HP * DH), lambda b, g: (b, OFF_Q // (HP * DH) + g)),
            pl.BlockSpec((slots, DH), lambda b, g: (b * G + g, 0)),
            pl.BlockSpec((DH, slots), lambda b, g: (0, b * G + g)),
            kv_spec(0), kv_spec(1), kv_spec(2), kv_spec(3),
            pl.BlockSpec((S, LANES), lambda b, g: (b, OFF_NG // LANES)),
        ],
        out_specs=pl.BlockSpec((S, HP * DH), lambda b, g: (b, g)),
        out_shape=jax.ShapeDtypeStruct((T, NSA_Q_WIDTH), BF16),
        scratch_shapes=[
            pltpu.VMEM((S, DH), BF16),
            pltpu.VMEM((S, DH), BF16),
            pltpu.VMEM((n_kt, v_rows, tq), BF16),
            pltpu.VMEM((n_kt, v_rows, tq), BF16),
            pltpu.VMEM((LANES, tq), F32),
            pltpu.VMEM((tq // LANES, slots, LANES), F32),
            pltpu.VMEM((n_sel, tq), F32),
            pltpu.VMEM((tq, HP * tq + LANES), F32),
            pltpu.VMEM((tq, HP * tq + LANES), F32),
            pltpu.VMEM((1, HP * tq), F32),
            pltpu.VMEM((v_rows, HP * tq), F32),
        ],
        compiler_params=_cparams(("parallel", "parallel")),
        name="nsa_attention",
    )(z, kc, vct, z, z, z, z, z)


def _merge_kernel(ya_ref, yb_ref, yc_ref, mga_ref, mgb_ref, mgc_ref, x_ref, wa_ref, wb_ref, wc_ref, wo_ref,
                  gpost_ref, gpre_ref, x1_ref, hf_ref):
    gate = lambda ref: _sigmoid(ref[...].astype(F32))
    merged = gate(mga_ref) * jnp.dot(ya_ref[...], wa_ref[...], preferred_element_type=F32)
    merged = merged + gate(mgb_ref) * jnp.dot(yb_ref[...], wb_ref[...], preferred_element_type=F32)
    merged = merged + gate(mgc_ref) * jnp.dot(yc_ref[...], wc_ref[...], preferred_element_type=F32)
    y = jnp.dot(merged.astype(BF16), wo_ref[...], preferred_element_type=F32)
    x1 = x_ref[...] + _rms(y, gpost_ref[...])
    x1_ref[...] = x1
    hf_ref[...] = _rms(x1, gpre_ref[...]).astype(BF16)


def _merge(ya, yb, yc, z, x2d, wa, wb, wc, wo, layer, gpost, gpre, *, tm=512):
    T, D = x2d.shape
    row = lambda c: pl.BlockSpec((tm, D), lambda i: (i, c))
    wfull = pl.BlockSpec((None, D, D), lambda i: (layer, 0, 0))
    vec = pl.BlockSpec((1, D), lambda i: (0, 0))
    mg0 = OFF_MG // D
    return pl.pallas_call(
        _merge_kernel,
        grid=(T // tm,),
        in_specs=[row(0), row(0), row(0), row(mg0), row(mg0 + 1), row(mg0 + 2), row(0),
                  wfull, wfull, wfull, wfull, vec, vec],
        out_specs=[row(0), row(0)],
        out_shape=[jax.ShapeDtypeStruct((T, D), F32), jax.ShapeDtypeStruct((T, D), BF16)],
        compiler_params=_cparams(("parallel",)),
        name="merge",
    )(ya, yb, yc, z, z, z, x2d, wa, wb, wc, wo, gpost, gpre)


def _ffn_kernel(hf_ref, x1_ref, win_ref, wout_ref, gpost_ref, o_ref, acc_scr, *, bounds):
    hf = hf_ref[...]
    for c, (lo, hi) in enumerate(zip(bounds[:-1], bounds[1:])):
        gate = jnp.dot(hf, win_ref[:, lo:hi], preferred_element_type=F32)
        up = jnp.dot(hf, win_ref[:, D_FF + lo:D_FF + hi], preferred_element_type=F32)
        act = (gate * _sigmoid(gate) * up).astype(BF16)
        part = jnp.dot(act, wout_ref[lo:hi, :], preferred_element_type=F32)
        if c == 0:
            acc_scr[...] = part
        else:
            acc_scr[...] += part
    o_ref[...] = x1_ref[...] + _rms(acc_scr[...], gpost_ref[...])


def _ffn(hf, x1, win, wout, layer, gpost, *, tm=512):
    T, D = x1.shape
    blocks = D_FF // V7X_MXU_WIDTH
    bounds = (0, (blocks + 1) // 2 * V7X_MXU_WIDTH, D_FF)
    row = pl.BlockSpec((tm, D), lambda i: (i, 0))
    return pl.pallas_call(
        functools.partial(_ffn_kernel, bounds=bounds),
        grid=(T // tm,),
        in_specs=[row, row,
                  pl.BlockSpec((None,) + win.shape[1:], lambda i: (layer, 0, 0), pipeline_mode=pl.Buffered(1)),
                  pl.BlockSpec((None,) + wout.shape[1:], lambda i: (layer, 0, 0), pipeline_mode=pl.Buffered(1)),
                  pl.BlockSpec((1, D), lambda i: (0, 0))],
        out_specs=row,
        out_shape=jax.ShapeDtypeStruct((T, D), F32),
        scratch_shapes=[pltpu.VMEM((tm, D), F32)],
        compiler_params=_cparams(("parallel",)),
        name="ffn",
    )(hf, x1, win, wout, gpost)


_SRC = [int(c) for c in np.cumsum([0, GMLP_WIDTH, GMLP_WIDTH, NSA_Q_WIDTH, 6 * NSA_KV_WIDTH, N_BRANCH * NSA_HEADS,
                                   RNN_WIDTH, RNN_WIDTH, N_BRANCH * D_MODEL])]


def _reorder_w_in_kernel(w_ref, o_ref):
    u0, _, q0, kv0, ng0, xr0, _, _, end = _SRC
    rest0 = kv0 + 2 * NSA_KV_WIDTH

    def put(dst, lo, hi, scale=None):
        x = w_ref[lo:hi, :]
        if scale is not None:
            x = x * scale
        o_ref[dst:dst + hi - lo, :] = x.astype(o_ref.dtype)

    put(OFF_U, u0, q0)
    put(OFF_XR, xr0, end)
    put(OFF_KC, kv0, rest0)
    put(OFF_NG, ng0, xr0)
    pad0 = OFF_NG + xr0 - ng0
    o_ref[pad0:OFF_Q, :] = jnp.zeros((OFF_Q - pad0, o_ref.shape[1]), o_ref.dtype)
    put(OFF_Q, q0, kv0, Q_SCALE)
    put(OFF_KV, rest0, ng0)


def _reorder_w_in(w_in, *, td=256):
    L, D, d_in = w_in.shape
    return pl.pallas_call(
        _reorder_w_in_kernel,
        grid=(L, D // td),
        in_specs=[pl.BlockSpec((None, d_in, td), lambda l, i: (l, 0, i))],
        out_specs=pl.BlockSpec((None, D_IN_PAD, td), lambda l, i: (l, 0, i)),
        out_shape=jax.ShapeDtypeStruct((L, D_IN_PAD, D), BF16),
        compiler_params=_cparams(("parallel", "parallel")),
        name="reorder_w_in",
    )(jnp.swapaxes(w_in, 1, 2))


def _block_diag_gates(wa, wx):
    per = RNN_BLOCK // RNN_HEAD_DIM
    nblk = RNN_HEADS // per
    eye = jnp.eye(per, dtype=wa.dtype)

    def bd(w):
        w = w.reshape(nblk, per, RNN_HEAD_DIM, RNN_HEAD_DIM)
        return jnp.einsum('kpio,pq->kpiqo', w, eye).reshape(nblk, RNN_BLOCK, RNN_BLOCK)

    return jnp.concatenate([bd(wa), bd(wx)], axis=-1).astype(BF16)


def _layer(x2d, B, S, wt_in_all, dense_all, layer, g_pre_mix, g_post_mix, g_pre_ffn, g_post_ffn,
           gmlp_ln_g, gmlp_ln_b, gmlp_ws, gmlp_bs,
           nsa_pe_k, nsa_pe_v, nsa_wk1, nsa_wk2, nsa_wv1, nsa_wv2,
           rnn_conv_w, rnn_conv_b, rnn_wa, rnn_ba, rnn_wx, rnn_bx, rnn_lam):
    row = lambda a: a.reshape(1, -1)
    z = _in_proj(x2d, row(g_pre_mix), wt_in_all, layer)

    y_a = _gmlp(z, row(gmlp_ln_g), row(gmlp_ln_b), gmlp_ws, gmlp_bs.T)
    y_c = _rglru(z, rnn_conv_w, row(rnn_conv_b), _block_diag_gates(rnn_wa, rnn_wx),
                 row(rnn_ba), row(rnn_bx), row(rnn_lam), B=B, S=S)

    kc, vct = _compress(z, nsa_pe_k.reshape(1, -1), nsa_pe_v.reshape(1, -1), nsa_wk1.astype(BF16),
                        nsa_wk2.astype(BF16), nsa_wv1.astype(BF16), nsa_wv2.T.astype(BF16), B=B, S=S)
    y_b = _nsa(z, kc, vct, B=B, S=S)

    x1, hf = _merge(y_a, y_b, y_c, z, x2d, *dense_all[:4], layer, row(g_post_mix), row(g_pre_ffn))
    return _ffn(hf, x1, *dense_all[4:], layer, row(g_post_ffn))


def kernel(x, g_pre_mix, g_post_mix, g_pre_ffn, g_post_ffn, w_in, gmlp_ln_g, gmlp_ln_b, gmlp_ws, gmlp_bs, nsa_pe_k, nsa_pe_v, nsa_wk1, nsa_wk2, nsa_wv1, nsa_wv2, rnn_conv_w, rnn_conv_b, rnn_wa, rnn_ba, rnn_wx, rnn_bx, rnn_lam, w_br_a, w_br_b, w_br_c, w_o, w_ffn_in, w_ffn_out):
    B, S, D = x.shape
    params = (g_pre_mix, g_post_mix, g_pre_ffn, g_post_ffn, gmlp_ln_g, gmlp_ln_b, gmlp_ws, gmlp_bs,
              nsa_pe_k, nsa_pe_v, nsa_wk1, nsa_wk2, nsa_wv1, nsa_wv2,
              rnn_conv_w, rnn_conv_b, rnn_wa, rnn_ba, rnn_wx, rnn_bx, rnn_lam)
    wt_in_all = _reorder_w_in(w_in)
    dense_all = tuple(w.astype(BF16) for w in (w_br_a, w_br_b, w_br_c, w_o, w_ffn_in, w_ffn_out))
    x2d = x.reshape(B * S, D)
    for l in range(w_in.shape[0]):
        x2d = _layer(x2d, B, S, wt_in_all, dense_all, l, *(p[l] for p in params))
    return x2d.reshape(B, S, D)
```

```python
import functools

import jax
import jax.numpy as jnp
import numpy as np
from jax import lax
from jax.experimental import pallas as pl
from jax.experimental.pallas import tpu as pltpu

F32 = jnp.float32
BF16 = jnp.bfloat16

EPS = 1e-6
NEG_INF = -1e30
FORCE_SCORE = 1e4

D_MODEL = 1024
GMLP_WIDTH = 1024
GMLP_GROUPS = 4
GMLP_GROUP_DIM = GMLP_WIDTH // GMLP_GROUPS
GMLP_CHUNK = 128

NSA_HEADS = 16
NSA_KV_GROUPS = 4
NSA_HEAD_DIM = 64
NSA_HPG = NSA_HEADS // NSA_KV_GROUPS
NSA_Q_WIDTH = NSA_HEADS * NSA_HEAD_DIM
NSA_KV_WIDTH = NSA_KV_GROUPS * NSA_HEAD_DIM
N_BRANCH = 3
CMP_BLOCK = 32
CMP_STRIDE = 16
CMP_HIDDEN = 256
SEL_BLOCK = 64
SEL_TOP_N = 16
WINDOW = 512
Q_SCALE = NSA_HEAD_DIM ** -0.5 * float(np.log2(np.e))

RNN_WIDTH = 1024
RNN_HEADS = 16
RNN_HEAD_DIM = RNN_WIDTH // RNN_HEADS
CONV_WIDTH = 4
LRU_C = 8.0
RNN_BLOCK = 256

D_FF = 2816

OFF_U = 0
OFF_V = 1024
OFF_XR = 2048
OFF_RG = 3072
OFF_MG = 4096
OFF_KC = 7168
OFF_NG = OFF_KC + 2 * NSA_KV_WIDTH
NG_PAD = 512
OFF_Q = OFF_NG + NG_PAD
OFF_KV = OFF_Q + NSA_Q_WIDTH
D_IN_PAD = OFF_KV + 4 * NSA_KV_WIDTH

LANES = 128
V7X_MXU_WIDTH = 256
BF16_SUBLANES = 16
V7X_VMEM_LIMIT = 56 * 1024 * 1024


def _cparams(sem, vmem=V7X_VMEM_LIMIT):
    return pltpu.CompilerParams(dimension_semantics=sem, vmem_limit_bytes=vmem)


def _rms(x, g):
    ms = jnp.mean(x * x, axis=-1, keepdims=True)
    return x * lax.rsqrt(ms + EPS) * g


def _gelu(x):
    return jax.nn.gelu(x)


def _sigmoid(x):
    return 0.5 * jnp.tanh(0.5 * x) + 0.5


_NT = (((1,), (1,)), ((), ()))


def _in_proj_kernel(x_ref, g_ref, w_ref, o_ref, h_scr):
    @pl.when(pl.program_id(1) == 0)
    def _():
        h_scr[...] = _rms(x_ref[...], g_ref[...]).astype(BF16)

    o_ref[...] = lax.dot_general(h_scr[...], w_ref[...], _NT, preferred_element_type=F32).astype(o_ref.dtype)


def _in_proj(x2d, g, wt_all, layer, *, tm=1024, tn=2048):
    T, D = x2d.shape
    N = wt_all.shape[1]
    return pl.pallas_call(
        _in_proj_kernel,
        grid=(T // tm, N // tn),
        in_specs=[
            pl.BlockSpec((tm, D), lambda i, j: (i, 0)),
            pl.BlockSpec((1, D), lambda i, j: (0, 0)),
            pl.BlockSpec((None, tn, D), lambda i, j: (layer, j, 0)),
        ],
        out_specs=pl.BlockSpec((tm, tn), lambda i, j: (i, j)),
        out_shape=jax.ShapeDtypeStruct((T, N), BF16),
        scratch_shapes=[pltpu.VMEM((tm, D), BF16)],
        compiler_params=_cparams(("parallel", "arbitrary")),
        name="in_proj",
    )(x2d, g, wt_all)


def _gmlp_kernel(u_ref, v_ref, lng_ref, lnb_ref, ws_ref, bst_ref, o_ref, *, n_chunks):
    C = GMLP_CHUNK
    row = lax.broadcasted_iota(jnp.int32, (C, C), 0)
    col = lax.broadcasted_iota(jnp.int32, (C, C), 1)
    causal = col <= row
    ws = [jnp.where(causal, ws_ref[g], 0.0).astype(BF16) for g in range(GMLP_GROUPS)]
    bst = bst_ref[...]
    for c in range(n_chunks):
        rows = slice(c * C, (c + 1) * C)
        gv = _gelu(v_ref[rows, :].astype(F32))
        mu = jnp.mean(gv, axis=-1, keepdims=True)
        d = gv - mu
        var = jnp.mean(d * d, axis=-1, keepdims=True)
        vn = (d * lax.rsqrt(var + EPS) * lng_ref[...] + lnb_ref[...]).astype(BF16)
        for g in range(GMLP_GROUPS):
            cols = slice(g * GMLP_GROUP_DIM, (g + 1) * GMLP_GROUP_DIM)
            mixed = jnp.dot(ws[g], vn[:, cols], preferred_element_type=F32) + bst[:, g:g + 1]
            o_ref[rows, cols] = (_gelu(u_ref[rows, cols].astype(F32)) * mixed).astype(o_ref.dtype)


def _gmlp(z, lng, lnb, ws, bst, *, n_chunks=8):
    T = z.shape[0]
    tm = GMLP_CHUNK * n_chunks
    W = GMLP_WIDTH
    return pl.pallas_call(
        functools.partial(_gmlp_kernel, n_chunks=n_chunks),
        grid=(T // tm,),
        in_specs=[
            pl.BlockSpec((tm, W), lambda i: (i, OFF_U // W)),
            pl.BlockSpec((tm, W), lambda i: (i, OFF_V // W)),
            pl.BlockSpec((1, W), lambda i: (0, 0)),
            pl.BlockSpec((1, W), lambda i: (0, 0)),
            pl.BlockSpec((GMLP_GROUPS, GMLP_CHUNK, GMLP_CHUNK), lambda i: (0, 0, 0)),
            pl.BlockSpec((GMLP_CHUNK, GMLP_GROUPS), lambda i: (0, 0)),
        ],
        out_specs=pl.BlockSpec((tm, W), lambda i: (i, 0)),
        out_shape=jax.ShapeDtypeStruct((T, W), BF16),
        compiler_params=_cparams(("parallel",)),
        name="gmlp",
    )(z, z, lng, lnb, ws, bst)


def _rglru_kernel(xr_ref, rg_ref, cw_ref, cb_ref, wab_ref, ba_ref, bx_ref, lam_ref, o_ref,
                  ext_scr, a_scr, b_scr, hc_scr, *, tt):
    W = RNN_WIDTH
    groups = tt // 8
    sub_row = lax.broadcasted_iota(jnp.int32, (groups, 8, RNN_BLOCK), 1)

    @pl.when(pl.program_id(1) == 0)
    def _():
        ext_scr[0:8, :] = jnp.zeros((8, W), F32)
        hc_scr[...] = jnp.zeros((8, W), F32)

    xr = xr_ref[...].astype(F32)
    ext_scr[8:8 + tt, :] = xr
    cw = cw_ref[...]
    xc = (cw[3:4] * xr + cw[2:3] * ext_scr[7:7 + tt, :] + cw[1:2] * ext_scr[6:6 + tt, :]
          + cw[0:1] * ext_scr[5:5 + tt, :] + cb_ref[...])
    ext_scr[0:8, :] = xr[tt - 8:tt, :]

    lam = lam_ref[...]
    neg = -lam
    softplus = jnp.maximum(neg, 0.0) + jnp.log1p(jnp.exp(-jnp.abs(neg)))
    for k in range(W // RNN_BLOCK):
        cols = slice(k * RNN_BLOCK, (k + 1) * RNN_BLOCK)
        xck = xc[:, cols]
        gates = jnp.dot(xck.astype(BF16), wab_ref[k], preferred_element_type=F32)
        r = _sigmoid(gates[:, :RNN_BLOCK] + ba_ref[:, cols])
        i = _sigmoid(gates[:, RNN_BLOCK:] + bx_ref[:, cols])
        log_a = -LRU_C * r * softplus[:, cols]
        a = jnp.exp(log_a)
        one_minus_a2 = -jnp.tanh(log_a) * (1.0 + a * a)
        b_in = jnp.sqrt(one_minus_a2) * (i * xck)
        a3 = a.reshape(groups, 8, RNN_BLOCK)
        b3 = b_in.reshape(groups, 8, RNN_BLOCK)
        for step in (1, 2, 4):
            keep = sub_row >= step
            a_prev = jnp.where(keep, pltpu.roll(a3, step, axis=1), 1.0)
            b_prev = jnp.where(keep, pltpu.roll(b3, step, axis=1), 0.0)
            b3 = a3 * b_prev + b3
            a3 = a3 * a_prev
        a_scr[:, cols] = a3.reshape(tt, RNN_BLOCK)
        b_scr[:, cols] = b3.reshape(tt, RNN_BLOCK)

    def carry_step(i, h_prev):
        base = pl.multiple_of(i * BF16_SUBLANES, BF16_SUBLANES)
        gate = _gelu(rg_ref[pl.ds(base, BF16_SUBLANES), :].astype(F32))
        ys = []
        for half in range(BF16_SUBLANES // 8):
            rows = pl.ds(base + half * 8, 8)
            h = b_scr[rows, :] + a_scr[rows, :] * h_prev
            ys.append(h * gate[half * 8:(half + 1) * 8, :])
            h_prev = h[7:8, :]
        o_ref[pl.ds(base, BF16_SUBLANES), :] = jnp.concatenate(ys, axis=0).astype(o_ref.dtype)
        return h_prev

    h_last = lax.fori_loop(0, tt // BF16_SUBLANES, carry_step, hc_scr[0:1, :])
    hc_scr[...] = jnp.broadcast_to(h_last, (8, W))


def _rglru(z, cw, cb, wab, ba, bx, lam, *, B, S, tt=512):
    T = z.shape[0]
    W = RNN_WIDTH
    nt = S // tt
    return pl.pallas_call(
        functools.partial(_rglru_kernel, tt=tt),
        grid=(B, nt),
        in_specs=[
            pl.BlockSpec((tt, W), lambda b, t: (b * nt + t, OFF_XR // W)),
            pl.BlockSpec((tt, W), lambda b, t: (b * nt + t, OFF_RG // W)),
            pl.BlockSpec((CONV_WIDTH, W), lambda b, t: (0, 0)),
            pl.BlockSpec((1, W), lambda b, t: (0, 0)),
            pl.BlockSpec((W // RNN_BLOCK, RNN_BLOCK, 2 * RNN_BLOCK), lambda b, t: (0, 0, 0)),
            pl.BlockSpec((1, W), lambda b, t: (0, 0)),
            pl.BlockSpec((1, W), lambda b, t: (0, 0)),
            pl.BlockSpec((1, W), lambda b, t: (0, 0)),
        ],
        out_specs=pl.BlockSpec((tt, W), lambda b, t: (b * nt + t, 0)),
        out_shape=jax.ShapeDtypeStruct((T, W), BF16),
        scratch_shapes=[
            pltpu.VMEM((tt + 8, W), F32),
            pltpu.VMEM((tt, W), F32),
            pltpu.VMEM((tt, W), F32),
            pltpu.VMEM((8, W), F32),
        ],
        compiler_params=_cparams(("parallel", "arbitrary")),
        name="rglru",
    )(z, z, cw, cb, wab, ba, bx, lam)


def _compress_hidden(x_ref, pe_ref, w1_ref, x_scr):
    DH = NSA_HEAD_DIM
    half = CMP_STRIDE * DH
    slots = x_ref.shape[0] // CMP_STRIDE
    x = x_ref[...].astype(F32)
    for c in range(NSA_KV_WIDTH // LANES):
        x_scr[c] = x[:, c * LANES:(c + 1) * LANES]
    phase = [[x_scr[c, pl.ds(l, slots, stride=CMP_STRIDE), :] for c in range(NSA_KV_WIDTH // LANES)]
             for l in range(CMP_STRIDE)]
    per_block = LANES // DH

    def group_rows(g):
        lanes = slice((g % per_block) * DH, (g % per_block + 1) * DH)
        return jnp.concatenate([phase[l][g // per_block][:, lanes] for l in range(CMP_STRIDE)], axis=1)

    hm = jnp.concatenate([group_rows(g) for g in range(NSA_KV_GROUPS)], axis=0).astype(BF16)
    rows = hm.shape[0]
    lo = jnp.dot(hm, w1_ref[0:half, :], preferred_element_type=F32)
    hi = jnp.dot(hm, w1_ref[half:2 * half, :], preferred_element_type=F32)
    pe = jnp.broadcast_to(pe_ref[...], (8, 2 * half)).astype(BF16)
    pe_term = jnp.dot(pe, w1_ref[...], preferred_element_type=F32)[0:1, :]
    return _gelu(lo + pltpu.roll(hi, rows - 1, axis=0) + pe_term).astype(BF16)


def _compress_kernel(xk_ref, xv_ref, pek_ref, pev_ref, wk1_ref, wk2_ref, wv1_ref, wv2t_ref, kc_ref, vct_ref, x_scr):
    kc_ref[...] = jnp.dot(_compress_hidden(xk_ref, pek_ref, wk1_ref, x_scr), wk2_ref[...],
                          preferred_element_type=F32)
    vct_ref[...] = lax.dot_general(wv2t_ref[...], _compress_hidden(xv_ref, pev_ref, wv1_ref, x_scr), _NT,
                                   preferred_element_type=F32)


def _compress(z, pek, pev, wk1, wk2, wv1, wv2t, *, B, S):
    G, DH = NSA_KV_GROUPS, NSA_HEAD_DIM
    slots = S // CMP_STRIDE
    KVW = NSA_KV_WIDTH
    full = lambda shape: pl.BlockSpec(shape, lambda b: (0,) * len(shape))
    return pl.pallas_call(
        _compress_kernel,
        grid=(B,),
        in_specs=[pl.BlockSpec((S, KVW), lambda b: (b, OFF_KC // KVW)),
                  pl.BlockSpec((S, KVW), lambda b: (b, OFF_KC // KVW + 1)),
                  full(pek.shape), full(pev.shape), full(wk1.shape), full(wk2.shape), full(wv1.shape),
                  full(wv2t.shape)],
        out_specs=[pl.BlockSpec((G * slots, DH), lambda b: (b, 0)), pl.BlockSpec((DH, G * slots), lambda b: (0, b))],
        out_shape=[jax.ShapeDtypeStruct((B * G * slots, DH), F32), jax.ShapeDtypeStruct((DH, B * G * slots), F32)],
        scratch_shapes=[pltpu.VMEM((KVW // LANES, S, LANES), F32)],
        compiler_params=_cparams(("parallel",)),
        name="nsa_compress",
    )(z, z, pek, pev, wk1, wk2, wv1, wv2t)


def _nsa_kernel(q_ref, kc_ref, vct_ref, ks_ref, vs_ref, kw_ref, vw_ref, gate_ref, o_ref,
                ks_scr, kw_scr, vs_scr, vw_scr, *tile_scratch, tq, S):
    DH = NSA_HEAD_DIM
    tk = tq
    v_rows = DH + BF16_SUBLANES
    g_idx = pl.program_id(1)

    for g in range(NSA_KV_GROUPS):
        @pl.when(g_idx == g)
        def _():
            lanes = slice(g * DH, (g + 1) * DH)
            ks_scr[...] = ks_ref[:, lanes]
            kw_scr[...] = kw_ref[:, lanes]
            ones = jnp.ones((BF16_SUBLANES, tk), BF16)
            for v_ref, v_scr in ((vs_ref, vs_scr), (vw_ref, vw_scr)):
                vt = v_ref[:, lanes].T
                for kt in range(S // tk):
                    v_scr[kt, 0:DH, :] = vt[:, kt * tk:(kt + 1) * tk]
                    v_scr[kt, DH:v_rows, :] = ones

    def q_tile(qi, carry):
        rows = pl.ds(pl.multiple_of(qi * tq, tq), tq)
        _nsa_q_tile(qi, g_idx, q_ref.at[rows, :], kc_ref, vct_ref, gate_ref.at[rows, :], o_ref.at[rows, :],
                    ks_scr, kw_scr, vs_scr, vw_scr, *tile_scratch, tq=tq, S=S)
        return carry

    lax.fori_loop(0, S // tq, q_tile, 0)


def _nsa_q_tile(qi, g_idx, q_ref, kc_ref, vct_ref, gate_ref, o_ref,
                ks_scr, kw_scr, vs_scr, vw_scr, gt_scr, ps_scr, sel_scr, sa_scr, sb_scr, sc_scr, m_scr, acc_scr,
                *, tq, S):
    HP, DH = NSA_HPG, NSA_HEAD_DIM
    M = HP * tq
    tk = tq
    n_sel = S // SEL_BLOCK
    slots = kc_ref.shape[0]
    v_rows = DH + BF16_SUBLANES
    t0 = qi * tq

    def per_head(x):
        return jnp.concatenate([x] * HP, axis=1)

    q_t = q_ref[...].T
    qt = jnp.concatenate([q_t[h * DH:(h + 1) * DH, :] for h in range(HP)], axis=1)

    def scores(k_scr, kt, dst):
        start = pl.multiple_of(kt * tk, tk)
        dst[:, 0:M] = jnp.dot(k_scr[pl.ds(start, tk), :], qt, preferred_element_type=F32)

    sc = jnp.dot(kc_ref[...].astype(BF16), qt, preferred_element_type=F32)
    scores(ks_scr, 0, sa_scr)
    n_idx = lax.broadcasted_iota(jnp.int32, (slots, tq), 0)
    t_idx = t0 + lax.broadcasted_iota(jnp.int32, (slots, tq), 1)
    valid = per_head(jnp.where(n_idx * CMP_STRIDE + (CMP_BLOCK - 1) <= t_idx, 1.0, 0.0)) > 0.5
    sc = jnp.where(valid, sc, NEG_INF)
    mx = jnp.max(sc, axis=0, keepdims=True)
    p = jnp.where(valid, jnp.exp2(sc - mx), 0.0)
    den = jnp.sum(p, axis=0, keepdims=True)
    p_c = p * (1.0 / jnp.where(den > 0.0, den, 1.0))
    o_cmp = jnp.dot(vct_ref[...].astype(BF16), p_c.astype(BF16), preferred_element_type=F32)

    p_sum = p_c[:, 0:tq]
    for h in range(1, HP):
        p_sum = p_sum + p_c[:, h * tq:(h + 1) * tq]
    per_sel = SEL_BLOCK // CMP_STRIDE
    for c in range(tq // LANES):
        ps_scr[c] = p_sum[:, c * LANES:(c + 1) * LANES]
    every = [jnp.concatenate([ps_scr[c, pl.ds(r, n_sel, stride=per_sel), :] for c in range(tq // LANES)], axis=1)
             for r in range(per_sel)]
    j_idx = lax.broadcasted_iota(jnp.int32, (n_sel, tq), 0)
    before = jnp.where(j_idx >= 1, pltpu.roll(every[per_sel - 1], 1, axis=0), 0.0)
    imp = every[0]
    for r in range(1, per_sel - 1):
        imp = imp + every[r]
    imp = imp + 0.5 * every[per_sel - 1] + 0.5 * before
    cur = (t0 + lax.broadcasted_iota(jnp.int32, (n_sel, tq), 1)) // SEL_BLOCK
    forced = (j_idx == 0) | (j_idx == cur) | (j_idx == cur - 1)
    imp = jnp.where(forced, FORCE_SCORE, jnp.where(j_idx > cur, NEG_INF, imp))
    rank = jnp.zeros((n_sel, tq), F32)
    for i in range(n_sel):
        ci = imp[i:i + 1, :]
        ge = jnp.where(ci >= imp, 1.0, 0.0)
        gt = jnp.where(ci > imp, 1.0, 0.0)
        rank = rank + jnp.where(j_idx > i, ge, gt)
    sel_scr[...] = jnp.where(rank < float(min(SEL_TOP_N, n_sel)), 1.0, 0.0)

    per_tile = tk // SEL_BLOCK
    kloc = lax.broadcasted_iota(jnp.int32, (tk, tq), 0)
    tloc = lax.broadcasted_iota(jnp.int32, (tk, tq), 1)
    key_ahead = kloc - tloc

    def slc_bias(kt):
        hit = jnp.concatenate(
            [jnp.broadcast_to(sel_scr[pl.ds(kt * per_tile + jj, 1), :], (SEL_BLOCK, tq)) for jj in range(per_tile)],
            axis=0)
        return jnp.where((hit > 0.5) & (key_ahead <= (qi - kt) * tk), 0.0, NEG_INF)

    def absorb(v_ext, src, bias):
        s = src[:, 0:M] if bias is None else src[:, 0:M] + per_head(bias)
        m_old = m_scr[...]
        m_new = jnp.maximum(m_old, jnp.max(s, axis=0, keepdims=True))
        alpha = jnp.exp2(m_old - m_new)
        pr = jnp.exp2(s - m_new).astype(BF16)
        m_scr[...] = m_new
        acc_scr[...] = alpha * acc_scr[...] + jnp.dot(v_ext, pr, preferred_element_type=F32)

    def reset():
        m_scr[...] = jnp.full((1, M), NEG_INF, F32)
        acc_scr[...] = jnp.zeros((v_rows, M), F32)

    def result():
        acc = acc_scr[...]
        return acc[0:DH, :] * (1.0 / acc[DH:DH + 1, :])

    reset()
    n_tiles = qi + 1

    def slc_scores(kt, dst):
        scores(ks_scr, jnp.minimum(kt, qi), dst)

    def slc_pair(j, carry):
        slc_scores(2 * j + 1, sb_scr)
        absorb(vs_scr[2 * j], sa_scr, slc_bias(2 * j))
        slc_scores(2 * j + 2, sa_scr)
        absorb(vs_scr[2 * j + 1], sb_scr, slc_bias(2 * j + 1))
        return carry

    lax.fori_loop(0, n_tiles // 2, slc_pair, 0)

    @pl.when(n_tiles % 2 == 1)
    def _():
        absorb(vs_scr[qi], sa_scr, slc_bias(qi))

    o_slc = result()

    reset()
    n_back = WINDOW // tk
    diag_bias = jnp.where(kloc <= tloc, 0.0, NEG_INF)

    def window_start(n_win):
        bufs = (sa_scr, sb_scr)
        scores(kw_scr, qi, bufs[0])
        for d in range(n_win):
            if d + 1 < n_win:
                scores(kw_scr, qi - (d + 1), bufs[(d + 1) % 2])
            absorb(vw_scr[qi - d], bufs[d % 2], diag_bias if d == 0 else None)

    def window_full():
        scores(kw_scr, qi, sa_scr)
        scores(kw_scr, qi - n_back, sb_scr)
        scores(kw_scr, qi - 1, sc_scr)
        near = per_head(jnp.where(kloc <= tloc, 1.0, 0.0)) > 0.5
        s = jnp.where(near, sa_scr[:, 0:M], sb_scr[:, 0:M])
        m_old = m_scr[...]
        m_new = jnp.maximum(m_old, jnp.max(s, axis=0, keepdims=True))
        alpha = jnp.exp2(m_old - m_new)
        pr = jnp.exp2(s - m_new).astype(BF16)
        m_scr[...] = m_new
        none = jnp.zeros_like(pr)
        acc_scr[...] = (alpha * acc_scr[...]
                        + jnp.dot(vw_scr[qi], jnp.where(near, pr, none), preferred_element_type=F32)
                        + jnp.dot(vw_scr[qi - n_back], jnp.where(near, none, pr), preferred_element_type=F32))
        absorb(vw_scr[qi - 1], sc_scr, None)

    for n_win in range(1, n_back + 1):
        @pl.when(qi == n_win - 1)
        def _():
            window_start(n_win)

    @pl.when(qi >= n_back)
    def _():
        window_full()

    o_win = result()

    gt_scr[...] = gate_ref[...].astype(F32).T

    def gate(branch):
        rows = [gt_scr[pl.ds(g_idx * (HP * N_BRANCH) + h * N_BRANCH + branch, 1), :] for h in range(HP)]
        return _sigmoid(jnp.concatenate(rows, axis=1))

    o = gate(0) * o_cmp + gate(1) * o_slc + gate(2) * o_win
    o_ref[...] = jnp.concatenate([o[:, h * tq:(h + 1) * tq].T for h in range(HP)], axis=1).astype(o_ref.dtype)


def _nsa(z, kc, vct, *, B, S, tq=256):
    G, HP, DH = NSA_KV_GROUPS, NSA_HPG, NSA_HEAD_DIM
    assert WINDOW == 2 * tq and CMP_BLOCK == 2 * CMP_STRIDE and S % tq == 0, (WINDOW, tq, S)
    T = B * S
    slots = S // CMP_STRIDE
    n_sel = S // SEL_BLOCK
    n_kt = S // tq
    v_rows = DH + BF16_SUBLANES
    KVW = NSA_KV_WIDTH

    def kv_spec(which):
        return pl.BlockSpec((S, KVW), lambda b, g: (b, OFF_KV // KVW + which))

    return pl.pallas_call(
        functools.partial(_nsa_kernel, tq=tq, S=S),
        grid=(B, G),
        in_specs=[
            pl.BlockSpec((S, HP * DH), lambda b, g: (b, OFF_Q // (HP * DH) + g)),
            pl.BlockSpec((slots, DH), lambda b, g: (b * G + g, 0)),
            pl.BlockSpec((DH, slots), lambda b, g: (0, b * G + g)),
            kv_spec(0), kv_spec(1), kv_spec(2), kv_spec(3),
            pl.BlockSpec((S, LANES), lambda b, g: (b, OFF_NG // LANES)),
        ],
        out_specs=pl.BlockSpec((S, HP * DH), lambda b, g: (b, g)),
        out_shape=jax.ShapeDtypeStruct((T, NSA_Q_WIDTH), BF16),
        scratch_shapes=[
            pltpu.VMEM((S, DH), BF16),
            pltpu.VMEM((S, DH), BF16),
            pltpu.VMEM((n_kt, v_rows, tq), BF16),
            pltpu.VMEM((n_kt, v_rows, tq), BF16),
            pltpu.VMEM((LANES, tq), F32),
            pltpu.VMEM((tq // LANES, slots, LANES), F32),
            pltpu.VMEM((n_sel, tq), F32),
            pltpu.VMEM((tq, HP * tq + LANES), F32),
            pltpu.VMEM((tq, HP * tq + LANES), F32),
            pltpu.VMEM((tq, HP * tq + LANES), F32),
            pltpu.VMEM((1, HP * tq), F32),
            pltpu.VMEM((v_rows, HP * tq), F32),
        ],
        compiler_params=_cparams(("parallel", "parallel")),
        name="nsa_attention",
    )(z, kc, vct, z, z, z, z, z)


def _merge_kernel(ya_ref, yb_ref, yc_ref, mga_ref, mgb_ref, mgc_ref, x_ref, wa_ref, wb_ref, wc_ref, wo_ref,
                  gpost_ref, gpre_ref, x1_ref, hf_ref):
    gate = lambda ref: _sigmoid(ref[...].astype(F32))
    merged = gate(mga_ref) * jnp.dot(ya_ref[...], wa_ref[...], preferred_element_type=F32)
    merged = merged + gate(mgb_ref) * jnp.dot(yb_ref[...], wb_ref[...], preferred_element_type=F32)
    merged = merged + gate(mgc_ref) * jnp.dot(yc_ref[...], wc_ref[...], preferred_element_type=F32)
    y = jnp.dot(merged.astype(BF16), wo_ref[...], preferred_element_type=F32)
    x1 = x_ref[...] + _rms(y, gpost_ref[...])
    x1_ref[...] = x1
    hf_ref[...] = _rms(x1, gpre_ref[...]).astype(BF16)


def _merge(ya, yb, yc, z, x2d, wa, wb, wc, wo, layer, gpost, gpre, *, tm=512):
    T, D = x2d.shape
    row = lambda c: pl.BlockSpec((tm, D), lambda i: (i, c))
    wfull = pl.BlockSpec((None, D, D), lambda i: (layer, 0, 0))
    vec = pl.BlockSpec((1, D), lambda i: (0, 0))
    mg0 = OFF_MG // D
    return pl.pallas_call(
        _merge_kernel,
        grid=(T // tm,),
        in_specs=[row(0), row(0), row(0), row(mg0), row(mg0 + 1), row(mg0 + 2), row(0),
                  wfull, wfull, wfull, wfull, vec, vec],
        out_specs=[row(0), row(0)],
        out_shape=[jax.ShapeDtypeStruct((T, D), F32), jax.ShapeDtypeStruct((T, D), BF16)],
        compiler_params=_cparams(("parallel",)),
        name="merge",
    )(ya, yb, yc, z, z, z, x2d, wa, wb, wc, wo, gpost, gpre)


def _ffn_kernel(hf_ref, x1_ref, win_ref, wout_ref, gpost_ref, o_ref, acc_scr, *, bounds):
    hf = hf_ref[...]
    for c, (lo, hi) in enumerate(zip(bounds[:-1], bounds[1:])):
        gate = jnp.dot(hf, win_ref[:, lo:hi], preferred_element_type=F32)
        up = jnp.dot(hf, win_ref[:, D_FF + lo:D_FF + hi], preferred_element_type=F32)
        act = (gate * _sigmoid(gate) * up).astype(BF16)
        part = jnp.dot(act, wout_ref[lo:hi, :], preferred_element_type=F32)
        if c == 0:
            acc_scr[...] = part
        else:
            acc_scr[...] += part
    o_ref[...] = x1_ref[...] + _rms(acc_scr[...], gpost_ref[...])


def _ffn(hf, x1, win, wout, layer, gpost, *, tm=512):
    T, D = x1.shape
    blocks = D_FF // V7X_MXU_WIDTH
    bounds = (0, (blocks + 1) // 2 * V7X_MXU_WIDTH, D_FF)
    row = pl.BlockSpec((tm, D), lambda i: (i, 0))
    return pl.pallas_call(
        functools.partial(_ffn_kernel, bounds=bounds),
        grid=(T // tm,),
        in_specs=[row, row,
                  pl.BlockSpec((None,) + win.shape[1:], lambda i: (layer, 0, 0), pipeline_mode=pl.Buffered(1)),
                  pl.BlockSpec((None,) + wout.shape[1:], lambda i: (layer, 0, 0), pipeline_mode=pl.Buffered(1)),
                  pl.BlockSpec((1, D), lambda i: (0, 0))],
        out_specs=row,
        out_shape=jax.ShapeDtypeStruct((T, D), F32),
        scratch_shapes=[pltpu.VMEM((tm, D), F32)],
        compiler_params=_cparams(("parallel",)),
        name="ffn",
    )(hf, x1, win, wout, gpost)


_SRC = [int(c) for c in np.cumsum([0, GMLP_WIDTH, GMLP_WIDTH, NSA_Q_WIDTH, 6 * NSA_KV_WIDTH, N_BRANCH * NSA_HEADS,
                                   RNN_WIDTH, RNN_WIDTH, N_BRANCH * D_MODEL])]


def _reorder_w_in_kernel(w_ref, o_ref):
    u0, _, q0, kv0, ng0, xr0, _, _, end = _SRC
    rest0 = kv0 + 2 * NSA_KV_WIDTH

    def put(dst, lo, hi, scale=None):
        x = w_ref[lo:hi, :]
        if scale is not None:
            x = x * scale
        o_ref[dst:dst + hi - lo, :] = x.astype(o_ref.dtype)

    put(OFF_U, u0, q0)
    put(OFF_XR, xr0, end)
    put(OFF_KC, kv0, rest0)
    put(OFF_NG, ng0, xr0)
    pad0 = OFF_NG + xr0 - ng0
    o_ref[pad0:OFF_Q, :] = jnp.zeros((OFF_Q - pad0, o_ref.shape[1]), o_ref.dtype)
    put(OFF_Q, q0, kv0, Q_SCALE)
    put(OFF_KV, rest0, ng0)


def _reorder_w_in(w_in, *, td=256):
    L, D, d_in = w_in.shape
    return pl.pallas_call(
        _reorder_w_in_kernel,
        grid=(L, D // td),
        in_specs=[pl.BlockSpec((None, d_in, td), lambda l, i: (l, 0, i))],
        out_specs=pl.BlockSpec((None, D_IN_PAD, td), lambda l, i: (l, 0, i)),
        out_shape=jax.ShapeDtypeStruct((L, D_IN_PAD, D), BF16),
        compiler_params=_cparams(("parallel", "parallel")),
        name="reorder_w_in",
    )(jnp.swapaxes(w_in, 1, 2))


def _block_diag_gates(wa, wx):
    per = RNN_BLOCK // RNN_HEAD_DIM
    nblk = RNN_HEADS // per
    eye = jnp.eye(per, dtype=wa.dtype)

    def bd(w):
        w = w.reshape(nblk, per, RNN_HEAD_DIM, RNN_HEAD_DIM)
        return jnp.einsum('kpio,pq->kpiqo', w, eye).reshape(nblk, RNN_BLOCK, RNN_BLOCK)

    return jnp.concatenate([bd(wa), bd(wx)], axis=-1).astype(BF16)


def _layer(x2d, B, S, wt_in_all, dense_all, layer, g_pre_mix, g_post_mix, g_pre_ffn, g_post_ffn,
           gmlp_ln_g, gmlp_ln_b, gmlp_ws, gmlp_bs,
           nsa_pe_k, nsa_pe_v, nsa_wk1, nsa_wk2, nsa_wv1, nsa_wv2,
           rnn_conv_w, rnn_conv_b, rnn_wa, rnn_ba, rnn_wx, rnn_bx, rnn_lam):
    row = lambda a: a.reshape(1, -1)
    z = _in_proj(x2d, row(g_pre_mix), wt_in_all, layer)

    y_a = _gmlp(z, row(gmlp_ln_g), row(gmlp_ln_b), gmlp_ws, gmlp_bs.T)
    y_c = _rglru(z, rnn_conv_w, row(rnn_conv_b), _block_diag_gates(rnn_wa, rnn_wx),
                 row(rnn_ba), row(rnn_bx), row(rnn_lam), B=B, S=S)

    kc, vct = _compress(z, nsa_pe_k.reshape(1, -1), nsa_pe_v.reshape(1, -1), nsa_wk1.astype(BF16),
                        nsa_wk2.astype(BF16), nsa_wv1.astype(BF16), nsa_wv2.T.astype(BF16), B=B, S=S)
    y_b = _nsa(z, kc, vct, B=B, S=S)

    x1, hf = _merge(y_a, y_b, y_c, z, x2d, *dense_all[:4], layer, row(g_post_mix), row(g_pre_ffn))
    return _ffn(hf, x1, *dense_all[4:], layer, row(g_post_ffn))


def kernel(x, g_pre_mix, g_post_mix, g_pre_ffn, g_post_ffn, w_in, gmlp_ln_g, gmlp_ln_b, gmlp_ws, gmlp_bs, nsa_pe_k, nsa_pe_v, nsa_wk1, nsa_wk2, nsa_wv1, nsa_wv2, rnn_conv_w, rnn_conv_b, rnn_wa, rnn_ba, rnn_wx, rnn_bx, rnn_lam, w_br_a, w_br_b, w_br_c, w_o, w_ffn_in, w_ffn_out):
    B, S, D = x.shape
    params = (g_pre_mix, g_post_mix, g_pre_ffn, g_post_ffn, gmlp_ln_g, gmlp_ln_b, gmlp_ws, gmlp_bs,
              nsa_pe_k, nsa_pe_v, nsa_wk1, nsa_wk2, nsa_wv1, nsa_wv2,
              rnn_conv_w, rnn_conv_b, rnn_wa, rnn_ba, rnn_wx, rnn_bx, rnn_lam)
    wt_in_all = _reorder_w_in(w_in)
    dense_all = tuple(w.astype(BF16) for w in (w_br_a, w_br_b, w_br_c, w_o, w_ffn_in, w_ffn_out))
    x2d = x.reshape(B * S, D)
    for l in range(w_in.shape[0]):
        x2d = _layer(x2d, B, S, wt_in_all, dense_all, l, *(p[l] for p in params))
    return x2d.reshape(B, S, D)
```

```python
import functools

import jax
import jax.numpy as jnp
import numpy as np
from jax import lax
from jax.experimental import pallas as pl
from jax.experimental.pallas import tpu as pltpu

F32 = jnp.float32
BF16 = jnp.bfloat16

EPS = 1e-6
NEG_INF = -1e30
FORCE_SCORE = 1e4

D_MODEL = 1024
GMLP_WIDTH = 1024
GMLP_GROUPS = 4
GMLP_GROUP_DIM = GMLP_WIDTH // GMLP_GROUPS
GMLP_CHUNK = 128

NSA_HEADS = 16
NSA_KV_GROUPS = 4
NSA_HEAD_DIM = 64
NSA_HPG = NSA_HEADS // NSA_KV_GROUPS
NSA_Q_WIDTH = NSA_HEADS * NSA_HEAD_DIM
NSA_KV_WIDTH = NSA_KV_GROUPS * NSA_HEAD_DIM
N_BRANCH = 3
CMP_BLOCK = 32
CMP_STRIDE = 16
CMP_HIDDEN = 256
SEL_BLOCK = 64
SEL_TOP_N = 16
WINDOW = 512
Q_SCALE = NSA_HEAD_DIM ** -0.5 * float(np.log2(np.e))

RNN_WIDTH = 1024
RNN_HEADS = 16
RNN_HEAD_DIM = RNN_WIDTH // RNN_HEADS
CONV_WIDTH = 4
LRU_C = 8.0
RNN_BLOCK = 256

D_FF = 2816

OFF_U = 0
OFF_V = 1024
OFF_XR = 2048
OFF_RG = 3072
OFF_MG = 4096
OFF_KC = 7168
OFF_NG = OFF_KC + 2 * NSA_KV_WIDTH
NG_PAD = 512
OFF_Q = OFF_NG + NG_PAD
OFF_KV = OFF_Q + NSA_Q_WIDTH
D_IN_PAD = OFF_KV + 4 * NSA_KV_WIDTH

SLC_UNROLL = 4
LANES = 128
V7X_MXU_WIDTH = 256
BF16_SUBLANES = 16
V7X_VMEM_LIMIT = 56 * 1024 * 1024


def _cparams(sem, vmem=V7X_VMEM_LIMIT):
    return pltpu.CompilerParams(dimension_semantics=sem, vmem_limit_bytes=vmem)


def _rms(x, g):
    ms = jnp.mean(x * x, axis=-1, keepdims=True)
    return x * lax.rsqrt(ms + EPS) * g


def _gelu(x):
    return jax.nn.gelu(x)


def _sigmoid(x):
    return 0.5 * jnp.tanh(0.5 * x) + 0.5


_NT = (((1,), (1,)), ((), ()))


def _in_proj_kernel(x_ref, g_ref, w_ref, o_ref, h_scr):
    @pl.when(pl.program_id(1) == 0)
    def _():
        h_scr[...] = _rms(x_ref[...], g_ref[...]).astype(BF16)

    o_ref[...] = lax.dot_general(h_scr[...], w_ref[...], _NT, preferred_element_type=F32).astype(o_ref.dtype)


def _in_proj(x2d, g, wt_all, layer, *, tm=1024, tn=2048):
    T, D = x2d.shape
    N = wt_all.shape[1]
    return pl.pallas_call(
        _in_proj_kernel,
        grid=(T // tm, N // tn),
        in_specs=[
            pl.BlockSpec((tm, D), lambda i, j: (i, 0)),
            pl.BlockSpec((1, D), lambda i, j: (0, 0)),
            pl.BlockSpec((None, tn, D), lambda i, j: (layer, j, 0)),
        ],
        out_specs=pl.BlockSpec((tm, tn), lambda i, j: (i, j)),
        out_shape=jax.ShapeDtypeStruct((T, N), BF16),
        scratch_shapes=[pltpu.VMEM((tm, D), BF16)],
        compiler_params=_cparams(("parallel", "arbitrary")),
        name="in_proj",
    )(x2d, g, wt_all)


def _gmlp_kernel(u_ref, v_ref, lng_ref, lnb_ref, ws_ref, bst_ref, o_ref, *, n_chunks):
    C = GMLP_CHUNK
    row = lax.broadcasted_iota(jnp.int32, (C, C), 0)
    col = lax.broadcasted_iota(jnp.int32, (C, C), 1)
    causal = col <= row
    ws = [jnp.where(causal, ws_ref[g], 0.0).astype(BF16) for g in range(GMLP_GROUPS)]
    bst = bst_ref[...]
    for c in range(n_chunks):
        rows = slice(c * C, (c + 1) * C)
        gv = _gelu(v_ref[rows, :].astype(F32))
        mu = jnp.mean(gv, axis=-1, keepdims=True)
        d = gv - mu
        var = jnp.mean(d * d, axis=-1, keepdims=True)
        vn = (d * lax.rsqrt(var + EPS) * lng_ref[...] + lnb_ref[...]).astype(BF16)
        for g in range(GMLP_GROUPS):
            cols = slice(g * GMLP_GROUP_DIM, (g + 1) * GMLP_GROUP_DIM)
            mixed = jnp.dot(ws[g], vn[:, cols], preferred_element_type=F32) + bst[:, g:g + 1]
            o_ref[rows, cols] = (_gelu(u_ref[rows, cols].astype(F32)) * mixed).astype(o_ref.dtype)


def _gmlp(z, lng, lnb, ws, bst, *, n_chunks=8):
    T = z.shape[0]
    tm = GMLP_CHUNK * n_chunks
    W = GMLP_WIDTH
    return pl.pallas_call(
        functools.partial(_gmlp_kernel, n_chunks=n_chunks),
        grid=(T // tm,),
        in_specs=[
            pl.BlockSpec((tm, W), lambda i: (i, OFF_U // W)),
            pl.BlockSpec((tm, W), lambda i: (i, OFF_V // W)),
            pl.BlockSpec((1, W), lambda i: (0, 0)),
            pl.BlockSpec((1, W), lambda i: (0, 0)),
            pl.BlockSpec((GMLP_GROUPS, GMLP_CHUNK, GMLP_CHUNK), lambda i: (0, 0, 0)),
            pl.BlockSpec((GMLP_CHUNK, GMLP_GROUPS), lambda i: (0, 0)),
        ],
        out_specs=pl.BlockSpec((tm, W), lambda i: (i, 0)),
        out_shape=jax.ShapeDtypeStruct((T, W), BF16),
        compiler_params=_cparams(("parallel",)),
        name="gmlp",
    )(z, z, lng, lnb, ws, bst)


def _rglru_kernel(xr_ref, rg_ref, cw_ref, cb_ref, wab_ref, ba_ref, bx_ref, lam_ref, o_ref,
                  ext_scr, a_scr, b_scr, hc_scr, *, tt):
    W = RNN_WIDTH
    groups = tt // 8
    sub_row = lax.broadcasted_iota(jnp.int32, (groups, 8, RNN_BLOCK), 1)

    @pl.when(pl.program_id(1) == 0)
    def _():
        ext_scr[0:8, :] = jnp.zeros((8, W), F32)
        hc_scr[...] = jnp.zeros((8, W), F32)

    xr = xr_ref[...].astype(F32)
    ext_scr[8:8 + tt, :] = xr
    cw = cw_ref[...]
    xc = (cw[3:4] * xr + cw[2:3] * ext_scr[7:7 + tt, :] + cw[1:2] * ext_scr[6:6 + tt, :]
          + cw[0:1] * ext_scr[5:5 + tt, :] + cb_ref[...])
    ext_scr[0:8, :] = xr[tt - 8:tt, :]

    lam = lam_ref[...]
    neg = -lam
    softplus = jnp.maximum(neg, 0.0) + jnp.log1p(jnp.exp(-jnp.abs(neg)))
    for k in range(W // RNN_BLOCK):
        cols = slice(k * RNN_BLOCK, (k + 1) * RNN_BLOCK)
        xck = xc[:, cols]
        gates = jnp.dot(xck.astype(BF16), wab_ref[k], preferred_element_type=F32)
        r = _sigmoid(gates[:, :RNN_BLOCK] + ba_ref[:, cols])
        i = _sigmoid(gates[:, RNN_BLOCK:] + bx_ref[:, cols])
        log_a = -LRU_C * r * softplus[:, cols]
        a = jnp.exp(log_a)
        one_minus_a2 = -jnp.tanh(log_a) * (1.0 + a * a)
        b_in = jnp.sqrt(one_minus_a2) * (i * xck)
        a3 = a.reshape(groups, 8, RNN_BLOCK)
        b3 = b_in.reshape(groups, 8, RNN_BLOCK)
        for step in (1, 2, 4):
            keep = sub_row >= step
            a_prev = jnp.where(keep, pltpu.roll(a3, step, axis=1), 1.0)
            b_prev = jnp.where(keep, pltpu.roll(b3, step, axis=1), 0.0)
            b3 = a3 * b_prev + b3
            a3 = a3 * a_prev
        a_scr[:, cols] = a3.reshape(tt, RNN_BLOCK)
        b_scr[:, cols] = b3.reshape(tt, RNN_BLOCK)

    def carry_step(i, h_prev):
        base = pl.multiple_of(i * BF16_SUBLANES, BF16_SUBLANES)
        gate = _gelu(rg_ref[pl.ds(base, BF16_SUBLANES), :].astype(F32))
        ys = []
        for half in range(BF16_SUBLANES // 8):
            rows = pl.ds(base + half * 8, 8)
            h = b_scr[rows, :] + a_scr[rows, :] * h_prev
            ys.append(h * gate[half * 8:(half + 1) * 8, :])
            h_prev = h[7:8, :]
        o_ref[pl.ds(base, BF16_SUBLANES), :] = jnp.concatenate(ys, axis=0).astype(o_ref.dtype)
        return h_prev

    h_last = lax.fori_loop(0, tt // BF16_SUBLANES, carry_step, hc_scr[0:1, :])
    hc_scr[...] = jnp.broadcast_to(h_last, (8, W))


def _rglru(z, cw, cb, wab, ba, bx, lam, *, B, S, tt=512):
    T = z.shape[0]
    W = RNN_WIDTH
    nt = S // tt
    return pl.pallas_call(
        functools.partial(_rglru_kernel, tt=tt),
        grid=(B, nt),
        in_specs=[
            pl.BlockSpec((tt, W), lambda b, t: (b * nt + t, OFF_XR // W)),
            pl.BlockSpec((tt, W), lambda b, t: (b * nt + t, OFF_RG // W)),
            pl.BlockSpec((CONV_WIDTH, W), lambda b, t: (0, 0)),
            pl.BlockSpec((1, W), lambda b, t: (0, 0)),
            pl.BlockSpec((W // RNN_BLOCK, RNN_BLOCK, 2 * RNN_BLOCK), lambda b, t: (0, 0, 0)),
            pl.BlockSpec((1, W), lambda b, t: (0, 0)),
            pl.BlockSpec((1, W), lambda b, t: (0, 0)),
            pl.BlockSpec((1, W), lambda b, t: (0, 0)),
        ],
        out_specs=pl.BlockSpec((tt, W), lambda b, t: (b * nt + t, 0)),
        out_shape=jax.ShapeDtypeStruct((T, W), BF16),
        scratch_shapes=[
            pltpu.VMEM((tt + 8, W), F32),
            pltpu.VMEM((tt, W), F32),
            pltpu.VMEM((tt, W), F32),
            pltpu.VMEM((8, W), F32),
        ],
        compiler_params=_cparams(("parallel", "arbitrary")),
        name="rglru",
    )(z, z, cw, cb, wab, ba, bx, lam)


def _compress_hidden(x_ref, pe_ref, w1_ref, x_scr):
    DH = NSA_HEAD_DIM
    half = CMP_STRIDE * DH
    slots = x_ref.shape[0] // CMP_STRIDE
    x = x_ref[...].astype(F32)
    for c in range(NSA_KV_WIDTH // LANES):
        x_scr[c] = x[:, c * LANES:(c + 1) * LANES]
    phase = [[x_scr[c, pl.ds(l, slots, stride=CMP_STRIDE), :] for c in range(NSA_KV_WIDTH // LANES)]
             for l in range(CMP_STRIDE)]
    per_block = LANES // DH

    def group_rows(g):
        lanes = slice((g % per_block) * DH, (g % per_block + 1) * DH)
        return jnp.concatenate([phase[l][g // per_block][:, lanes] for l in range(CMP_STRIDE)], axis=1)

    hm = jnp.concatenate([group_rows(g) for g in range(NSA_KV_GROUPS)], axis=0).astype(BF16)
    rows = hm.shape[0]
    lo = jnp.dot(hm, w1_ref[0:half, :], preferred_element_type=F32)
    hi = jnp.dot(hm, w1_ref[half:2 * half, :], preferred_element_type=F32)
    pe = jnp.broadcast_to(pe_ref[...], (8, 2 * half)).astype(BF16)
    pe_term = jnp.dot(pe, w1_ref[...], preferred_element_type=F32)[0:1, :]
    return _gelu(lo + pltpu.roll(hi, rows - 1, axis=0) + pe_term).astype(BF16)


def _compress_kernel(xk_ref, xv_ref, pek_ref, pev_ref, wk1_ref, wk2_ref, wv1_ref, wv2t_ref, kc_ref, vct_ref, x_scr):
    kc_ref[...] = jnp.dot(_compress_hidden(xk_ref, pek_ref, wk1_ref, x_scr), wk2_ref[...],
                          preferred_element_type=F32)
    vct_ref[...] = lax.dot_general(wv2t_ref[...], _compress_hidden(xv_ref, pev_ref, wv1_ref, x_scr), _NT,
                                   preferred_element_type=F32)


def _compress(z, pek, pev, wk1, wk2, wv1, wv2t, *, B, S):
    G, DH = NSA_KV_GROUPS, NSA_HEAD_DIM
    slots = S // CMP_STRIDE
    KVW = NSA_KV_WIDTH
    full = lambda shape: pl.BlockSpec(shape, lambda b: (0,) * len(shape))
    return pl.pallas_call(
        _compress_kernel,
        grid=(B,),
        in_specs=[pl.BlockSpec((S, KVW), lambda b: (b, OFF_KC // KVW)),
                  pl.BlockSpec((S, KVW), lambda b: (b, OFF_KC // KVW + 1)),
                  full(pek.shape), full(pev.shape), full(wk1.shape), full(wk2.shape), full(wv1.shape),
                  full(wv2t.shape)],
        out_specs=[pl.BlockSpec((G * slots, DH), lambda b: (b, 0)), pl.BlockSpec((DH, G * slots), lambda b: (0, b))],
        out_shape=[jax.ShapeDtypeStruct((B * G * slots, DH), F32), jax.ShapeDtypeStruct((DH, B * G * slots), F32)],
        scratch_shapes=[pltpu.VMEM((KVW // LANES, S, LANES), F32)],
        compiler_params=_cparams(("parallel",)),
        name="nsa_compress",
    )(z, z, pek, pev, wk1, wk2, wv1, wv2t)


def _nsa_kernel(q_ref, kc_ref, vct_ref, ks_ref, vs_ref, kw_ref, vw_ref, gate_ref, o_ref,
                ks_scr, kw_scr, vs_scr, vw_scr, *tile_scratch, tq, S):
    DH = NSA_HEAD_DIM
    tk = tq
    v_rows = DH + BF16_SUBLANES
    g_idx = pl.program_id(1)

    for g in range(NSA_KV_GROUPS):
        @pl.when(g_idx == g)
        def _():
            lanes = slice(g * DH, (g + 1) * DH)
            ks_scr[...] = ks_ref[:, lanes]
            kw_scr[...] = kw_ref[:, lanes]
            ones = jnp.ones((BF16_SUBLANES, tk), BF16)
            for v_ref, v_scr in ((vs_ref, vs_scr), (vw_ref, vw_scr)):
                vt = v_ref[:, lanes].T
                for kt in range(S // tk):
                    v_scr[kt, 0:DH, :] = vt[:, kt * tk:(kt + 1) * tk]
                    v_scr[kt, DH:v_rows, :] = ones

    def q_tile(qi, carry):
        rows = pl.ds(pl.multiple_of(qi * tq, tq), tq)
        _nsa_q_tile(qi, g_idx, q_ref.at[rows, :], kc_ref, vct_ref, gate_ref.at[rows, :], o_ref.at[rows, :],
                    ks_scr, kw_scr, vs_scr, vw_scr, *tile_scratch, tq=tq, S=S)
        return carry

    lax.fori_loop(0, S // tq, q_tile, 0)


def _nsa_q_tile(qi, g_idx, q_ref, kc_ref, vct_ref, gate_ref, o_ref,
                ks_scr, kw_scr, vs_scr, vw_scr, gt_scr, ps_scr, sel_scr, sa_scr, sb_scr, sc_scr, m_scr, acc_scr,
                *, tq, S):
    HP, DH = NSA_HPG, NSA_HEAD_DIM
    M = HP * tq
    tk = tq
    n_sel = S // SEL_BLOCK
    slots = kc_ref.shape[0]
    v_rows = DH + BF16_SUBLANES
    t0 = qi * tq

    def per_head(x):
        return jnp.concatenate([x] * HP, axis=1)

    q_t = q_ref[...].T
    qt = jnp.concatenate([q_t[h * DH:(h + 1) * DH, :] for h in range(HP)], axis=1)

    def scores(k_scr, kt, dst):
        start = pl.multiple_of(kt * tk, tk)
        dst[:, 0:M] = jnp.dot(k_scr[pl.ds(start, tk), :], qt, preferred_element_type=F32)

    sc = jnp.dot(kc_ref[...].astype(BF16), qt, preferred_element_type=F32)
    scores(ks_scr, 0, sa_scr)
    n_idx = lax.broadcasted_iota(jnp.int32, (slots, tq), 0)
    t_idx = t0 + lax.broadcasted_iota(jnp.int32, (slots, tq), 1)
    valid = per_head(jnp.where(n_idx * CMP_STRIDE + (CMP_BLOCK - 1) <= t_idx, 1.0, 0.0)) > 0.5
    sc = jnp.where(valid, sc, NEG_INF)
    mx = jnp.max(sc, axis=0, keepdims=True)
    p = jnp.where(valid, jnp.exp2(sc - mx), 0.0)
    den = jnp.sum(p, axis=0, keepdims=True)
    p_c = p * (1.0 / jnp.where(den > 0.0, den, 1.0))
    o_cmp = jnp.dot(vct_ref[...].astype(BF16), p_c.astype(BF16), preferred_element_type=F32)

    p_sum = p_c[:, 0:tq]
    for h in range(1, HP):
        p_sum = p_sum + p_c[:, h * tq:(h + 1) * tq]
    per_sel = SEL_BLOCK // CMP_STRIDE
    for c in range(tq // LANES):
        ps_scr[c] = p_sum[:, c * LANES:(c + 1) * LANES]
    every = [jnp.concatenate([ps_scr[c, pl.ds(r, n_sel, stride=per_sel), :] for c in range(tq // LANES)], axis=1)
             for r in range(per_sel)]
    j_idx = lax.broadcasted_iota(jnp.int32, (n_sel, tq), 0)
    before = jnp.where(j_idx >= 1, pltpu.roll(every[per_sel - 1], 1, axis=0), 0.0)
    imp = every[0]
    for r in range(1, per_sel - 1):
        imp = imp + every[r]
    imp = imp + 0.5 * every[per_sel - 1] + 0.5 * before
    cur = (t0 + lax.broadcasted_iota(jnp.int32, (n_sel, tq), 1)) // SEL_BLOCK
    forced = (j_idx == 0) | (j_idx == cur) | (j_idx == cur - 1)
    imp = jnp.where(forced, FORCE_SCORE, jnp.where(j_idx > cur, NEG_INF, imp))
    rank = jnp.zeros((n_sel, tq), F32)
    for i in range(n_sel):
        ci = imp[i:i + 1, :]
        ge = jnp.where(ci >= imp, 1.0, 0.0)
        gt = jnp.where(ci > imp, 1.0, 0.0)
        rank = rank + jnp.where(j_idx > i, ge, gt)
    sel_scr[...] = jnp.where(rank < float(min(SEL_TOP_N, n_sel)), 1.0, 0.0)

    per_tile = tk // SEL_BLOCK
    kloc = lax.broadcasted_iota(jnp.int32, (tk, tq), 0)
    tloc = lax.broadcasted_iota(jnp.int32, (tk, tq), 1)
    key_ahead = kloc - tloc

    def slc_bias(kt):
        hit = jnp.concatenate(
            [jnp.broadcast_to(sel_scr[pl.ds(kt * per_tile + jj, 1), :], (SEL_BLOCK, tq)) for jj in range(per_tile)],
            axis=0)
        return jnp.where((hit > 0.5) & (key_ahead <= (qi - kt) * tk), 0.0, NEG_INF)

    def absorb(v_ext, src, bias):
        s = src[:, 0:M] if bias is None else src[:, 0:M] + per_head(bias)
        m_old = m_scr[...]
        m_new = jnp.maximum(m_old, jnp.max(s, axis=0, keepdims=True))
        alpha = jnp.exp2(m_old - m_new)
        pr = jnp.exp2(s - m_new).astype(BF16)
        m_scr[...] = m_new
        acc_scr[...] = alpha * acc_scr[...] + jnp.dot(v_ext, pr, preferred_element_type=F32)

    def reset():
        m_scr[...] = jnp.full((1, M), NEG_INF, F32)
        acc_scr[...] = jnp.zeros((v_rows, M), F32)

    def result():
        acc = acc_scr[...]
        return acc[0:DH, :] * (1.0 / acc[DH:DH + 1, :])

    reset()
    n_tiles = qi + 1
    slc_bufs = (sa_scr, sb_scr)

    def slc_run(first, count, look_ahead):
        for u in range(count):
            if look_ahead or u + 1 < count:
                scores(ks_scr, jnp.minimum(first + u + 1, qi), slc_bufs[(u + 1) % 2])
            absorb(vs_scr[first + u], slc_bufs[u % 2], slc_bias(first + u))

    def slc_trip(j, carry):
        slc_run(SLC_UNROLL * j, SLC_UNROLL, True)
        return carry

    lax.fori_loop(0, n_tiles // SLC_UNROLL, slc_trip, 0)
    for rem in range(1, SLC_UNROLL):
        @pl.when(n_tiles % SLC_UNROLL == rem)
        def _():
            slc_run(n_tiles - rem, rem, False)

    o_slc = result()

    reset()
    n_back = WINDOW // tk
    diag_bias = jnp.where(kloc <= tloc, 0.0, NEG_INF)

    def window_start(n_win):
        bufs = (sa_scr, sb_scr)
        scores(kw_scr, qi, bufs[0])
        for d in range(n_win):
            if d + 1 < n_win:
                scores(kw_scr, qi - (d + 1), bufs[(d + 1) % 2])
            absorb(vw_scr[qi - d], bufs[d % 2], diag_bias if d == 0 else None)

    def window_full():
        scores(kw_scr, qi, sa_scr)
        scores(kw_scr, qi - n_back, sb_scr)
        scores(kw_scr, qi - 1, sc_scr)
        near = per_head(jnp.where(kloc <= tloc, 1.0, 0.0)) > 0.5
        s = jnp.where(near, sa_scr[:, 0:M], sb_scr[:, 0:M])
        m_old = m_scr[...]
        m_new = jnp.maximum(m_old, jnp.max(s, axis=0, keepdims=True))
        alpha = jnp.exp2(m_old - m_new)
        pr = jnp.exp2(s - m_new).astype(BF16)
        m_scr[...] = m_new
        none = jnp.zeros_like(pr)
        acc_scr[...] = (alpha * acc_scr[...]
                        + jnp.dot(vw_scr[qi], jnp.where(near, pr, none), preferred_element_type=F32)
                        + jnp.dot(vw_scr[qi - n_back], jnp.where(near, none, pr), preferred_element_type=F32))
        absorb(vw_scr[qi - 1], sc_scr, None)

    for n_win in range(1, n_back + 1):
        @pl.when(qi == n_win - 1)
        def _():
            window_start(n_win)

    @pl.when(qi >= n_back)
    def _():
        window_full()

    o_win = result()

    gt_scr[...] = gate_ref[...].astype(F32).T

    def gate(branch):
        rows = [gt_scr[pl.ds(g_idx * (HP * N_BRANCH) + h * N_BRANCH + branch, 1), :] for h in range(HP)]
        return _sigmoid(jnp.concatenate(rows, axis=1))

    o = gate(0) * o_cmp + gate(1) * o_slc + gate(2) * o_win
    o_ref[...] = jnp.concatenate([o[:, h * tq:(h + 1) * tq].T for h in range(HP)], axis=1).astype(o_ref.dtype)


def _nsa(z, kc, vct, *, B, S, tq=256):
    G, HP, DH = NSA_KV_GROUPS, NSA_HPG, NSA_HEAD_DIM
    assert WINDOW == 2 * tq and CMP_BLOCK == 2 * CMP_STRIDE and S % tq == 0, (WINDOW, tq, S)
    T = B * S
    slots = S // CMP_STRIDE
    n_sel = S // SEL_BLOCK
    n_kt = S // tq
    v_rows = DH + BF16_SUBLANES
    KVW = NSA_KV_WIDTH

    def kv_spec(which):
        return pl.BlockSpec((S, KVW), lambda b, g: (b, OFF_KV // KVW + which))

    return pl.pallas_call(
        functools.partial(_nsa_kernel, tq=tq, S=S),
        grid=(B, G),
        in_specs=[
            pl.BlockSpec((S, HP * DH), lambda b, g: (b, OFF_Q // (HP * DH) + g)),
            pl.BlockSpec((slots, DH), lambda b, g: (b * G + g, 0)),
            pl.BlockSpec((DH, slots), lambda b, g: (0, b * G + g)),
            kv_spec(0), kv_spec(1), kv_spec(2), kv_spec(3),
            pl.BlockSpec((S, LANES), lambda b, g: (b, OFF_NG // LANES)),
        ],
        out_specs=pl.BlockSpec((S, HP * DH), lambda b, g: (b, g)),
        out_shape=jax.ShapeDtypeStruct((T, NSA_Q_WIDTH), BF16),
        scratch_shapes=[
            pltpu.VMEM((S, DH), BF16),
            pltpu.VMEM((S, DH), BF16),
            pltpu.VMEM((n_kt, v_rows, tq), BF16),
            pltpu.VMEM((n_kt, v_rows, tq), BF16),
            pltpu.VMEM((LANES, tq), F32),
            pltpu.VMEM((tq // LANES, slots, LANES), F32),
            pltpu.VMEM((n_sel, tq), F32),
            pltpu.VMEM((tq, HP * tq + LANES), F32),
            pltpu.VMEM((tq, HP * tq + LANES), F32),
            pltpu.VMEM((tq, HP * tq + LANES), F32),
            pltpu.VMEM((1, HP * tq), F32),
            pltpu.VMEM((v_rows, HP * tq), F32),
        ],
        compiler_params=_cparams(("parallel", "parallel")),
        name="nsa_attention",
    )(z, kc, vct, z, z, z, z, z)


def _merge_kernel(ya_ref, yb_ref, yc_ref, mga_ref, mgb_ref, mgc_ref, x_ref, wa_ref, wb_ref, wc_ref, wo_ref,
                  gpost_ref, gpre_ref, x1_ref, hf_ref):
    gate = lambda ref: _sigmoid(ref[...].astype(F32))
    merged = gate(mga_ref) * jnp.dot(ya_ref[...], wa_ref[...], preferred_element_type=F32)
    merged = merged + gate(mgb_ref) * jnp.dot(yb_ref[...], wb_ref[...], preferred_element_type=F32)
    merged = merged + gate(mgc_ref) * jnp.dot(yc_ref[...], wc_ref[...], preferred_element_type=F32)
    y = jnp.dot(merged.astype(BF16), wo_ref[...], preferred_element_type=F32)
    x1 = x_ref[...] + _rms(y, gpost_ref[...])
    x1_ref[...] = x1
    hf_ref[...] = _rms(x1, gpre_ref[...]).astype(BF16)


def _merge(ya, yb, yc, z, x2d, wa, wb, wc, wo, layer, gpost, gpre, *, tm=512):
    T, D = x2d.shape
    row = lambda c: pl.BlockSpec((tm, D), lambda i: (i, c))
    wfull = pl.BlockSpec((None, D, D), lambda i: (layer, 0, 0))
    vec = pl.BlockSpec((1, D), lambda i: (0, 0))
    mg0 = OFF_MG // D
    return pl.pallas_call(
        _merge_kernel,
        grid=(T // tm,),
        in_specs=[row(0), row(0), row(0), row(mg0), row(mg0 + 1), row(mg0 + 2), row(0),
                  wfull, wfull, wfull, wfull, vec, vec],
        out_specs=[row(0), row(0)],
        out_shape=[jax.ShapeDtypeStruct((T, D), F32), jax.ShapeDtypeStruct((T, D), BF16)],
        compiler_params=_cparams(("parallel",)),
        name="merge",
    )(ya, yb, yc, z, z, z, x2d, wa, wb, wc, wo, gpost, gpre)


def _ffn_kernel(hf_ref, x1_ref, win_ref, wout_ref, gpost_ref, o_ref, acc_scr, *, bounds):
    hf = hf_ref[...]
    for c, (lo, hi) in enumerate(zip(bounds[:-1], bounds[1:])):
        gate = jnp.dot(hf, win_ref[:, lo:hi], preferred_element_type=F32)
        up = jnp.dot(hf, win_ref[:, D_FF + lo:D_FF + hi], preferred_element_type=F32)
        act = (gate * _sigmoid(gate) * up).astype(BF16)
        part = jnp.dot(act, wout_ref[lo:hi, :], preferred_element_type=F32)
        if c == 0:
            acc_scr[...] = part
        else:
            acc_scr[...] += part
    o_ref[...] = x1_ref[...] + _rms(acc_scr[...], gpost_ref[...])


def _ffn(hf, x1, win, wout, layer, gpost, *, tm=512):
    T, D = x1.shape
    blocks = D_FF // V7X_MXU_WIDTH
    bounds = (0, (blocks + 1) // 2 * V7X_MXU_WIDTH, D_FF)
    row = pl.BlockSpec((tm, D), lambda i: (i, 0))
    return pl.pallas_call(
        functools.partial(_ffn_kernel, bounds=bounds),
        grid=(T // tm,),
        in_specs=[row, row,
                  pl.BlockSpec((None,) + win.shape[1:], lambda i: (layer, 0, 0), pipeline_mode=pl.Buffered(1)),
                  pl.BlockSpec((None,) + wout.shape[1:], lambda i: (layer, 0, 0), pipeline_mode=pl.Buffered(1)),
                  pl.BlockSpec((1, D), lambda i: (0, 0))],
        out_specs=row,
        out_shape=jax.ShapeDtypeStruct((T, D), F32),
        scratch_shapes=[pltpu.VMEM((tm, D), F32)],
        compiler_params=_cparams(("parallel",)),
        name="ffn",
    )(hf, x1, win, wout, gpost)


_SRC = [int(c) for c in np.cumsum([0, GMLP_WIDTH, GMLP_WIDTH, NSA_Q_WIDTH, 6 * NSA_KV_WIDTH, N_BRANCH * NSA_HEADS,
                                   RNN_WIDTH, RNN_WIDTH, N_BRANCH * D_MODEL])]


def _reorder_w_in_kernel(w_ref, o_ref):
    u0, _, q0, kv0, ng0, xr0, _, _, end = _SRC
    rest0 = kv0 + 2 * NSA_KV_WIDTH

    def put(dst, lo, hi, scale=None):
        x = w_ref[lo:hi, :]
        if scale is not None:
            x = x * scale
        o_ref[dst:dst + hi - lo, :] = x.astype(o_ref.dtype)

    put(OFF_U, u0, q0)
    put(OFF_XR, xr0, end)
    put(OFF_KC, kv0, rest0)
    put(OFF_NG, ng0, xr0)
    pad0 = OFF_NG + xr0 - ng0
    o_ref[pad0:OFF_Q, :] = jnp.zeros((OFF_Q - pad0, o_ref.shape[1]), o_ref.dtype)
    put(OFF_Q, q0, kv0, Q_SCALE)
    put(OFF_KV, rest0, ng0)


def _reorder_w_in(w_in, *, td=256):
    L, D, d_in = w_in.shape
    return pl.pallas_call(
        _reorder_w_in_kernel,
        grid=(L, D // td),
        in_specs=[pl.BlockSpec((None, d_in, td), lambda l, i: (l, 0, i))],
        out_specs=pl.BlockSpec((None, D_IN_PAD, td), lambda l, i: (l, 0, i)),
        out_shape=jax.ShapeDtypeStruct((L, D_IN_PAD, D), BF16),
        compiler_params=_cparams(("parallel", "parallel")),
        name="reorder_w_in",
    )(jnp.swapaxes(w_in, 1, 2))


def _block_diag_gates(wa, wx):
    per = RNN_BLOCK // RNN_HEAD_DIM
    nblk = RNN_HEADS // per
    eye = jnp.eye(per, dtype=wa.dtype)

    def bd(w):
        w = w.reshape(nblk, per, RNN_HEAD_DIM, RNN_HEAD_DIM)
        return jnp.einsum('kpio,pq->kpiqo', w, eye).reshape(nblk, RNN_BLOCK, RNN_BLOCK)

    return jnp.concatenate([bd(wa), bd(wx)], axis=-1).astype(BF16)


def _layer(x2d, B, S, wt_in_all, dense_all, layer, g_pre_mix, g_post_mix, g_pre_ffn, g_post_ffn,
           gmlp_ln_g, gmlp_ln_b, gmlp_ws, gmlp_bs,
           nsa_pe_k, nsa_pe_v, nsa_wk1, nsa_wk2, nsa_wv1, nsa_wv2,
           rnn_conv_w, rnn_conv_b, rnn_wa, rnn_ba, rnn_wx, rnn_bx, rnn_lam):
    row = lambda a: a.reshape(1, -1)
    z = _in_proj(x2d, row(g_pre_mix), wt_in_all, layer)

    y_a = _gmlp(z, row(gmlp_ln_g), row(gmlp_ln_b), gmlp_ws, gmlp_bs.T)
    y_c = _rglru(z, rnn_conv_w, row(rnn_conv_b), _block_diag_gates(rnn_wa, rnn_wx),
                 row(rnn_ba), row(rnn_bx), row(rnn_lam), B=B, S=S)

    kc, vct = _compress(z, nsa_pe_k.reshape(1, -1), nsa_pe_v.reshape(1, -1), nsa_wk1.astype(BF16),
                        nsa_wk2.astype(BF16), nsa_wv1.astype(BF16), nsa_wv2.T.astype(BF16), B=B, S=S)
    y_b = _nsa(z, kc, vct, B=B, S=S)

    x1, hf = _merge(y_a, y_b, y_c, z, x2d, *dense_all[:4], layer, row(g_post_mix), row(g_pre_ffn))
    return _ffn(hf, x1, *dense_all[4:], layer, row(g_post_ffn))


def kernel(x, g_pre_mix, g_post_mix, g_pre_ffn, g_post_ffn, w_in, gmlp_ln_g, gmlp_ln_b, gmlp_ws, gmlp_bs, nsa_pe_k, nsa_pe_v, nsa_wk1, nsa_wk2, nsa_wv1, nsa_wv2, rnn_conv_w, rnn_conv_b, rnn_wa, rnn_ba, rnn_wx, rnn_bx, rnn_lam, w_br_a, w_br_b, w_br_c, w_o, w_ffn_in, w_ffn_out):
    B, S, D = x.shape
    params = (g_pre_mix, g_post_mix, g_pre_ffn, g_post_ffn, gmlp_ln_g, gmlp_ln_b, gmlp_ws, gmlp_bs,
              nsa_pe_k, nsa_pe_v, nsa_wk1, nsa_wk2, nsa_wv1, nsa_wv2,
              rnn_conv_w, rnn_conv_b, rnn_wa, rnn_ba, rnn_wx, rnn_bx, rnn_lam)
    wt_in_all = _reorder_w_in(w_in)
    dense_all = tuple(w.astype(BF16) for w in (w_br_a, w_br_b, w_br_c, w_o, w_ffn_in, w_ffn_out))
    x2d = x.reshape(B * S, D)
    for l in range(w_in.shape[0]):
        x2d = _layer(x2d, B, S, wt_in_all, dense_all, l, *(p[l] for p in params))
    return x2d.reshape(B, S, D)
```

```python
import functools

import jax
import jax.numpy as jnp
import numpy as np
from jax import lax
from jax.experimental import pallas as pl
from jax.experimental.pallas import tpu as pltpu

F32 = jnp.float32
BF16 = jnp.bfloat16

EPS = 1e-6
NEG_INF = -1e30
FORCE_SCORE = 1e4

D_MODEL = 1024
GMLP_WIDTH = 1024
GMLP_GROUPS = 4
GMLP_GROUP_DIM = GMLP_WIDTH // GMLP_GROUPS
GMLP_CHUNK = 128

NSA_HEADS = 16
NSA_KV_GROUPS = 4
NSA_HEAD_DIM = 64
NSA_HPG = NSA_HEADS // NSA_KV_GROUPS
NSA_Q_WIDTH = NSA_HEADS * NSA_HEAD_DIM
NSA_KV_WIDTH = NSA_KV_GROUPS * NSA_HEAD_DIM
N_BRANCH = 3
CMP_BLOCK = 32
CMP_STRIDE = 16
CMP_HIDDEN = 256
SEL_BLOCK = 64
SEL_TOP_N = 16
WINDOW = 512
Q_SCALE = NSA_HEAD_DIM ** -0.5 * float(np.log2(np.e))

RNN_WIDTH = 1024
RNN_HEADS = 16
RNN_HEAD_DIM = RNN_WIDTH // RNN_HEADS
CONV_WIDTH = 4
LRU_C = 8.0
RNN_BLOCK = 256

D_FF = 2816

OFF_U = 0
OFF_V = 1024
OFF_XR = 2048
OFF_RG = 3072
OFF_MG = 4096
OFF_KC = 7168
OFF_NG = OFF_KC + 2 * NSA_KV_WIDTH
NG_PAD = 512
OFF_Q = OFF_NG + NG_PAD
OFF_KV = OFF_Q + NSA_Q_WIDTH
D_IN_PAD = OFF_KV + 4 * NSA_KV_WIDTH

SLC_UNROLL = 4
LANES = 128
V7X_MXU_WIDTH = 256
BF16_SUBLANES = 16
V7X_VMEM_LIMIT = 56 * 1024 * 1024


def _cparams(sem, vmem=V7X_VMEM_LIMIT):
    return pltpu.CompilerParams(dimension_semantics=sem, vmem_limit_bytes=vmem)


def _rms(x, g):
    ms = jnp.mean(x * x, axis=-1, keepdims=True)
    return x * lax.rsqrt(ms + EPS) * g


def _gelu(x):
    return jax.nn.gelu(x)


def _sigmoid(x):
    return 0.5 * jnp.tanh(0.5 * x) + 0.5


_NT = (((1,), (1,)), ((), ()))


def _in_proj_kernel(x_ref, g_ref, w_ref, o_ref, h_scr):
    @pl.when(pl.program_id(1) == 0)
    def _():
        h_scr[...] = _rms(x_ref[...], g_ref[...]).astype(BF16)

    o_ref[...] = lax.dot_general(h_scr[...], w_ref[...], _NT, preferred_element_type=F32).astype(o_ref.dtype)


def _in_proj(x2d, g, wt_all, layer, *, tm=1024, tn=2048):
    T, D = x2d.shape
    N = wt_all.shape[1]
    return pl.pallas_call(
        _in_proj_kernel,
        grid=(T // tm, N // tn),
        in_specs=[
            pl.BlockSpec((tm, D), lambda i, j: (i, 0)),
            pl.BlockSpec((1, D), lambda i, j: (0, 0)),
            pl.BlockSpec((None, tn, D), lambda i, j: (layer, j, 0)),
        ],
        out_specs=pl.BlockSpec((tm, tn), lambda i, j: (i, j)),
        out_shape=jax.ShapeDtypeStruct((T, N), BF16),
        scratch_shapes=[pltpu.VMEM((tm, D), BF16)],
        compiler_params=_cparams(("parallel", "arbitrary")),
        name="in_proj",
    )(x2d, g, wt_all)


def _gmlp_kernel(u_ref, v_ref, lng_ref, lnb_ref, ws_ref, bst_ref, o_ref, *, n_chunks):
    C = GMLP_CHUNK
    row = lax.broadcasted_iota(jnp.int32, (C, C), 0)
    col = lax.broadcasted_iota(jnp.int32, (C, C), 1)
    causal = col <= row
    ws = [jnp.where(causal, ws_ref[g], 0.0).astype(BF16) for g in range(GMLP_GROUPS)]
    bst = bst_ref[...]
    for c in range(n_chunks):
        rows = slice(c * C, (c + 1) * C)
        gv = _gelu(v_ref[rows, :].astype(F32))
        mu = jnp.mean(gv, axis=-1, keepdims=True)
        d = gv - mu
        var = jnp.mean(d * d, axis=-1, keepdims=True)
        vn = (d * lax.rsqrt(var + EPS) * lng_ref[...] + lnb_ref[...]).astype(BF16)
        for g in range(GMLP_GROUPS):
            cols = slice(g * GMLP_GROUP_DIM, (g + 1) * GMLP_GROUP_DIM)
            mixed = jnp.dot(ws[g], vn[:, cols], preferred_element_type=F32) + bst[:, g:g + 1]
            o_ref[rows, cols] = (_gelu(u_ref[rows, cols].astype(F32)) * mixed).astype(o_ref.dtype)


def _gmlp(z, lng, lnb, ws, bst, *, n_chunks=8):
    T = z.shape[0]
    tm = GMLP_CHUNK * n_chunks
    W = GMLP_WIDTH
    return pl.pallas_call(
        functools.partial(_gmlp_kernel, n_chunks=n_chunks),
        grid=(T // tm,),
        in_specs=[
            pl.BlockSpec((tm, W), lambda i: (i, OFF_U // W)),
            pl.BlockSpec((tm, W), lambda i: (i, OFF_V // W)),
            pl.BlockSpec((1, W), lambda i: (0, 0)),
            pl.BlockSpec((1, W), lambda i: (0, 0)),
            pl.BlockSpec((GMLP_GROUPS, GMLP_CHUNK, GMLP_CHUNK), lambda i: (0, 0, 0)),
            pl.BlockSpec((GMLP_CHUNK, GMLP_GROUPS), lambda i: (0, 0)),
        ],
        out_specs=pl.BlockSpec((tm, W), lambda i: (i, 0)),
        out_shape=jax.ShapeDtypeStruct((T, W), BF16),
        compiler_params=_cparams(("parallel",)),
        name="gmlp",
    )(z, z, lng, lnb, ws, bst)


def _rglru_kernel(xr_ref, rg_ref, cw_ref, cb_ref, wab_ref, ba_ref, bx_ref, lam_ref, o_ref,
                  ext_scr, a_scr, b_scr, hc_scr, *, tt):
    W = RNN_WIDTH
    groups = tt // 8
    sub_row = lax.broadcasted_iota(jnp.int32, (groups, 8, RNN_BLOCK), 1)

    @pl.when(pl.program_id(1) == 0)
    def _():
        ext_scr[0:8, :] = jnp.zeros((8, W), F32)
        hc_scr[...] = jnp.zeros((8, W), F32)

    xr = xr_ref[...].astype(F32)
    ext_scr[8:8 + tt, :] = xr
    cw = cw_ref[...]
    xc = (cw[3:4] * xr + cw[2:3] * ext_scr[7:7 + tt, :] + cw[1:2] * ext_scr[6:6 + tt, :]
          + cw[0:1] * ext_scr[5:5 + tt, :] + cb_ref[...])
    ext_scr[0:8, :] = xr[tt - 8:tt, :]

    lam = lam_ref[...]
    neg = -lam
    softplus = jnp.maximum(neg, 0.0) + jnp.log1p(jnp.exp(-jnp.abs(neg)))
    for k in range(W // RNN_BLOCK):
        cols = slice(k * RNN_BLOCK, (k + 1) * RNN_BLOCK)
        xck = xc[:, cols]
        gates = jnp.dot(xck.astype(BF16), wab_ref[k], preferred_element_type=F32)
        r = _sigmoid(gates[:, :RNN_BLOCK] + ba_ref[:, cols])
        i = _sigmoid(gates[:, RNN_BLOCK:] + bx_ref[:, cols])
        log_a = -LRU_C * r * softplus[:, cols]
        a = jnp.exp(log_a)
        one_minus_a2 = -jnp.tanh(log_a) * (1.0 + a * a)
        b_in = jnp.sqrt(one_minus_a2) * (i * xck)
        a3 = a.reshape(groups, 8, RNN_BLOCK)
        b3 = b_in.reshape(groups, 8, RNN_BLOCK)
        for step in (1, 2, 4):
            keep = sub_row >= step
            a_prev = jnp.where(keep, pltpu.roll(a3, step, axis=1), 1.0)
            b_prev = jnp.where(keep, pltpu.roll(b3, step, axis=1), 0.0)
            b3 = a3 * b_prev + b3
            a3 = a3 * a_prev
        a_scr[:, cols] = a3.reshape(tt, RNN_BLOCK)
        b_scr[:, cols] = b3.reshape(tt, RNN_BLOCK)

    def carry_step(i, h_prev):
        base = pl.multiple_of(i * BF16_SUBLANES, BF16_SUBLANES)
        gate = _gelu(rg_ref[pl.ds(base, BF16_SUBLANES), :].astype(F32))
        ys = []
        for half in range(BF16_SUBLANES // 8):
            rows = pl.ds(base + half * 8, 8)
            h = b_scr[rows, :] + a_scr[rows, :] * h_prev
            ys.append(h * gate[half * 8:(half + 1) * 8, :])
            h_prev = h[7:8, :]
        o_ref[pl.ds(base, BF16_SUBLANES), :] = jnp.concatenate(ys, axis=0).astype(o_ref.dtype)
        return h_prev

    h_last = lax.fori_loop(0, tt // BF16_SUBLANES, carry_step, hc_scr[0:1, :])
    hc_scr[...] = jnp.broadcast_to(h_last, (8, W))


def _rglru(z, cw, cb, wab, ba, bx, lam, *, B, S, tt=512):
    T = z.shape[0]
    W = RNN_WIDTH
    nt = S // tt
    return pl.pallas_call(
        functools.partial(_rglru_kernel, tt=tt),
        grid=(B, nt),
        in_specs=[
            pl.BlockSpec((tt, W), lambda b, t: (b * nt + t, OFF_XR // W)),
            pl.BlockSpec((tt, W), lambda b, t: (b * nt + t, OFF_RG // W)),
            pl.BlockSpec((CONV_WIDTH, W), lambda b, t: (0, 0)),
            pl.BlockSpec((1, W), lambda b, t: (0, 0)),
            pl.BlockSpec((W // RNN_BLOCK, RNN_BLOCK, 2 * RNN_BLOCK), lambda b, t: (0, 0, 0)),
            pl.BlockSpec((1, W), lambda b, t: (0, 0)),
            pl.BlockSpec((1, W), lambda b, t: (0, 0)),
            pl.BlockSpec((1, W), lambda b, t: (0, 0)),
        ],
        out_specs=pl.BlockSpec((tt, W), lambda b, t: (b * nt + t, 0)),
        out_shape=jax.ShapeDtypeStruct((T, W), BF16),
        scratch_shapes=[
            pltpu.VMEM((tt + 8, W), F32),
            pltpu.VMEM((tt, W), F32),
            pltpu.VMEM((tt, W), F32),
            pltpu.VMEM((8, W), F32),
        ],
        compiler_params=_cparams(("parallel", "arbitrary")),
        name="rglru",
    )(z, z, cw, cb, wab, ba, bx, lam)


def _compress_hidden(x_ref, pe_ref, w1_ref, x_scr):
    DH = NSA_HEAD_DIM
    half = CMP_STRIDE * DH
    slots = x_ref.shape[0] // CMP_STRIDE
    x = x_ref[...].astype(F32)
    for c in range(NSA_KV_WIDTH // LANES):
        x_scr[c] = x[:, c * LANES:(c + 1) * LANES]
    phase = [[x_scr[c, pl.ds(l, slots, stride=CMP_STRIDE), :] for c in range(NSA_KV_WIDTH // LANES)]
             for l in range(CMP_STRIDE)]
    per_block = LANES // DH

    def group_rows(g):
        lanes = slice((g % per_block) * DH, (g % per_block + 1) * DH)
        return jnp.concatenate([phase[l][g // per_block][:, lanes] for l in range(CMP_STRIDE)], axis=1)

    hm = jnp.concatenate([group_rows(g) for g in range(NSA_KV_GROUPS)], axis=0).astype(BF16)
    rows = hm.shape[0]
    lo = jnp.dot(hm, w1_ref[0:half, :], preferred_element_type=F32)
    hi = jnp.dot(hm, w1_ref[half:2 * half, :], preferred_element_type=F32)
    pe = jnp.broadcast_to(pe_ref[...], (8, 2 * half)).astype(BF16)
    pe_term = jnp.dot(pe, w1_ref[...], preferred_element_type=F32)[0:1, :]
    return _gelu(lo + pltpu.roll(hi, rows - 1, axis=0) + pe_term).astype(BF16)


def _compress_kernel(xk_ref, xv_ref, pek_ref, pev_ref, wk1_ref, wk2_ref, wv1_ref, wv2t_ref, kc_ref, vct_ref, x_scr):
    kc_ref[...] = jnp.dot(_compress_hidden(xk_ref, pek_ref, wk1_ref, x_scr), wk2_ref[...],
                          preferred_element_type=F32)
    vct_ref[...] = lax.dot_general(wv2t_ref[...], _compress_hidden(xv_ref, pev_ref, wv1_ref, x_scr), _NT,
                                   preferred_element_type=F32)


def _compress(z, pek, pev, wk1, wk2, wv1, wv2t, *, B, S):
    G, DH = NSA_KV_GROUPS, NSA_HEAD_DIM
    slots = S // CMP_STRIDE
    KVW = NSA_KV_WIDTH
    full = lambda shape: pl.BlockSpec(shape, lambda b: (0,) * len(shape))
    return pl.pallas_call(
        _compress_kernel,
        grid=(B,),
        in_specs=[pl.BlockSpec((S, KVW), lambda b: (b, OFF_KC // KVW)),
                  pl.BlockSpec((S, KVW), lambda b: (b, OFF_KC // KVW + 1)),
                  full(pek.shape), full(pev.shape), full(wk1.shape), full(wk2.shape), full(wv1.shape),
                  full(wv2t.shape)],
        out_specs=[pl.BlockSpec((G * slots, DH), lambda b: (b, 0)), pl.BlockSpec((DH, G * slots), lambda b: (0, b))],
        out_shape=[jax.ShapeDtypeStruct((B * G * slots, DH), F32), jax.ShapeDtypeStruct((DH, B * G * slots), F32)],
        scratch_shapes=[pltpu.VMEM((KVW // LANES, S, LANES), F32)],
        compiler_params=_cparams(("parallel",)),
        name="nsa_compress",
    )(z, z, pek, pev, wk1, wk2, wv1, wv2t)


def _nsa_kernel(q_ref, kc_ref, vct_ref, ks_ref, vs_ref, kw_ref, vw_ref, gate_ref, o_ref,
                ks_scr, kw_scr, vs_scr, vw_scr, *tile_scratch, tq, S):
    DH = NSA_HEAD_DIM
    tk = tq
    v_rows = DH + BF16_SUBLANES
    g_idx = pl.program_id(1)

    for g in range(NSA_KV_GROUPS):
        @pl.when(g_idx == g)
        def _():
            lanes = slice(g * DH, (g + 1) * DH)
            ks_scr[...] = ks_ref[:, lanes]
            kw_scr[...] = kw_ref[:, lanes]
            ones = jnp.ones((BF16_SUBLANES, tk), BF16)
            for v_ref, v_scr in ((vs_ref, vs_scr), (vw_ref, vw_scr)):
                vt = v_ref[:, lanes].T
                for kt in range(S // tk):
                    v_scr[kt, 0:DH, :] = vt[:, kt * tk:(kt + 1) * tk]
                    v_scr[kt, DH:v_rows, :] = ones

    def q_tile(qi, carry):
        rows = pl.ds(pl.multiple_of(qi * tq, tq), tq)
        _nsa_q_tile(qi, g_idx, q_ref.at[rows, :], kc_ref, vct_ref, gate_ref.at[rows, :], o_ref.at[rows, :],
                    ks_scr, kw_scr, vs_scr, vw_scr, *tile_scratch, tq=tq, S=S)
        return carry

    lax.fori_loop(0, S // tq, q_tile, 0)


def _nsa_q_tile(qi, g_idx, q_ref, kc_ref, vct_ref, gate_ref, o_ref,
                ks_scr, kw_scr, vs_scr, vw_scr, gt_scr, ps_scr, sel_scr, sa_scr, sb_scr, wa_scr, wb_scr, wc_scr,
                m_scr, acc_scr, *, tq, S):
    HP, DH = NSA_HPG, NSA_HEAD_DIM
    M = HP * tq
    tk = tq
    n_sel = S // SEL_BLOCK
    slots = kc_ref.shape[0]
    v_rows = DH + BF16_SUBLANES
    t0 = qi * tq

    def per_head(x):
        return jnp.concatenate([x] * HP, axis=1)

    q_t = q_ref[...].T
    qt = jnp.concatenate([q_t[h * DH:(h + 1) * DH, :] for h in range(HP)], axis=1)

    def scores(k_scr, kt, dst):
        start = pl.multiple_of(kt * tk, tk)
        dst[:, 0:M] = jnp.dot(k_scr[pl.ds(start, tk), :], qt, preferred_element_type=F32)

    sc = jnp.dot(kc_ref[...].astype(BF16), qt, preferred_element_type=F32)
    scores(ks_scr, 0, sa_scr)
    n_back = WINDOW // tk
    win_bufs = (wa_scr, wb_scr, wc_scr)
    for d in range(n_back + 1):
        scores(kw_scr, jnp.maximum(qi - d, 0), win_bufs[d])
    n_idx = lax.broadcasted_iota(jnp.int32, (slots, tq), 0)
    t_idx = t0 + lax.broadcasted_iota(jnp.int32, (slots, tq), 1)
    valid = per_head(jnp.where(n_idx * CMP_STRIDE + (CMP_BLOCK - 1) <= t_idx, 1.0, 0.0)) > 0.5
    sc = jnp.where(valid, sc, NEG_INF)
    mx = jnp.max(sc, axis=0, keepdims=True)
    p = jnp.where(valid, jnp.exp2(sc - mx), 0.0)
    den = jnp.sum(p, axis=0, keepdims=True)
    p_c = p * (1.0 / jnp.where(den > 0.0, den, 1.0))
    o_cmp = jnp.dot(vct_ref[...].astype(BF16), p_c.astype(BF16), preferred_element_type=F32)

    p_sum = p_c[:, 0:tq]
    for h in range(1, HP):
        p_sum = p_sum + p_c[:, h * tq:(h + 1) * tq]
    per_sel = SEL_BLOCK // CMP_STRIDE
    for c in range(tq // LANES):
        ps_scr[c] = p_sum[:, c * LANES:(c + 1) * LANES]
    every = [jnp.concatenate([ps_scr[c, pl.ds(r, n_sel, stride=per_sel), :] for c in range(tq // LANES)], axis=1)
             for r in range(per_sel)]
    j_idx = lax.broadcasted_iota(jnp.int32, (n_sel, tq), 0)
    before = jnp.where(j_idx >= 1, pltpu.roll(every[per_sel - 1], 1, axis=0), 0.0)
    imp = every[0]
    for r in range(1, per_sel - 1):
        imp = imp + every[r]
    imp = imp + 0.5 * every[per_sel - 1] + 0.5 * before
    cur = (t0 + lax.broadcasted_iota(jnp.int32, (n_sel, tq), 1)) // SEL_BLOCK
    forced = (j_idx == 0) | (j_idx == cur) | (j_idx == cur - 1)
    imp = jnp.where(forced, FORCE_SCORE, jnp.where(j_idx > cur, NEG_INF, imp))
    rank = jnp.zeros((n_sel, tq), F32)
    for i in range(n_sel):
        ci = imp[i:i + 1, :]
        ge = jnp.where(ci >= imp, 1.0, 0.0)
        gt = jnp.where(ci > imp, 1.0, 0.0)
        rank = rank + jnp.where(j_idx > i, ge, gt)
    sel_scr[...] = jnp.where(rank < float(min(SEL_TOP_N, n_sel)), 1.0, 0.0)

    per_tile = tk // SEL_BLOCK
    kloc = lax.broadcasted_iota(jnp.int32, (tk, tq), 0)
    tloc = lax.broadcasted_iota(jnp.int32, (tk, tq), 1)
    key_ahead = kloc - tloc

    def slc_bias(kt):
        hit = jnp.concatenate(
            [jnp.broadcast_to(sel_scr[pl.ds(kt * per_tile + jj, 1), :], (SEL_BLOCK, tq)) for jj in range(per_tile)],
            axis=0)
        return jnp.where((hit > 0.5) & (key_ahead <= (qi - kt) * tk), 0.0, NEG_INF)

    def absorb(v_ext, src, bias):
        s = src[:, 0:M] if bias is None else src[:, 0:M] + per_head(bias)
        m_old = m_scr[...]
        m_new = jnp.maximum(m_old, jnp.max(s, axis=0, keepdims=True))
        alpha = jnp.exp2(m_old - m_new)
        pr = jnp.exp2(s - m_new).astype(BF16)
        m_scr[...] = m_new
        acc_scr[...] = alpha * acc_scr[...] + jnp.dot(v_ext, pr, preferred_element_type=F32)

    def reset():
        m_scr[...] = jnp.full((1, M), NEG_INF, F32)
        acc_scr[...] = jnp.zeros((v_rows, M), F32)

    def result():
        acc = acc_scr[...]
        return acc[0:DH, :] * (1.0 / acc[DH:DH + 1, :])

    reset()
    n_tiles = qi + 1
    slc_bufs = (sa_scr, sb_scr)

    def slc_run(first, count, look_ahead):
        for u in range(count):
            if look_ahead or u + 1 < count:
                scores(ks_scr, jnp.minimum(first + u + 1, qi), slc_bufs[(u + 1) % 2])
            absorb(vs_scr[first + u], slc_bufs[u % 2], slc_bias(first + u))

    def slc_trip(j, carry):
        slc_run(SLC_UNROLL * j, SLC_UNROLL, True)
        return carry

    lax.fori_loop(0, n_tiles // SLC_UNROLL, slc_trip, 0)
    for rem in range(1, SLC_UNROLL):
        @pl.when(n_tiles % SLC_UNROLL == rem)
        def _():
            slc_run(n_tiles - rem, rem, False)

    o_slc = result()

    reset()
    diag_bias = jnp.where(kloc <= tloc, 0.0, NEG_INF)

    def window_start(n_win):
        for d in range(n_win):
            absorb(vw_scr[qi - d], win_bufs[d], diag_bias if d == 0 else None)

    def window_full():
        near = per_head(jnp.where(kloc <= tloc, 1.0, 0.0)) > 0.5
        s = jnp.where(near, win_bufs[0][:, 0:M], win_bufs[n_back][:, 0:M])
        m_old = m_scr[...]
        m_new = jnp.maximum(m_old, jnp.max(s, axis=0, keepdims=True))
        alpha = jnp.exp2(m_old - m_new)
        pr = jnp.exp2(s - m_new).astype(BF16)
        m_scr[...] = m_new
        none = jnp.zeros_like(pr)
        acc_scr[...] = (alpha * acc_scr[...]
                        + jnp.dot(vw_scr[qi], jnp.where(near, pr, none), preferred_element_type=F32)
                        + jnp.dot(vw_scr[qi - n_back], jnp.where(near, none, pr), preferred_element_type=F32))
        absorb(vw_scr[qi - 1], win_bufs[1], None)

    for n_win in range(1, n_back + 1):
        @pl.when(qi == n_win - 1)
        def _():
            window_start(n_win)

    @pl.when(qi >= n_back)
    def _():
        window_full()

    o_win = result()

    gt_scr[...] = gate_ref[...].astype(F32).T

    def gate(branch):
        rows = [gt_scr[pl.ds(g_idx * (HP * N_BRANCH) + h * N_BRANCH + branch, 1), :] for h in range(HP)]
        return _sigmoid(jnp.concatenate(rows, axis=1))

    o = gate(0) * o_cmp + gate(1) * o_slc + gate(2) * o_win
    o_ref[...] = jnp.concatenate([o[:, h * tq:(h + 1) * tq].T for h in range(HP)], axis=1).astype(o_ref.dtype)


def _nsa(z, kc, vct, *, B, S, tq=256):
    G, HP, DH = NSA_KV_GROUPS, NSA_HPG, NSA_HEAD_DIM
    assert WINDOW == 2 * tq and CMP_BLOCK == 2 * CMP_STRIDE and S % tq == 0, (WINDOW, tq, S)
    T = B * S
    slots = S // CMP_STRIDE
    n_sel = S // SEL_BLOCK
    n_kt = S // tq
    v_rows = DH + BF16_SUBLANES
    KVW = NSA_KV_WIDTH

    def kv_spec(which):
        return pl.BlockSpec((S, KVW), lambda b, g: (b, OFF_KV // KVW + which))

    return pl.pallas_call(
        functools.partial(_nsa_kernel, tq=tq, S=S),
        grid=(B, G),
        in_specs=[
            pl.BlockSpec((S, HP * DH), lambda b, g: (b, OFF_Q // (HP * DH) + g)),
            pl.BlockSpec((slots, DH), lambda b, g: (b * G + g, 0)),
            pl.BlockSpec((DH, slots), lambda b, g: (0, b * G + g)),
            kv_spec(0), kv_spec(1), kv_spec(2), kv_spec(3),
            pl.BlockSpec((S, LANES), lambda b, g: (b, OFF_NG // LANES)),
        ],
        out_specs=pl.BlockSpec((S, HP * DH), lambda b, g: (b, g)),
        out_shape=jax.ShapeDtypeStruct((T, NSA_Q_WIDTH), BF16),
        scratch_shapes=[
            pltpu.VMEM((S, DH), BF16),
            pltpu.VMEM((S, DH), BF16),
            pltpu.VMEM((n_kt, v_rows, tq), BF16),
            pltpu.VMEM((n_kt, v_rows, tq), BF16),
            pltpu.VMEM((LANES, tq), F32),
            pltpu.VMEM((tq // LANES, slots, LANES), F32),
            pltpu.VMEM((n_sel, tq), F32),
            pltpu.VMEM((tq, HP * tq + LANES), F32),
            pltpu.VMEM((tq, HP * tq + LANES), F32),
            pltpu.VMEM((tq, HP * tq + LANES), F32),
            pltpu.VMEM((tq, HP * tq + LANES), F32),
            pltpu.VMEM((tq, HP * tq + LANES), F32),
            pltpu.VMEM((1, HP * tq), F32),
            pltpu.VMEM((v_rows, HP * tq), F32),
        ],
        compiler_params=_cparams(("parallel", "parallel")),
        name="nsa_attention",
    )(z, kc, vct, z, z, z, z, z)


def _merge_kernel(ya_ref, yb_ref, yc_ref, mga_ref, mgb_ref, mgc_ref, x_ref, wa_ref, wb_ref, wc_ref, wo_ref,
                  gpost_ref, gpre_ref, x1_ref, hf_ref):
    gate = lambda ref: _sigmoid(ref[...].astype(F32))
    merged = gate(mga_ref) * jnp.dot(ya_ref[...], wa_ref[...], preferred_element_type=F32)
    merged = merged + gate(mgb_ref) * jnp.dot(yb_ref[...], wb_ref[...], preferred_element_type=F32)
    merged = merged + gate(mgc_ref) * jnp.dot(yc_ref[...], wc_ref[...], preferred_element_type=F32)
    y = jnp.dot(merged.astype(BF16), wo_ref[...], preferred_element_type=F32)
    x1 = x_ref[...] + _rms(y, gpost_ref[...])
    x1_ref[...] = x1
    hf_ref[...] = _rms(x1, gpre_ref[...]).astype(BF16)


def _merge(ya, yb, yc, z, x2d, wa, wb, wc, wo, layer, gpost, gpre, *, tm=512):
    T, D = x2d.shape
    row = lambda c: pl.BlockSpec((tm, D), lambda i: (i, c))
    wfull = pl.BlockSpec((None, D, D), lambda i: (layer, 0, 0))
    vec = pl.BlockSpec((1, D), lambda i: (0, 0))
    mg0 = OFF_MG // D
    return pl.pallas_call(
        _merge_kernel,
        grid=(T // tm,),
        in_specs=[row(0), row(0), row(0), row(mg0), row(mg0 + 1), row(mg0 + 2), row(0),
                  wfull, wfull, wfull, wfull, vec, vec],
        out_specs=[row(0), row(0)],
        out_shape=[jax.ShapeDtypeStruct((T, D), F32), jax.ShapeDtypeStruct((T, D), BF16)],
        compiler_params=_cparams(("parallel",)),
        name="merge",
    )(ya, yb, yc, z, z, z, x2d, wa, wb, wc, wo, gpost, gpre)


def _ffn_kernel(hf_ref, x1_ref, win_ref, wout_ref, gpost_ref, o_ref, acc_scr, *, bounds):
    hf = hf_ref[...]
    for c, (lo, hi) in enumerate(zip(bounds[:-1], bounds[1:])):
        gate = jnp.dot(hf, win_ref[:, lo:hi], preferred_element_type=F32)
        up = jnp.dot(hf, win_ref[:, D_FF + lo:D_FF + hi], preferred_element_type=F32)
        act = (gate * _sigmoid(gate) * up).astype(BF16)
        part = jnp.dot(act, wout_ref[lo:hi, :], preferred_element_type=F32)
        if c == 0:
            acc_scr[...] = part
        else:
            acc_scr[...] += part
    o_ref[...] = x1_ref[...] + _rms(acc_scr[...], gpost_ref[...])


def _ffn(hf, x1, win, wout, layer, gpost, *, tm=512):
    T, D = x1.shape
    blocks = D_FF // V7X_MXU_WIDTH
    bounds = (0, (blocks + 1) // 2 * V7X_MXU_WIDTH, D_FF)
    row = pl.BlockSpec((tm, D), lambda i: (i, 0))
    return pl.pallas_call(
        functools.partial(_ffn_kernel, bounds=bounds),
        grid=(T // tm,),
        in_specs=[row, row,
                  pl.BlockSpec((None,) + win.shape[1:], lambda i: (layer, 0, 0), pipeline_mode=pl.Buffered(1)),
                  pl.BlockSpec((None,) + wout.shape[1:], lambda i: (layer, 0, 0), pipeline_mode=pl.Buffered(1)),
                  pl.BlockSpec((1, D), lambda i: (0, 0))],
        out_specs=row,
        out_shape=jax.ShapeDtypeStruct((T, D), F32),
        scratch_shapes=[pltpu.VMEM((tm, D), F32)],
        compiler_params=_cparams(("parallel",)),
        name="ffn",
    )(hf, x1, win, wout, gpost)


_SRC = [int(c) for c in np.cumsum([0, GMLP_WIDTH, GMLP_WIDTH, NSA_Q_WIDTH, 6 * NSA_KV_WIDTH, N_BRANCH * NSA_HEADS,
                                   RNN_WIDTH, RNN_WIDTH, N_BRANCH * D_MODEL])]


def _reorder_w_in_kernel(w_ref, o_ref):
    u0, _, q0, kv0, ng0, xr0, _, _, end = _SRC
    rest0 = kv0 + 2 * NSA_KV_WIDTH

    def put(dst, lo, hi, scale=None):
        x = w_ref[lo:hi, :]
        if scale is not None:
            x = x * scale
        o_ref[dst:dst + hi - lo, :] = x.astype(o_ref.dtype)

    put(OFF_U, u0, q0)
    put(OFF_XR, xr0, end)
    put(OFF_KC, kv0, rest0)
    put(OFF_NG, ng0, xr0)
    pad0 = OFF_NG + xr0 - ng0
    o_ref[pad0:OFF_Q, :] = jnp.zeros((OFF_Q - pad0, o_ref.shape[1]), o_ref.dtype)
    put(OFF_Q, q0, kv0, Q_SCALE)
    put(OFF_KV, rest0, ng0)


def _reorder_w_in(w_in, *, td=256):
    L, D, d_in = w_in.shape
    return pl.pallas_call(
        _reorder_w_in_kernel,
        grid=(L, D // td),
        in_specs=[pl.BlockSpec((None, d_in, td), lambda l, i: (l, 0, i))],
        out_specs=pl.BlockSpec((None, D_IN_PAD, td), lambda l, i: (l, 0, i)),
        out_shape=jax.ShapeDtypeStruct((L, D_IN_PAD, D), BF16),
        compiler_params=_cparams(("parallel", "parallel")),
        name="reorder_w_in",
    )(jnp.swapaxes(w_in, 1, 2))


def _block_diag_gates(wa, wx):
    per = RNN_BLOCK // RNN_HEAD_DIM
    nblk = RNN_HEADS // per
    eye = jnp.eye(per, dtype=wa.dtype)

    def bd(w):
        w = w.reshape(nblk, per, RNN_HEAD_DIM, RNN_HEAD_DIM)
        return jnp.einsum('kpio,pq->kpiqo', w, eye).reshape(nblk, RNN_BLOCK, RNN_BLOCK)

    return jnp.concatenate([bd(wa), bd(wx)], axis=-1).astype(BF16)


def _layer(x2d, B, S, wt_in_all, dense_all, layer, g_pre_mix, g_post_mix, g_pre_ffn, g_post_ffn,
           gmlp_ln_g, gmlp_ln_b, gmlp_ws, gmlp_bs,
           nsa_pe_k, nsa_pe_v, nsa_wk1, nsa_wk2, nsa_wv1, nsa_wv2,
           rnn_conv_w, rnn_conv_b, rnn_wa, rnn_ba, rnn_wx, rnn_bx, rnn_lam):
    row = lambda a: a.reshape(1, -1)
    z = _in_proj(x2d, row(g_pre_mix), wt_in_all, layer)

    y_a = _gmlp(z, row(gmlp_ln_g), row(gmlp_ln_b), gmlp_ws, gmlp_bs.T)
    y_c = _rglru(z, rnn_conv_w, row(rnn_conv_b), _block_diag_gates(rnn_wa, rnn_wx),
                 row(rnn_ba), row(rnn_bx), row(rnn_lam), B=B, S=S)

    kc, vct = _compress(z, nsa_pe_k.reshape(1, -1), nsa_pe_v.reshape(1, -1), nsa_wk1.astype(BF16),
                        nsa_wk2.astype(BF16), nsa_wv1.astype(BF16), nsa_wv2.T.astype(BF16), B=B, S=S)
    y_b = _nsa(z, kc, vct, B=B, S=S)

    x1, hf = _merge(y_a, y_b, y_c, z, x2d, *dense_all[:4], layer, row(g_post_mix), row(g_pre_ffn))
    return _ffn(hf, x1, *dense_all[4:], layer, row(g_post_ffn))


def kernel(x, g_pre_mix, g_post_mix, g_pre_ffn, g_post_ffn, w_in, gmlp_ln_g, gmlp_ln_b, gmlp_ws, gmlp_bs, nsa_pe_k, nsa_pe_v, nsa_wk1, nsa_wk2, nsa_wv1, nsa_wv2, rnn_conv_w, rnn_conv_b, rnn_wa, rnn_ba, rnn_wx, rnn_bx, rnn_lam, w_br_a, w_br_b, w_br_c, w_o, w_ffn_in, w_ffn_out):
    B, S, D = x.shape
    params = (g_pre_mix, g_post_mix, g_pre_ffn, g_post_ffn, gmlp_ln_g, gmlp_ln_b, gmlp_ws, gmlp_bs,
              nsa_pe_k, nsa_pe_v, nsa_wk1, nsa_wk2, nsa_wv1, nsa_wv2,
              rnn_conv_w, rnn_conv_b, rnn_wa, rnn_ba, rnn_wx, rnn_bx, rnn_lam)
    wt_in_all = _reorder_w_in(w_in)
    dense_all = tuple(w.astype(BF16) for w in (w_br_a, w_br_b, w_br_c, w_o, w_ffn_in, w_ffn_out))
    x2d = x.reshape(B * S, D)
    for l in range(w_in.shape[0]):
        x2d = _layer(x2d, B, S, wt_in_all, dense_all, l, *(p[l] for p in params))
    return x2d.reshape(B, S, D)
```

```python
import functools

import jax
import jax.numpy as jnp
import numpy as np
from jax import lax
from jax.experimental import pallas as pl
from jax.experimental.pallas import tpu as pltpu

F32 = jnp.float32
BF16 = jnp.bfloat16

EPS = 1e-6
NEG_INF = -1e30
FORCE_SCORE = 1e4

D_MODEL = 1024
GMLP_WIDTH = 1024
GMLP_GROUPS = 4
GMLP_GROUP_DIM = GMLP_WIDTH // GMLP_GROUPS
GMLP_CHUNK = 128

NSA_HEADS = 16
NSA_KV_GROUPS = 4
NSA_HEAD_DIM = 64
NSA_HPG = NSA_HEADS // NSA_KV_GROUPS
NSA_Q_WIDTH = NSA_HEADS * NSA_HEAD_DIM
NSA_KV_WIDTH = NSA_KV_GROUPS * NSA_HEAD_DIM
N_BRANCH = 3
CMP_BLOCK = 32
CMP_STRIDE = 16
CMP_HIDDEN = 256
SEL_BLOCK = 64
SEL_TOP_N = 16
WINDOW = 512
Q_SCALE = NSA_HEAD_DIM ** -0.5 * float(np.log2(np.e))

RNN_WIDTH = 1024
RNN_HEADS = 16
RNN_HEAD_DIM = RNN_WIDTH // RNN_HEADS
CONV_WIDTH = 4
LRU_C = 8.0
RNN_BLOCK = 256

D_FF = 2816

OFF_U = 0
OFF_V = 1024
OFF_XR = 2048
OFF_RG = 3072
OFF_MG = 4096
OFF_KC = 7168
OFF_NG = OFF_KC + 2 * NSA_KV_WIDTH
NG_PAD = 512
OFF_Q = OFF_NG + NG_PAD
OFF_KV = OFF_Q + NSA_Q_WIDTH
D_IN_PAD = OFF_KV + 4 * NSA_KV_WIDTH

SLC_UNROLL = 4
LANES = 128
V7X_MXU_WIDTH = 256
BF16_SUBLANES = 16
V7X_VMEM_LIMIT = 56 * 1024 * 1024


def _cparams(sem, vmem=V7X_VMEM_LIMIT):
    return pltpu.CompilerParams(dimension_semantics=sem, vmem_limit_bytes=vmem)


def _rms(x, g):
    ms = jnp.mean(x * x, axis=-1, keepdims=True)
    return x * lax.rsqrt(ms + EPS) * g


def _gelu(x):
    return jax.nn.gelu(x)


def _sigmoid(x):
    return 0.5 * jnp.tanh(0.5 * x) + 0.5


_NT = (((1,), (1,)), ((), ()))


def _in_proj_kernel(x_ref, g_ref, w_ref, o_ref, h_scr):
    @pl.when(pl.program_id(1) == 0)
    def _():
        h_scr[...] = _rms(x_ref[...], g_ref[...]).astype(BF16)

    o_ref[...] = lax.dot_general(h_scr[...], w_ref[...], _NT, preferred_element_type=F32).astype(o_ref.dtype)


def _in_proj(x2d, g, wt_all, layer, *, tm=2048, tn=2048):
    T, D = x2d.shape
    N = wt_all.shape[1]
    return pl.pallas_call(
        _in_proj_kernel,
        grid=(T // tm, N // tn),
        in_specs=[
            pl.BlockSpec((tm, D), lambda i, j: (i, 0)),
            pl.BlockSpec((1, D), lambda i, j: (0, 0)),
            pl.BlockSpec((None, tn, D), lambda i, j: (layer, j, 0)),
        ],
        out_specs=pl.BlockSpec((tm, tn), lambda i, j: (i, j)),
        out_shape=jax.ShapeDtypeStruct((T, N), BF16),
        scratch_shapes=[pltpu.VMEM((tm, D), BF16)],
        compiler_params=_cparams(("parallel", "arbitrary")),
        name="in_proj",
    )(x2d, g, wt_all)


def _gmlp_kernel(u_ref, v_ref, lng_ref, lnb_ref, ws_ref, bst_ref, o_ref, *, n_chunks):
    C = GMLP_CHUNK
    row = lax.broadcasted_iota(jnp.int32, (C, C), 0)
    col = lax.broadcasted_iota(jnp.int32, (C, C), 1)
    causal = col <= row
    ws = [jnp.where(causal, ws_ref[g], 0.0).astype(BF16) for g in range(GMLP_GROUPS)]
    bst = bst_ref[...]
    for c in range(n_chunks):
        rows = slice(c * C, (c + 1) * C)
        gv = _gelu(v_ref[rows, :].astype(F32))
        mu = jnp.mean(gv, axis=-1, keepdims=True)
        d = gv - mu
        var = jnp.mean(d * d, axis=-1, keepdims=True)
        vn = (d * lax.rsqrt(var + EPS) * lng_ref[...] + lnb_ref[...]).astype(BF16)
        for g in range(GMLP_GROUPS):
            cols = slice(g * GMLP_GROUP_DIM, (g + 1) * GMLP_GROUP_DIM)
            mixed = jnp.dot(ws[g], vn[:, cols], preferred_element_type=F32) + bst[:, g:g + 1]
            o_ref[rows, cols] = (_gelu(u_ref[rows, cols].astype(F32)) * mixed).astype(o_ref.dtype)


def _gmlp(z, lng, lnb, ws, bst, *, n_chunks=8):
    T = z.shape[0]
    tm = GMLP_CHUNK * n_chunks
    W = GMLP_WIDTH
    return pl.pallas_call(
        functools.partial(_gmlp_kernel, n_chunks=n_chunks),
        grid=(T // tm,),
        in_specs=[
            pl.BlockSpec((tm, W), lambda i: (i, OFF_U // W)),
            pl.BlockSpec((tm, W), lambda i: (i, OFF_V // W)),
            pl.BlockSpec((1, W), lambda i: (0, 0)),
            pl.BlockSpec((1, W), lambda i: (0, 0)),
            pl.BlockSpec((GMLP_GROUPS, GMLP_CHUNK, GMLP_CHUNK), lambda i: (0, 0, 0)),
            pl.BlockSpec((GMLP_CHUNK, GMLP_GROUPS), lambda i: (0, 0)),
        ],
        out_specs=pl.BlockSpec((tm, W), lambda i: (i, 0)),
        out_shape=jax.ShapeDtypeStruct((T, W), BF16),
        compiler_params=_cparams(("parallel",)),
        name="gmlp",
    )(z, z, lng, lnb, ws, bst)


def _rglru_kernel(xr_ref, rg_ref, cw_ref, cb_ref, wab_ref, ba_ref, bx_ref, lam_ref, o_ref,
                  ext_scr, a_scr, b_scr, hc_scr, *, tt):
    W = RNN_WIDTH
    groups = tt // 8
    sub_row = lax.broadcasted_iota(jnp.int32, (groups, 8, RNN_BLOCK), 1)

    @pl.when(pl.program_id(1) == 0)
    def _():
        ext_scr[0:8, :] = jnp.zeros((8, W), F32)
        hc_scr[...] = jnp.zeros((8, W), F32)

    xr = xr_ref[...].astype(F32)
    ext_scr[8:8 + tt, :] = xr
    cw = cw_ref[...]
    xc = (cw[3:4] * xr + cw[2:3] * ext_scr[7:7 + tt, :] + cw[1:2] * ext_scr[6:6 + tt, :]
          + cw[0:1] * ext_scr[5:5 + tt, :] + cb_ref[...])
    ext_scr[0:8, :] = xr[tt - 8:tt, :]

    lam = lam_ref[...]
    neg = -lam
    softplus = jnp.maximum(neg, 0.0) + jnp.log1p(jnp.exp(-jnp.abs(neg)))
    for k in range(W // RNN_BLOCK):
        cols = slice(k * RNN_BLOCK, (k + 1) * RNN_BLOCK)
        xck = xc[:, cols]
        gates = jnp.dot(xck.astype(BF16), wab_ref[k], preferred_element_type=F32)
        r = _sigmoid(gates[:, :RNN_BLOCK] + ba_ref[:, cols])
        i = _sigmoid(gates[:, RNN_BLOCK:] + bx_ref[:, cols])
        log_a = -LRU_C * r * softplus[:, cols]
        a = jnp.exp(log_a)
        one_minus_a2 = -jnp.tanh(log_a) * (1.0 + a * a)
        b_in = jnp.sqrt(one_minus_a2) * (i * xck)
        a3 = a.reshape(groups, 8, RNN_BLOCK)
        b3 = b_in.reshape(groups, 8, RNN_BLOCK)
        for step in (1, 2, 4):
            keep = sub_row >= step
            a_prev = jnp.where(keep, pltpu.roll(a3, step, axis=1), 1.0)
            b_prev = jnp.where(keep, pltpu.roll(b3, step, axis=1), 0.0)
            b3 = a3 * b_prev + b3
            a3 = a3 * a_prev
        a_scr[:, cols] = a3.reshape(tt, RNN_BLOCK)
        b_scr[:, cols] = b3.reshape(tt, RNN_BLOCK)

    def carry_step(i, h_prev):
        base = pl.multiple_of(i * BF16_SUBLANES, BF16_SUBLANES)
        gate = _gelu(rg_ref[pl.ds(base, BF16_SUBLANES), :].astype(F32))
        ys = []
        for half in range(BF16_SUBLANES // 8):
            rows = pl.ds(base + half * 8, 8)
            h = b_scr[rows, :] + a_scr[rows, :] * h_prev
            ys.append(h * gate[half * 8:(half + 1) * 8, :])
            h_prev = h[7:8, :]
        o_ref[pl.ds(base, BF16_SUBLANES), :] = jnp.concatenate(ys, axis=0).astype(o_ref.dtype)
        return h_prev

    h_last = lax.fori_loop(0, tt // BF16_SUBLANES, carry_step, hc_scr[0:1, :])
    hc_scr[...] = jnp.broadcast_to(h_last, (8, W))


def _rglru(z, cw, cb, wab, ba, bx, lam, *, B, S, tt=512):
    T = z.shape[0]
    W = RNN_WIDTH
    nt = S // tt
    return pl.pallas_call(
        functools.partial(_rglru_kernel, tt=tt),
        grid=(B, nt),
        in_specs=[
            pl.BlockSpec((tt, W), lambda b, t: (b * nt + t, OFF_XR // W)),
            pl.BlockSpec((tt, W), lambda b, t: (b * nt + t, OFF_RG // W)),
            pl.BlockSpec((CONV_WIDTH, W), lambda b, t: (0, 0)),
            pl.BlockSpec((1, W), lambda b, t: (0, 0)),
            pl.BlockSpec((W // RNN_BLOCK, RNN_BLOCK, 2 * RNN_BLOCK), lambda b, t: (0, 0, 0)),
            pl.BlockSpec((1, W), lambda b, t: (0, 0)),
            pl.BlockSpec((1, W), lambda b, t: (0, 0)),
            pl.BlockSpec((1, W), lambda b, t: (0, 0)),
        ],
        out_specs=pl.BlockSpec((tt, W), lambda b, t: (b * nt + t, 0)),
        out_shape=jax.ShapeDtypeStruct((T, W), BF16),
        scratch_shapes=[
            pltpu.VMEM((tt + 8, W), F32),
            pltpu.VMEM((tt, W), F32),
            pltpu.VMEM((tt, W), F32),
            pltpu.VMEM((8, W), F32),
        ],
        compiler_params=_cparams(("parallel", "arbitrary")),
        name="rglru",
    )(z, z, cw, cb, wab, ba, bx, lam)


def _compress_hidden(x_ref, pe_ref, w1_ref, x_scr):
    DH = NSA_HEAD_DIM
    half = CMP_STRIDE * DH
    slots = x_ref.shape[0] // CMP_STRIDE
    x = x_ref[...].astype(F32)
    for c in range(NSA_KV_WIDTH // LANES):
        x_scr[c] = x[:, c * LANES:(c + 1) * LANES]
    phase = [[x_scr[c, pl.ds(l, slots, stride=CMP_STRIDE), :] for c in range(NSA_KV_WIDTH // LANES)]
             for l in range(CMP_STRIDE)]
    per_block = LANES // DH

    def group_rows(g):
        lanes = slice((g % per_block) * DH, (g % per_block + 1) * DH)
        return jnp.concatenate([phase[l][g // per_block][:, lanes] for l in range(CMP_STRIDE)], axis=1)

    hm = jnp.concatenate([group_rows(g) for g in range(NSA_KV_GROUPS)], axis=0).astype(BF16)
    rows = hm.shape[0]
    lo = jnp.dot(hm, w1_ref[0:half, :], preferred_element_type=F32)
    hi = jnp.dot(hm, w1_ref[half:2 * half, :], preferred_element_type=F32)
    pe = jnp.broadcast_to(pe_ref[...], (8, 2 * half)).astype(BF16)
    pe_term = jnp.dot(pe, w1_ref[...], preferred_element_type=F32)[0:1, :]
    return _gelu(lo + pltpu.roll(hi, rows - 1, axis=0) + pe_term).astype(BF16)


def _compress_kernel(xk_ref, xv_ref, pek_ref, pev_ref, wk1_ref, wk2_ref, wv1_ref, wv2t_ref, kc_ref, vct_ref, x_scr):
    kc_ref[...] = jnp.dot(_compress_hidden(xk_ref, pek_ref, wk1_ref, x_scr), wk2_ref[...],
                          preferred_element_type=F32)
    vct_ref[...] = lax.dot_general(wv2t_ref[...], _compress_hidden(xv_ref, pev_ref, wv1_ref, x_scr), _NT,
                                   preferred_element_type=F32)


def _compress(z, pek, pev, wk1, wk2, wv1, wv2t, *, B, S):
    G, DH = NSA_KV_GROUPS, NSA_HEAD_DIM
    slots = S // CMP_STRIDE
    KVW = NSA_KV_WIDTH
    full = lambda shape: pl.BlockSpec(shape, lambda b: (0,) * len(shape))
    return pl.pallas_call(
        _compress_kernel,
        grid=(B,),
        in_specs=[pl.BlockSpec((S, KVW), lambda b: (b, OFF_KC // KVW)),
                  pl.BlockSpec((S, KVW), lambda b: (b, OFF_KC // KVW + 1)),
                  full(pek.shape), full(pev.shape), full(wk1.shape), full(wk2.shape), full(wv1.shape),
                  full(wv2t.shape)],
        out_specs=[pl.BlockSpec((G * slots, DH), lambda b: (b, 0)), pl.BlockSpec((DH, G * slots), lambda b: (0, b))],
        out_shape=[jax.ShapeDtypeStruct((B * G * slots, DH), F32), jax.ShapeDtypeStruct((DH, B * G * slots), F32)],
        scratch_shapes=[pltpu.VMEM((KVW // LANES, S, LANES), F32)],
        compiler_params=_cparams(("parallel",)),
        name="nsa_compress",
    )(z, z, pek, pev, wk1, wk2, wv1, wv2t)


def _nsa_kernel(q_ref, kc_ref, vct_ref, ks_ref, vs_ref, kw_ref, vw_ref, gate_ref, o_ref,
                ks_scr, kw_scr, vs_scr, vw_scr, *tile_scratch, tq, S):
    DH = NSA_HEAD_DIM
    tk = tq
    v_rows = DH + BF16_SUBLANES
    g_idx = pl.program_id(1)

    for g in range(NSA_KV_GROUPS):
        @pl.when(g_idx == g)
        def _():
            lanes = slice(g * DH, (g + 1) * DH)
            ks_scr[...] = ks_ref[:, lanes]
            kw_scr[...] = kw_ref[:, lanes]
            ones = jnp.ones((BF16_SUBLANES, tk), BF16)
            for v_ref, v_scr in ((vs_ref, vs_scr), (vw_ref, vw_scr)):
                vt = v_ref[:, lanes].T
                for kt in range(S // tk):
                    v_scr[kt, 0:DH, :] = vt[:, kt * tk:(kt + 1) * tk]
                    v_scr[kt, DH:v_rows, :] = ones

    def q_tile(qi, carry):
        rows = pl.ds(pl.multiple_of(qi * tq, tq), tq)
        _nsa_q_tile(qi, g_idx, q_ref.at[rows, :], kc_ref, vct_ref, gate_ref.at[rows, :], o_ref.at[rows, :],
                    ks_scr, kw_scr, vs_scr, vw_scr, *tile_scratch, tq=tq, S=S)
        return carry

    lax.fori_loop(0, S // tq, q_tile, 0)


def _nsa_q_tile(qi, g_idx, q_ref, kc_ref, vct_ref, gate_ref, o_ref,
                ks_scr, kw_scr, vs_scr, vw_scr, gt_scr, ps_scr, sel_scr, sa_scr, sb_scr, wa_scr, wb_scr, wc_scr,
                m_scr, acc_scr, *, tq, S):
    HP, DH = NSA_HPG, NSA_HEAD_DIM
    M = HP * tq
    tk = tq
    n_sel = S // SEL_BLOCK
    slots = kc_ref.shape[0]
    v_rows = DH + BF16_SUBLANES
    t0 = qi * tq

    def per_head(x):
        return jnp.concatenate([x] * HP, axis=1)

    q_t = q_ref[...].T
    qt = jnp.concatenate([q_t[h * DH:(h + 1) * DH, :] for h in range(HP)], axis=1)

    def scores(k_scr, kt, dst):
        start = pl.multiple_of(kt * tk, tk)
        dst[:, 0:M] = jnp.dot(k_scr[pl.ds(start, tk), :], qt, preferred_element_type=F32)

    sc = jnp.dot(kc_ref[...].astype(BF16), qt, preferred_element_type=F32)
    scores(ks_scr, 0, sa_scr)
    n_back = WINDOW // tk
    win_bufs = (wa_scr, wb_scr, wc_scr)
    for d in range(n_back + 1):
        scores(kw_scr, jnp.maximum(qi - d, 0), win_bufs[d])
    n_idx = lax.broadcasted_iota(jnp.int32, (slots, tq), 0)
    t_idx = t0 + lax.broadcasted_iota(jnp.int32, (slots, tq), 1)
    valid = per_head(jnp.where(n_idx * CMP_STRIDE + (CMP_BLOCK - 1) <= t_idx, 1.0, 0.0)) > 0.5
    sc = jnp.where(valid, sc, NEG_INF)
    mx = jnp.max(sc, axis=0, keepdims=True)
    p = jnp.where(valid, jnp.exp2(sc - mx), 0.0)
    den = jnp.sum(p, axis=0, keepdims=True)
    p_c = p * (1.0 / jnp.where(den > 0.0, den, 1.0))
    o_cmp = jnp.dot(vct_ref[...].astype(BF16), p_c.astype(BF16), preferred_element_type=F32)

    p_sum = p_c[:, 0:tq]
    for h in range(1, HP):
        p_sum = p_sum + p_c[:, h * tq:(h + 1) * tq]
    per_sel = SEL_BLOCK // CMP_STRIDE
    for c in range(tq // LANES):
        ps_scr[c] = p_sum[:, c * LANES:(c + 1) * LANES]
    every = [jnp.concatenate([ps_scr[c, pl.ds(r, n_sel, stride=per_sel), :] for c in range(tq // LANES)], axis=1)
             for r in range(per_sel)]
    j_idx = lax.broadcasted_iota(jnp.int32, (n_sel, tq), 0)
    before = jnp.where(j_idx >= 1, pltpu.roll(every[per_sel - 1], 1, axis=0), 0.0)
    imp = every[0]
    for r in range(1, per_sel - 1):
        imp = imp + every[r]
    imp = imp + 0.5 * every[per_sel - 1] + 0.5 * before
    cur = (t0 + lax.broadcasted_iota(jnp.int32, (n_sel, tq), 1)) // SEL_BLOCK
    forced = (j_idx == 0) | (j_idx == cur) | (j_idx == cur - 1)
    imp = jnp.where(forced, FORCE_SCORE, jnp.where(j_idx > cur, NEG_INF, imp))
    rank = jnp.zeros((n_sel, tq), F32)
    for i in range(n_sel):
        ci = imp[i:i + 1, :]
        ge = jnp.where(ci >= imp, 1.0, 0.0)
        gt = jnp.where(ci > imp, 1.0, 0.0)
        rank = rank + jnp.where(j_idx > i, ge, gt)
    sel_scr[...] = jnp.where(rank < float(min(SEL_TOP_N, n_sel)), 1.0, 0.0)

    per_tile = tk // SEL_BLOCK
    kloc = lax.broadcasted_iota(jnp.int32, (tk, tq), 0)
    tloc = lax.broadcasted_iota(jnp.int32, (tk, tq), 1)
    key_ahead = kloc - tloc

    def slc_bias(kt):
        hit = jnp.concatenate(
            [jnp.broadcast_to(sel_scr[pl.ds(kt * per_tile + jj, 1), :], (SEL_BLOCK, tq)) for jj in range(per_tile)],
            axis=0)
        return jnp.where((hit > 0.5) & (key_ahead <= (qi - kt) * tk), 0.0, NEG_INF)

    def absorb(v_ext, src, bias):
        s = src[:, 0:M] if bias is None else src[:, 0:M] + per_head(bias)
        m_old = m_scr[...]
        m_new = jnp.maximum(m_old, jnp.max(s, axis=0, keepdims=True))
        alpha = jnp.exp2(m_old - m_new)
        pr = jnp.exp2(s - m_new).astype(BF16)
        m_scr[...] = m_new
        acc_scr[...] = alpha * acc_scr[...] + jnp.dot(v_ext, pr, preferred_element_type=F32)

    def reset():
        m_scr[...] = jnp.full((1, M), NEG_INF, F32)
        acc_scr[...] = jnp.zeros((v_rows, M), F32)

    def result():
        acc = acc_scr[...]
        return acc[0:DH, :] * (1.0 / acc[DH:DH + 1, :])

    reset()
    n_tiles = qi + 1
    slc_bufs = (sa_scr, sb_scr)

    def slc_run(first, count, look_ahead):
        for u in range(count):
            if look_ahead or u + 1 < count:
                scores(ks_scr, jnp.minimum(first + u + 1, qi), slc_bufs[(u + 1) % 2])
            absorb(vs_scr[first + u], slc_bufs[u % 2], slc_bias(first + u))

    def slc_trip(j, carry):
        slc_run(SLC_UNROLL * j, SLC_UNROLL, True)
        return carry

    lax.fori_loop(0, n_tiles // SLC_UNROLL, slc_trip, 0)
    for rem in range(1, SLC_UNROLL):
        @pl.when(n_tiles % SLC_UNROLL == rem)
        def _():
            slc_run(n_tiles - rem, rem, False)

    o_slc = result()

    reset()
    diag_bias = jnp.where(kloc <= tloc, 0.0, NEG_INF)

    def window_start(n_win):
        for d in range(n_win):
            absorb(vw_scr[qi - d], win_bufs[d], diag_bias if d == 0 else None)

    def window_full():
        near = per_head(jnp.where(kloc <= tloc, 1.0, 0.0)) > 0.5
        s = jnp.where(near, win_bufs[0][:, 0:M], win_bufs[n_back][:, 0:M])
        m_old = m_scr[...]
        m_new = jnp.maximum(m_old, jnp.max(s, axis=0, keepdims=True))
        alpha = jnp.exp2(m_old - m_new)
        pr = jnp.exp2(s - m_new).astype(BF16)
        m_scr[...] = m_new
        none = jnp.zeros_like(pr)
        acc_scr[...] = (alpha * acc_scr[...]
                        + jnp.dot(vw_scr[qi], jnp.where(near, pr, none), preferred_element_type=F32)
                        + jnp.dot(vw_scr[qi - n_back], jnp.where(near, none, pr), preferred_element_type=F32))
        absorb(vw_scr[qi - 1], win_bufs[1], None)

    for n_win in range(1, n_back + 1):
        @pl.when(qi == n_win - 1)
        def _():
            window_start(n_win)

    @pl.when(qi >= n_back)
    def _():
        window_full()

    o_win = result()

    gt_scr[...] = gate_ref[...].astype(F32).T

    def gate(branch):
        rows = [gt_scr[pl.ds(g_idx * (HP * N_BRANCH) + h * N_BRANCH + branch, 1), :] for h in range(HP)]
        return _sigmoid(jnp.concatenate(rows, axis=1))

    o = gate(0) * o_cmp + gate(1) * o_slc + gate(2) * o_win
    o_ref[...] = jnp.concatenate([o[:, h * tq:(h + 1) * tq].T for h in range(HP)], axis=1).astype(o_ref.dtype)


def _nsa(z, kc, vct, *, B, S, tq=256):
    G, HP, DH = NSA_KV_GROUPS, NSA_HPG, NSA_HEAD_DIM
    assert WINDOW == 2 * tq and CMP_BLOCK == 2 * CMP_STRIDE and S % tq == 0, (WINDOW, tq, S)
    T = B * S
    slots = S // CMP_STRIDE
    n_sel = S // SEL_BLOCK
    n_kt = S // tq
    v_rows = DH + BF16_SUBLANES
    KVW = NSA_KV_WIDTH

    def kv_spec(which):
        return pl.BlockSpec((S, KVW), lambda b, g: (b, OFF_KV // KVW + which))

    return pl.pallas_call(
        functools.partial(_nsa_kernel, tq=tq, S=S),
        grid=(B, G),
        in_specs=[
            pl.BlockSpec((S, HP * DH), lambda b, g: (b, OFF_Q // (HP * DH) + g)),
            pl.BlockSpec((slots, DH), lambda b, g: (b * G + g, 0)),
            pl.BlockSpec((DH, slots), lambda b, g: (0, b * G + g)),
            kv_spec(0), kv_spec(1), kv_spec(2), kv_spec(3),
            pl.BlockSpec((S, LANES), lambda b, g: (b, OFF_NG // LANES)),
        ],
        out_specs=pl.BlockSpec((S, HP * DH), lambda b, g: (b, g)),
        out_shape=jax.ShapeDtypeStruct((T, NSA_Q_WIDTH), BF16),
        scratch_shapes=[
            pltpu.VMEM((S, DH), BF16),
            pltpu.VMEM((S, DH), BF16),
            pltpu.VMEM((n_kt, v_rows, tq), BF16),
            pltpu.VMEM((n_kt, v_rows, tq), BF16),
            pltpu.VMEM((LANES, tq), F32),
            pltpu.VMEM((tq // LANES, slots, LANES), F32),
            pltpu.VMEM((n_sel, tq), F32),
            pltpu.VMEM((tq, HP * tq + LANES), F32),
            pltpu.VMEM((tq, HP * tq + LANES), F32),
            pltpu.VMEM((tq, HP * tq + LANES), F32),
            pltpu.VMEM((tq, HP * tq + LANES), F32),
            pltpu.VMEM((tq, HP * tq + LANES), F32),
            pltpu.VMEM((1, HP * tq), F32),
            pltpu.VMEM((v_rows, HP * tq), F32),
        ],
        compiler_params=_cparams(("parallel", "parallel")),
        name="nsa_attention",
    )(z, kc, vct, z, z, z, z, z)


def _merge_kernel(ya_ref, yb_ref, yc_ref, mga_ref, mgb_ref, mgc_ref, x_ref, wa_ref, wb_ref, wc_ref, wo_ref,
                  gpost_ref, gpre_ref, x1_ref, hf_ref):
    gate = lambda ref: _sigmoid(ref[...].astype(F32))
    merged = gate(mga_ref) * jnp.dot(ya_ref[...], wa_ref[...], preferred_element_type=F32)
    merged = merged + gate(mgb_ref) * jnp.dot(yb_ref[...], wb_ref[...], preferred_element_type=F32)
    merged = merged + gate(mgc_ref) * jnp.dot(yc_ref[...], wc_ref[...], preferred_element_type=F32)
    y = jnp.dot(merged.astype(BF16), wo_ref[...], preferred_element_type=F32)
    x1 = x_ref[...] + _rms(y, gpost_ref[...])
    x1_ref[...] = x1
    hf_ref[...] = _rms(x1, gpre_ref[...]).astype(BF16)


def _merge(ya, yb, yc, z, x2d, wa, wb, wc, wo, layer, gpost, gpre, *, tm=512):
    T, D = x2d.shape
    row = lambda c: pl.BlockSpec((tm, D), lambda i: (i, c))
    wfull = pl.BlockSpec((None, D, D), lambda i: (layer, 0, 0))
    vec = pl.BlockSpec((1, D), lambda i: (0, 0))
    mg0 = OFF_MG // D
    return pl.pallas_call(
        _merge_kernel,
        grid=(T // tm,),
        in_specs=[row(0), row(0), row(0), row(mg0), row(mg0 + 1), row(mg0 + 2), row(0),
                  wfull, wfull, wfull, wfull, vec, vec],
        out_specs=[row(0), row(0)],
        out_shape=[jax.ShapeDtypeStruct((T, D), F32), jax.ShapeDtypeStruct((T, D), BF16)],
        compiler_params=_cparams(("parallel",)),
        name="merge",
    )(ya, yb, yc, z, z, z, x2d, wa, wb, wc, wo, gpost, gpre)


def _ffn_kernel(hf_ref, x1_ref, win_ref, wout_ref, gpost_ref, o_ref, acc_scr, *, bounds):
    hf = hf_ref[...]
    for c, (lo, hi) in enumerate(zip(bounds[:-1], bounds[1:])):
        gate = jnp.dot(hf, win_ref[:, lo:hi], preferred_element_type=F32)
        up = jnp.dot(hf, win_ref[:, D_FF + lo:D_FF + hi], preferred_element_type=F32)
        act = (gate * _sigmoid(gate) * up).astype(BF16)
        part = jnp.dot(act, wout_ref[lo:hi, :], preferred_element_type=F32)
        if c == 0:
            acc_scr[...] = part
        else:
            acc_scr[...] += part
    o_ref[...] = x1_ref[...] + _rms(acc_scr[...], gpost_ref[...])


def _ffn(hf, x1, win, wout, layer, gpost, *, tm=512):
    T, D = x1.shape
    blocks = D_FF // V7X_MXU_WIDTH
    bounds = (0, (blocks + 1) // 2 * V7X_MXU_WIDTH, D_FF)
    row = pl.BlockSpec((tm, D), lambda i: (i, 0))
    return pl.pallas_call(
        functools.partial(_ffn_kernel, bounds=bounds),
        grid=(T // tm,),
        in_specs=[row, row,
                  pl.BlockSpec((None,) + win.shape[1:], lambda i: (layer, 0, 0), pipeline_mode=pl.Buffered(1)),
                  pl.BlockSpec((None,) + wout.shape[1:], lambda i: (layer, 0, 0), pipeline_mode=pl.Buffered(1)),
                  pl.BlockSpec((1, D), lambda i: (0, 0))],
        out_specs=row,
        out_shape=jax.ShapeDtypeStruct((T, D), F32),
        scratch_shapes=[pltpu.VMEM((tm, D), F32)],
        compiler_params=_cparams(("parallel",)),
        name="ffn",
    )(hf, x1, win, wout, gpost)


_SRC = [int(c) for c in np.cumsum([0, GMLP_WIDTH, GMLP_WIDTH, NSA_Q_WIDTH, 6 * NSA_KV_WIDTH, N_BRANCH * NSA_HEADS,
                                   RNN_WIDTH, RNN_WIDTH, N_BRANCH * D_MODEL])]


def _reorder_w_in_kernel(w_ref, o_ref):
    u0, _, q0, kv0, ng0, xr0, _, _, end = _SRC
    rest0 = kv0 + 2 * NSA_KV_WIDTH

    def put(dst, lo, hi, scale=None):
        x = w_ref[lo:hi, :]
        if scale is not None:
            x = x * scale
        o_ref[dst:dst + hi - lo, :] = x.astype(o_ref.dtype)

    put(OFF_U, u0, q0)
    put(OFF_XR, xr0, end)
    put(OFF_KC, kv0, rest0)
    put(OFF_NG, ng0, xr0)
    pad0 = OFF_NG + xr0 - ng0
    o_ref[pad0:OFF_Q, :] = jnp.zeros((OFF_Q - pad0, o_ref.shape[1]), o_ref.dtype)
    put(OFF_Q, q0, kv0, Q_SCALE)
    put(OFF_KV, rest0, ng0)


def _reorder_w_in(w_in, *, td=256):
    L, D, d_in = w_in.shape
    return pl.pallas_call(
        _reorder_w_in_kernel,
        grid=(L, D // td),
        in_specs=[pl.BlockSpec((None, d_in, td), lambda l, i: (l, 0, i))],
        out_specs=pl.BlockSpec((None, D_IN_PAD, td), lambda l, i: (l, 0, i)),
        out_shape=jax.ShapeDtypeStruct((L, D_IN_PAD, D), BF16),
        compiler_params=_cparams(("parallel", "parallel")),
        name="reorder_w_in",
    )(jnp.swapaxes(w_in, 1, 2))


def _block_diag_gates(wa, wx):
    per = RNN_BLOCK // RNN_HEAD_DIM
    nblk = RNN_HEADS // per
    eye = jnp.eye(per, dtype=wa.dtype)

    def bd(w):
        w = w.reshape(nblk, per, RNN_HEAD_DIM, RNN_HEAD_DIM)
        return jnp.einsum('kpio,pq->kpiqo', w, eye).reshape(nblk, RNN_BLOCK, RNN_BLOCK)

    return jnp.concatenate([bd(wa), bd(wx)], axis=-1).astype(BF16)


def _layer(x2d, B, S, wt_in_all, dense_all, layer, g_pre_mix, g_post_mix, g_pre_ffn, g_post_ffn,
           gmlp_ln_g, gmlp_ln_b, gmlp_ws, gmlp_bs,
           nsa_pe_k, nsa_pe_v, nsa_wk1, nsa_wk2, nsa_wv1, nsa_wv2,
           rnn_conv_w, rnn_conv_b, rnn_wa, rnn_ba, rnn_wx, rnn_bx, rnn_lam):
    row = lambda a: a.reshape(1, -1)
    z = _in_proj(x2d, row(g_pre_mix), wt_in_all, layer)

    y_a = _gmlp(z, row(gmlp_ln_g), row(gmlp_ln_b), gmlp_ws, gmlp_bs.T)
    y_c = _rglru(z, rnn_conv_w, row(rnn_conv_b), _block_diag_gates(rnn_wa, rnn_wx),
                 row(rnn_ba), row(rnn_bx), row(rnn_lam), B=B, S=S)

    kc, vct = _compress(z, nsa_pe_k.reshape(1, -1), nsa_pe_v.reshape(1, -1), nsa_wk1.astype(BF16),
                        nsa_wk2.astype(BF16), nsa_wv1.astype(BF16), nsa_wv2.T.astype(BF16), B=B, S=S)
    y_b = _nsa(z, kc, vct, B=B, S=S)

    x1, hf = _merge(y_a, y_b, y_c, z, x2d, *dense_all[:4], layer, row(g_post_mix), row(g_pre_ffn))
    return _ffn(hf, x1, *dense_all[4:], layer, row(g_post_ffn))


def kernel(x, g_pre_mix, g_post_mix, g_pre_ffn, g_post_ffn, w_in, gmlp_ln_g, gmlp_ln_b, gmlp_ws, gmlp_bs, nsa_pe_k, nsa_pe_v, nsa_wk1, nsa_wk2, nsa_wv1, nsa_wv2, rnn_conv_w, rnn_conv_b, rnn_wa, rnn_ba, rnn_wx, rnn_bx, rnn_lam, w_br_a, w_br_b, w_br_c, w_o, w_ffn_in, w_ffn_out):
    B, S, D = x.shape
    params = (g_pre_mix, g_post_mix, g_pre_ffn, g_post_ffn, gmlp_ln_g, gmlp_ln_b, gmlp_ws, gmlp_bs,
              nsa_pe_k, nsa_pe_v, nsa_wk1, nsa_wk2, nsa_wv1, nsa_wv2,
              rnn_conv_w, rnn_conv_b, rnn_wa, rnn_ba, rnn_wx, rnn_bx, rnn_lam)
    wt_in_all = _reorder_w_in(w_in)
    dense_all = tuple(w.astype(BF16) for w in (w_br_a, w_br_b, w_br_c, w_o, w_ffn_in, w_ffn_out))
    x2d = x.reshape(B * S, D)
    for l in range(w_in.shape[0]):
        x2d = _layer(x2d, B, S, wt_in_all, dense_all, l, *(p[l] for p in params))
    return x2d.reshape(B, S, D)
```
